```python
import math
import jax, jax.numpy as jnp
from jax import lax
import numpy as np


D_MODEL = 2048
BATCH = 4
SEQ = 2048
DEPTH = 4
DEC_BATCH = 128
DEC_SEQ = 4
PAST_LEN = 16384
PAGE_SIZE = 128

N_MIXERS = 2
N_RWKV = (DEPTH + 1) // 2
N_HGRN = DEPTH // 2
RWKV_HEAD = 64
RWKV_HEADS = D_MODEL // RWKV_HEAD
RWKV_DECAY_LORA = max(32, int(round(1.8 * D_MODEL ** 0.5 / 32)) * 32)
RWKV_AAA_LORA = max(32, int(round(1.8 * D_MODEL ** 0.5 / 32)) * 32)
RWKV_MV_LORA = max(32, int(round(1.3 * D_MODEL ** 0.5 / 32)) * 32)
RWKV_GN_EPS = 1e-5 * RWKV_HEAD
N_SHIFT_MIX = 6
HGRN_EXPAND = 128
HGRN_HEADS = D_MODEL // HGRN_EXPAND
HGRN_DK = HGRN_EXPAND
HGRN_DV = D_MODEL // HGRN_HEADS
HGRN_CHUNK = 64
NORM_EPS = 1e-6
GATE_FLOOR = 1e-30

kernel_name = 'hybrid_rwkv7_hgrn2_decode_step'


def rms_norm(x, w):
    x32 = x.astype(jnp.float32)
    y = x32 * lax.rsqrt(jnp.mean(x32 * x32, axis=-1, keepdims=True) + NORM_EPS)
    return (y * w.astype(jnp.float32)).astype(x.dtype)


def rwkv7_mix(x, shift_prev, wkv_prev, v_first, vres, mu, w_in, w0, w1, w2, a0, a1, a2,
              k_k, k_a, r_k, ln_w, ln_b, w_o):
    B, T, D = x.shape
    f32 = jnp.float32
    x_prev = jnp.concatenate([shift_prev[:, None, :].astype(x.dtype), x[:, :-1]], axis=1)
    xs = x[:, :, None, :] + (x_prev - x)[:, :, None, :] * mu
    proj = jnp.einsum('btnd,nde->btne', xs[:, :, :4], w_in)
    r, k, v, z = proj[:, :, 0], proj[:, :, 1], proj[:, :, 2], proj[:, :, 3]
    xv, xw, xa = xs[:, :, 2], xs[:, :, 4], xs[:, :, 5]
    w_log = -jax.nn.softplus(-(w0 + jnp.tanh(xw @ w1) @ w2)) - 0.5
    decay = jnp.exp(-jnp.exp(w_log.astype(f32)))
    if vres is None:
        v_first = v
    else:
        v0, v1, v2 = vres
        v = v + (v_first - v) * jax.nn.sigmoid(v0 + (xv @ v1) @ v2)
    a = jax.nn.sigmoid(a0 + (xa @ a1) @ a2)
    heads = lambda t: t.reshape(B, T, RWKV_HEADS, RWKV_HEAD).astype(f32)
    kk = heads(k * k_k)
    kk = kk / jnp.maximum(jnp.sqrt(jnp.sum(kk * kk, axis=-1, keepdims=True)), 1e-12)
    k = k * (1.0 + (a - 1.0) * k_a)
    rh, kh, vh, ah, wh = heads(r), heads(k), heads(v), heads(a), heads(decay)
    bh = kk * ah

    def step(S, inp):
        r_t, w_t, k_t, v_t, kk_t, b_t = inp
        sa = jnp.einsum('bhvk,bhk->bhv', S, -kk_t)
        S = S * w_t[:, :, None, :] + sa[..., None] * b_t[:, :, None, :] + v_t[..., None] * k_t[:, :, None, :]
        return S, jnp.einsum('bhvk,bhk->bhv', S, r_t)

    tm = lambda t: jnp.moveaxis(t, 1, 0)
    S_fin, y = lax.scan(step, wkv_prev.astype(f32), (tm(rh), tm(wh), tm(kh), tm(vh), tm(kk), tm(bh)))
    y = jnp.moveaxis(y, 0, 1)
    mean = jnp.mean(y, axis=-1, keepdims=True)
    var = jnp.mean(jnp.square(y - mean), axis=-1, keepdims=True)
    y = ((y - mean) * lax.rsqrt(var + RWKV_GN_EPS)).reshape(B, T, D) * ln_w + ln_b
    bonus = jnp.sum(rh * kh * r_k, axis=-1, keepdims=True) * vh
    y = y + bonus.reshape(B, T, D)
    out = (y.astype(x.dtype) * jax.nn.silu(z)) @ w_o
    return out, x[:, -1], S_fin, v_first


def hgrn2_mix(x, S_prev, lb, w_in, norm_w, w_o):
    B, T, D = x.shape
    f32 = jnp.float32
    q, f_logit, i, z = jnp.split(x @ w_in, 4, axis=-1)
    f_logit = f_logit.astype(f32)
    f = lb + (1.0 - lb) * jax.nn.sigmoid(f_logit)
    log_f = jnp.log(jnp.maximum(f, GATE_FLOOR))
    k = (1.0 - lb) * jax.nn.sigmoid(-f_logit)
    q = jax.nn.silu(q.astype(f32)) * HGRN_DK ** -0.5
    c = math.gcd(T, HGRN_CHUNK)
    nc = T // c
    chunks = lambda t, d: t.reshape(B, nc, c, HGRN_HEADS, d).transpose(1, 0, 3, 2, 4)
    qc, kc = chunks(q, HGRN_DK), chunks(k, HGRN_DK)
    vc = chunks(i.astype(f32), HGRN_DV)
    gc = jnp.cumsum(chunks(log_f, HGRN_DK), axis=3)
    mask = jnp.tril(jnp.ones((c, c), dtype=bool))[:, :, None]

    def chunk_step(S, inp):
        q_c, k_c, v_c, G = inp
        diff = G[:, :, :, None, :] - G[:, :, None, :, :]
        dec = jnp.where(mask, jnp.exp(jnp.minimum(diff, 0.0)), 0.0)
        A = jnp.einsum('bhtd,bhsd,bhtsd->bhts', q_c, k_c, dec)
        o = jnp.einsum('bhts,bhse->bhte', A, v_c) + jnp.einsum('bhtd,bhde->bhte', q_c * jnp.exp(G), S)
        G_last = G[:, :, -1:, :]
        S = jnp.exp(G_last[:, :, 0, :, None]) * S + jnp.einsum('bhsd,bhse->bhde', k_c * jnp.exp(G_last - G), v_c)
        return S, o

    S_fin, o = lax.scan(chunk_step, S_prev.astype(f32), (qc, kc, vc, gc))
    o = o.transpose(1, 0, 3, 2, 4).reshape(B, T, HGRN_HEADS, HGRN_DV)
    o = o * lax.rsqrt(jnp.mean(o * o, axis=-1, keepdims=True) + NORM_EPS) * norm_w
    out = (o.reshape(B, T, D).astype(x.dtype) * jax.nn.silu(z)) @ w_o
    return out, S_fin


def setup_inputs(seed: int = 0) -> dict:
    key = jax.random.key(seed)
    ks = jax.random.split(key, 32)
    f32 = jnp.float32
    D = D_MODEL
    nrm = lambda kk, shape, s: s * jax.random.normal(kk, shape, f32)
    return {
        'x_prompt': nrm(ks[0], (BATCH, SEQ, D), 1.0),
        'x_sample': nrm(ks[1], (DEC_BATCH, DEC_SEQ, D), 1.0),
        'state_rwkv_shift': nrm(ks[2], (N_RWKV, DEC_BATCH, D), 1.0),
        'state_rwkv_wkv': nrm(ks[3], (N_RWKV, DEC_BATCH, RWKV_HEADS, RWKV_HEAD, RWKV_HEAD), 0.5),
        'state_hgrn': nrm(ks[4], (N_HGRN, DEC_BATCH, HGRN_HEADS, HGRN_DK, HGRN_DV), 0.5),
        'norm_pre': 1.0 + nrm(ks[5], (DEPTH, D), 0.02),
        'norm_post': 1.0 + nrm(ks[6], (DEPTH, D), 0.02),
        'rwkv_mu': jax.random.uniform(ks[7], (N_RWKV, N_SHIFT_MIX, D), f32),
        'rwkv_w_in': nrm(ks[8], (N_RWKV, 4, D, D), D ** -0.5),
        'rwkv_w0': jax.random.uniform(ks[9], (N_RWKV, D), f32, -4.0, 1.0),
        'rwkv_w1': nrm(ks[10], (N_RWKV, D, RWKV_DECAY_LORA), D ** -0.5),
        'rwkv_w2': nrm(ks[11], (N_RWKV, RWKV_DECAY_LORA, D), 0.1 * RWKV_DECAY_LORA ** -0.5),
        'rwkv_a0': nrm(ks[12], (N_RWKV, D), 0.1),
        'rwkv_a1': nrm(ks[13], (N_RWKV, D, RWKV_AAA_LORA), D ** -0.5),
        'rwkv_a2': nrm(ks[14], (N_RWKV, RWKV_AAA_LORA, D), RWKV_AAA_LORA ** -0.5),
        'rwkv_v0': nrm(ks[15], (N_RWKV - 1, D), 0.1),
        'rwkv_v1': nrm(ks[16], (N_RWKV - 1, D, RWKV_MV_LORA), D ** -0.5),
        'rwkv_v2': nrm(ks[17], (N_RWKV - 1, RWKV_MV_LORA, D), RWKV_MV_LORA ** -0.5),
        'rwkv_k_k': 0.85 + nrm(ks[18], (N_RWKV, D), 0.05),
        'rwkv_k_a': 1.0 + nrm(ks[19], (N_RWKV, D), 0.05),
        'rwkv_r_k': nrm(ks[20], (N_RWKV, RWKV_HEADS, RWKV_HEAD), 0.1),
        'rwkv_ln_w': 1.0 + nrm(ks[21], (N_RWKV, D), 0.02),
        'rwkv_ln_b': nrm(ks[22], (N_RWKV, D), 0.01),
        'rwkv_w_o': nrm(ks[23], (N_RWKV, D, D), D ** -0.5),
        'hgrn_w_in': nrm(ks[24], (N_HGRN, D, 4 * D), D ** -0.5),
        'hgrn_lb_logits': nrm(ks[25], (N_HGRN, HGRN_HEADS * HGRN_DK), 0.5),
        'hgrn_norm_w': 1.0 + nrm(ks[26], (N_HGRN, HGRN_DV), 0.02),
        'hgrn_w_o': nrm(ks[27], (N_HGRN, D, D), D ** -0.5),
    }


def reference(x_prompt, x_sample, state_rwkv_shift, state_rwkv_wkv, state_hgrn,
              norm_pre, norm_post, rwkv_mu, rwkv_w_in, rwkv_w0, rwkv_w1, rwkv_w2,
              rwkv_a0, rwkv_a1, rwkv_a2, rwkv_v0, rwkv_v1, rwkv_v2, rwkv_k_k, rwkv_k_a,
              rwkv_r_k, rwkv_ln_w, rwkv_ln_b, rwkv_w_o, hgrn_w_in, hgrn_lb_logits,
              hgrn_norm_w, hgrn_w_o):
    lb_soft = jax.nn.softmax(hgrn_lb_logits.astype(jnp.float32), axis=0)
    lower_bounds = jnp.cumsum(lb_soft, axis=0) - lb_soft[0]

    def trunk(x, shift0, wkv0, hgrn0):
        shifts, wkvs, hgrns = [], [], []
        v_first = None
        for layer in range(DEPTH):
            j = layer // N_MIXERS
            h = rms_norm(x, norm_pre[layer])
            if layer % N_MIXERS == 0:
                vres = None if j == 0 else (rwkv_v0[j - 1], rwkv_v1[j - 1], rwkv_v2[j - 1])
                out, sh, st, v_first = rwkv7_mix(
                    h, shift0[j], wkv0[j], v_first, vres, rwkv_mu[j], rwkv_w_in[j],
                    rwkv_w0[j], rwkv_w1[j], rwkv_w2[j], rwkv_a0[j], rwkv_a1[j], rwkv_a2[j],
                    rwkv_k_k[j], rwkv_k_a[j], rwkv_r_k[j], rwkv_ln_w[j], rwkv_ln_b[j], rwkv_w_o[j])
                shifts.append(sh.astype(x.dtype))
                wkvs.append(st.astype(x.dtype))
            else:
                out, st = hgrn2_mix(h, hgrn0[j], lower_bounds[j], hgrn_w_in[j], hgrn_norm_w[j], hgrn_w_o[j])
                hgrns.append(st.astype(x.dtype))
            x = x + rms_norm(out, norm_post[layer])
        return x, jnp.stack(shifts), jnp.stack(wkvs), jnp.stack(hgrns)

    B = x_prompt.shape[0]
    dt = x_prompt.dtype
    zero_shift = jnp.zeros((N_RWKV, B, D_MODEL), dt)
    zero_wkv = jnp.zeros((N_RWKV, B, RWKV_HEADS, RWKV_HEAD, RWKV_HEAD), dt)
    zero_hgrn = jnp.zeros((N_HGRN, B, HGRN_HEADS, HGRN_DK, HGRN_DV), dt)
    y_prompt, p_shift, p_wkv, p_hgrn = trunk(x_prompt, zero_shift, zero_wkv, zero_hgrn)
    y_sample, s_shift, s_wkv, s_hgrn = trunk(x_sample, state_rwkv_shift, state_rwkv_wkv, state_hgrn)
    return (y_prompt, y_sample, p_shift, p_wkv, p_hgrn, s_shift, s_wkv, s_hgrn)
```

```python
import functools
import math

import jax
import jax.numpy as jnp
from jax import lax
from jax.experimental import pallas as pl
from jax.experimental.pallas import tpu as pltpu

F32 = jnp.float32
BF16 = jnp.bfloat16

D_MODEL = 2048
DEPTH = 4
RWKV_HEAD = 64
HGRN_DK = 128
RWKV_GN_EPS = 1e-5 * RWKV_HEAD
NORM_EPS = 1e-6
GATE_FLOOR = 1e-30

LANES = 128
GROUP = 256
HEADS_PER_GROUP = GROUP // RWKV_HEAD
LORA_PAD = 128
PROMPT_CHUNK = 64
SAMPLE_TPAD = 16
ROW_TILE = 256
VMEM_LIMIT = 48 * 1024 * 1024


def _mm(a, b):
    return jnp.dot(a, b, preferred_element_type=F32)


def _mm_nt(a, b):
    return lax.dot_general(a, b, (((1,), (1,)), ((), ())), preferred_element_type=F32)


def _mm_tn(a, b):
    return lax.dot_general(a, b, (((0,), (0,)), ((), ())), preferred_element_type=F32)


def _iota(shape, dim):
    return lax.broadcasted_iota(jnp.int32, shape, dim)


def _split2(x):
    hi = x.astype(BF16)
    lo = (x - hi.astype(F32)).astype(BF16)
    return hi, lo


def _segsum(x, ones_blk):
    hi, lo = _split2(x)
    return _mm(hi, ones_blk) + _mm(lo, ones_blk)


def _cumsum_rows(x):
    n = x.shape[0]
    tri = jnp.where(_iota((n, n), 0) >= _iota((n, n), 1), 1.0, 0.0).astype(BF16)
    x1 = x.astype(BF16)
    r1 = x - x1.astype(F32)
    x2 = r1.astype(BF16)
    x3 = (r1 - x2.astype(F32)).astype(BF16)
    return _mm(tri, x1) + _mm(tri, x2) + _mm(tri, x3)


def _block_diag(x, head_width, n_heads):
    shift = int(math.log2(head_width))
    lane_head = lax.shift_right_logical(_iota(x.shape, 1), shift)
    parts = [jnp.where(lane_head == h, x, 0.0).astype(BF16) for h in range(n_heads)]
    return jnp.concatenate(parts, axis=0)


def _sigmoid(x):
    return jax.nn.sigmoid(x)


def _softplus(x):
    return jnp.maximum(x, 0.0) + jnp.log(1.0 + jnp.exp(-jnp.abs(x)))


def _rmsnorm_kernel(x_ref, w_ref, o_ref):
    x = x_ref[...]
    ms = jnp.mean(x * x, axis=-1, keepdims=True)
    o_ref[...] = x * lax.rsqrt(ms + NORM_EPS) * w_ref[...]


def _rmsnorm(x, w):
    m, d = x.shape
    return pl.pallas_call(
        _rmsnorm_kernel,
        grid=(m // ROW_TILE,),
        in_specs=[pl.BlockSpec((ROW_TILE, d), lambda i: (i, 0)),
                  pl.BlockSpec((1, d), lambda i: (0, 0))],
        out_specs=pl.BlockSpec((ROW_TILE, d), lambda i: (i, 0)),
        out_shape=jax.ShapeDtypeStruct((m, d), F32),
        name="rmsnorm",
    )(x, w.reshape(1, d))


def _proj_mix_kernel(h_ref, hp_ref, mu_ref, w_ref, o_ref):
    h = h_ref[...]
    xs = h + (hp_ref[...] - h) * mu_ref[...]
    o_ref[...] = _mm(xs.astype(BF16), w_ref[...])


def _proj_plain_kernel(h_ref, w_ref, o_ref):
    o_ref[...] = _mm(h_ref[...].astype(BF16), w_ref[...])


def _proj_rwkv(h, hprev, mu4, w4):
    m, d = h.shape
    n_proj = w4.shape[0]
    return pl.pallas_call(
        _proj_mix_kernel,
        grid=(n_proj, m // ROW_TILE),
        in_specs=[pl.BlockSpec((ROW_TILE, d), lambda n, i: (i, 0)),
                  pl.BlockSpec((ROW_TILE, d), lambda n, i: (i, 0)),
                  pl.BlockSpec((None, 1, d), lambda n, i: (n, 0, 0)),
                  pl.BlockSpec((None, d, d), lambda n, i: (n, 0, 0))],
        out_specs=pl.BlockSpec((ROW_TILE, d), lambda n, i: (i, n)),
        out_shape=jax.ShapeDtypeStruct((m, n_proj * d), F32),
        compiler_params=pltpu.CompilerParams(
            dimension_semantics=("arbitrary", "arbitrary"), vmem_limit_bytes=VMEM_LIMIT),
        name="proj_rwkv",
    )(h, hprev, mu4, w4)


def _proj_hgrn(h, w):
    m, d = h.shape
    n_proj = w.shape[1] // d
    return pl.pallas_call(
        _proj_plain_kernel,
        grid=(n_proj, m // ROW_TILE),
        in_specs=[pl.BlockSpec((ROW_TILE, d), lambda n, i: (i, 0)),
                  pl.BlockSpec((d, d), lambda n, i: (0, n))],
        out_specs=pl.BlockSpec((ROW_TILE, d), lambda n, i: (i, n)),
        out_shape=jax.ShapeDtypeStruct((m, n_proj * d), F32),
        compiler_params=pltpu.CompilerParams(
            dimension_semantics=("arbitrary", "arbitrary"), vmem_limit_bytes=VMEM_LIMIT),
        name="proj_hgrn",
    )(h, w)


def _lora_kernel(has_v, h_ref, hp_ref, mu_ref, w1_ref, a1_ref, v1_ref, wm_ref, am_ref, vm_ref):
    h = h_ref[...]
    delta = hp_ref[...] - h
    xw = h + delta * mu_ref[4:5, :]
    xa = h + delta * mu_ref[5:6, :]
    wm_ref[...] = jnp.tanh(_mm(xw.astype(BF16), w1_ref[...]))
    am_ref[...] = _mm(xa.astype(BF16), a1_ref[...])
    if has_v:
        xv = h + delta * mu_ref[2:3, :]
        vm_ref[...] = _mm(xv.astype(BF16), v1_ref[...])
    else:
        vm_ref[...] = jnp.zeros(vm_ref.shape, F32)


def _lora(h, hprev, mu6, w1p, a1p, v1p, has_v):
    m, d = h.shape
    row = pl.BlockSpec((ROW_TILE, d), lambda i: (i, 0))
    wspec = pl.BlockSpec((d, LORA_PAD), lambda i: (0, 0))
    ospec = pl.BlockSpec((ROW_TILE, LORA_PAD), lambda i: (i, 0))
    oshape = jax.ShapeDtypeStruct((m, LORA_PAD), F32)
    return pl.pallas_call(
        functools.partial(_lora_kernel, has_v),
        grid=(m // ROW_TILE,),
        in_specs=[row, row, pl.BlockSpec(mu6.shape, lambda i: (0, 0)), wspec, wspec, wspec],
        out_specs=[ospec, ospec, ospec],
        out_shape=[oshape, oshape, oshape],
        name="lora",
    )(h, hprev, mu6, w1p, a1p, v1p)


def _outproj_kernel(has_next, g_ref, w_ref, x_ref, npost_ref, npre_ref, xo_ref, ho_ref):
    out = _mm(g_ref[...], w_ref[...])
    ms = jnp.mean(out * out, axis=-1, keepdims=True)
    xn = x_ref[...] + out * lax.rsqrt(ms + NORM_EPS) * npost_ref[...]
    xo_ref[...] = xn
    if has_next:
        ms2 = jnp.mean(xn * xn, axis=-1, keepdims=True)
        ho_ref[...] = xn * lax.rsqrt(ms2 + NORM_EPS) * npre_ref[...]
    else:
        ho_ref[...] = xn


def _outproj(g, w, x, npost, npre_next):
    m, d = x.shape
    has_next = npre_next is not None
    if not has_next:
        npre_next = npost
    row = pl.BlockSpec((ROW_TILE, d), lambda i: (i, 0))
    vec = pl.BlockSpec((1, d), lambda i: (0, 0))
    oshape = jax.ShapeDtypeStruct((m, d), F32)
    return pl.pallas_call(
        functools.partial(_outproj_kernel, has_next),
        grid=(m // ROW_TILE,),
        in_specs=[row, pl.BlockSpec((d, d), lambda i: (0, 0)), row, vec, vec],
        out_specs=[row, row],
        out_shape=[oshape, oshape],
        compiler_params=pltpu.CompilerParams(
            dimension_semantics=("arbitrary",), vmem_limit_bytes=VMEM_LIMIT),
        name="outproj",
    )(g, w, x, npost.reshape(1, d), npre_next.reshape(1, d))


def _rwkv_chunk_kernel(chunk, tvalid, has_vres, n_double,
                       r_ref, k_ref, v_ref, z_ref, wm_ref, am_ref, vm_ref, vf_ref,
                       w2_ref, a2_ref, v2_ref, prm_ref, s0_ref, ones_ref,
                       g_ref, so_ref, sbd_ref):
    c = pl.program_id(2)
    n_chunks = pl.num_programs(2)
    hw = RWKV_HEAD
    nh = HEADS_PER_GROUP
    diag = (lax.shift_right_logical(_iota((GROUP, GROUP), 0), 6)
            == lax.shift_right_logical(_iota((GROUP, GROUP), 1), 6))

    @pl.when(c == 0)
    def _init():
        s4 = s0_ref[...]
        tiled = jnp.concatenate([s4] * nh, axis=1)
        sbd_ref[...] = jnp.where(diag, tiled, 0.0)

    ones_blk = ones_ref[...]
    prm = prm_ref[...]
    w0, a0, v0 = prm[0:1], prm[1:2], prm[2:3]
    kk_w, ka_w, rk_w, ln_w, ln_b = prm[3:4], prm[4:5], prm[5:6], prm[6:7], prm[7:8]

    r = r_ref[...]
    k = k_ref[...]
    v = v_ref[...]
    z = z_ref[...]

    wl = w0 + _mm(wm_ref[...].astype(BF16), w2_ref[...])
    logw = -jnp.exp(-_softplus(-wl) - 0.5)
    if tvalid < chunk:
        logw = jnp.where(_iota(logw.shape, 0) < tvalid, logw, 0.0)
    alpha = _sigmoid(a0 + _mm(am_ref[...].astype(BF16), a2_ref[...]))
    if has_vres:
        gate = _sigmoid(v0 + _mm(vm_ref[...].astype(BF16), v2_ref[...]))
        v = v + (vf_ref[...] - v) * gate
    kk = k * kk_w
    kk = kk / jnp.maximum(jnp.sqrt(_segsum(kk * kk, ones_blk)), 1e-12)
    k2 = k * (1.0 + (alpha - 1.0) * ka_w)
    b = kk * alpha
    a = -kk

    cum = _cumsum_rows(logw)
    e_neg = jnp.exp(-cum)
    a_hat = a * jnp.exp(cum - logw)
    b_hat = b * e_neg
    k_hat = k2 * e_neg
    r_hat = r * jnp.exp(cum)
    cum_last = cum[chunk - 1:chunk, :]
    e_tail = jnp.exp(cum_last - cum)
    b_tail = b * e_tail
    k_tail = k2 * e_tail

    lhs = jnp.concatenate([a_hat, r_hat], axis=0).astype(BF16)
    gram_b = _mm_nt(lhs, _block_diag(b_hat, hw, nh))
    gram_k = _mm_nt(lhs, _block_diag(k_hat, hw, nh))
    shape_cc = (chunk, nh * chunk)
    t_idx = _iota(shape_cc, 0)
    i_idx = jnp.bitwise_and(_iota(shape_cc, 1), chunk - 1)
    strict = i_idx < t_idx
    incl = i_idx <= t_idx
    n_ab = jnp.where(strict, gram_b[:chunk], 0.0)
    a_ak = jnp.where(strict, gram_k[:chunk], 0.0)
    a_rb = jnp.where(incl, gram_b[chunk:], 0.0)
    a_rk = jnp.where(incl, gram_k[chunk:], 0.0)

    s_bd = sbd_ref[...]
    s_bf = s_bd.astype(BF16)
    v_bd = _block_diag(v, hw, nh)

    x = _mm_nt(a_hat.astype(BF16), s_bf) + _mm(a_ak.astype(BF16), v_bd)
    p = n_ab
    for j in range(n_double):
        x = x + _mm(p.astype(BF16), _block_diag(x, hw, nh))
        if j + 1 < n_double:
            p = _mm(p.astype(BF16), _block_diag(p, chunk, nh))
    u = x

    y = (_mm_nt(r_hat.astype(BF16), s_bf) + _mm(a_rb.astype(BF16), _block_diag(u, hw, nh))
         + _mm(a_rk.astype(BF16), v_bd))

    uv = jnp.concatenate([u, v], axis=0).astype(BF16)
    bk = jnp.concatenate([b_tail, k_tail], axis=0).astype(BF16)
    s_new = s_bd * jnp.exp(cum_last) + jnp.where(diag, _mm_tn(uv, bk), 0.0)
    sbd_ref[...] = s_new

    inv_n = 1.0 / RWKV_HEAD
    mean = _segsum(y, ones_blk) * inv_n
    dev = y - mean
    var = _segsum(dev * dev, ones_blk) * inv_n
    yn = dev * lax.rsqrt(var + RWKV_GN_EPS) * ln_w + ln_b
    bonus = _segsum(r * k2 * rk_w, ones_blk) * v
    g_ref[...] = ((yn + bonus) * (z * _sigmoid(z))).astype(BF16)

    @pl.when(c == n_chunks - 1)
    def _fin():
        so_ref[...] = (s_new[:, 0:hw] + s_new[:, hw:2 * hw]
                       + s_new[:, 2 * hw:3 * hw] + s_new[:, 3 * hw:4 * hw])


def _rwkv_chunk(proj, vf_src, wm, am, vm, w2p, a2p, v2p, prm, s0, *, n_seq, n_chunks, chunk,
                tvalid, has_vres):
    m = n_seq * n_chunks * chunk
    d = proj.shape[1] // 4
    n_groups = d // GROUP
    n_double = max(1, math.ceil(math.log2(tvalid)))

    def row(s, c):
        return s * n_chunks + c

    def col(off):
        return pl.BlockSpec((chunk, GROUP), lambda s, g, c: (row(s, c), off * n_groups + g))

    lora = pl.BlockSpec((chunk, LORA_PAD), lambda s, g, c: (row(s, c), 0))
    up = pl.BlockSpec((LORA_PAD, GROUP), lambda s, g, c: (0, g))
    state = pl.BlockSpec((None, None, GROUP, RWKV_HEAD), lambda s, g, c: (s, g, 0, 0))
    ones_blk = jnp.where(
        lax.shift_right_logical(_iota((GROUP, GROUP), 0), 6)
        == lax.shift_right_logical(_iota((GROUP, GROUP), 1), 6), 1.0, 0.0).astype(BF16)
    kernel = functools.partial(_rwkv_chunk_kernel, chunk, tvalid, has_vres, n_double)
    return pl.pallas_call(
        kernel,
        grid=(n_seq, n_groups, n_chunks),
        in_specs=[col(0), col(1), col(2), col(3), lora, lora, lora, col(2), up, up, up,
                  pl.BlockSpec((8, GROUP), lambda s, g, c: (0, g)), state,
                  pl.BlockSpec((GROUP, GROUP), lambda s, g, c: (0, 0))],
        out_specs=[pl.BlockSpec((chunk, GROUP), lambda s, g, c: (row(s, c), g)), state],
        out_shape=[jax.ShapeDtypeStruct((m, d), BF16),
                   jax.ShapeDtypeStruct(s0.shape, F32)],
        scratch_shapes=[pltpu.VMEM((GROUP, GROUP), F32)],
        compiler_params=pltpu.CompilerParams(
            dimension_semantics=("arbitrary", "arbitrary", "arbitrary")),
        name="rwkv_chunk",
    )(proj, proj, proj, proj, wm, am, vm, vf_src, w2p, a2p, v2p, prm, s0, ones_blk)


def _hgrn_chunk_kernel(chunk, tvalid, layer_j,
                       q_ref, f_ref, i_ref, z_ref, lbl_ref, nw_ref, s0_ref,
                       g_ref, so_ref, st_ref, gs_ref, qs_ref, os_ref):
    c = pl.program_id(2)
    n_chunks = pl.num_programs(2)

    @pl.when(c == 0)
    def _init():
        st_ref[...] = s0_ref[...].T

    logits = lbl_ref[...]
    ex = jnp.exp(logits - jnp.max(logits, axis=0, keepdims=True))
    soft = ex / jnp.sum(ex, axis=0, keepdims=True)
    lb = jnp.sum(soft[0:layer_j + 1], axis=0, keepdims=True) - soft[0:1]

    fl = f_ref[...]
    f = lb + (1.0 - lb) * _sigmoid(fl)
    logf = jnp.log(jnp.maximum(f, GATE_FLOOR))
    if tvalid < chunk:
        logf = jnp.where(_iota(logf.shape, 0) < tvalid, logf, 0.0)
    kg = (1.0 - lb) * _sigmoid(-fl)
    qraw = q_ref[...]
    qs = qraw * _sigmoid(qraw) * (HGRN_DK ** -0.5)
    val = i_ref[...]
    z = z_ref[...]

    gcum = _cumsum_rows(logf)
    st = st_ref[...]
    o_inter = _mm_nt((qs * jnp.exp(gcum)).astype(BF16), st.astype(BF16))

    gs_ref[...] = gcum
    qs_ref[...] = qs
    s_idx = _iota((chunk, 1), 0)

    def row_body(t, carry):
        g_t = gs_ref[pl.ds(t, 1), :]
        q_t = qs_ref[pl.ds(t, 1), :]
        dec = jnp.exp(jnp.minimum(g_t - gcum, 0.0))
        att = jnp.sum(q_t * kg * dec, axis=1, keepdims=True)
        att = jnp.where(s_idx <= t, att, 0.0)
        os_ref[pl.ds(t, 1), :] = jnp.sum(att * val, axis=0, keepdims=True)
        return carry

    if tvalid < chunk:
        os_ref[...] = jnp.zeros(os_ref.shape, F32)
    lax.fori_loop(0, tvalid, row_body, 0)

    g_last = gcum[chunk - 1:chunk, :]
    k_tail = kg * jnp.exp(g_last - gcum)
    st_new = st * jnp.exp(g_last) + _mm_tn(val.astype(BF16), k_tail.astype(BF16))
    st_ref[...] = st_new

    o = o_inter + os_ref[...]
    o = o * lax.rsqrt(jnp.mean(o * o, axis=-1, keepdims=True) + NORM_EPS) * nw_ref[...]
    g_ref[...] = (o * (z * _sigmoid(z))).astype(BF16)

    @pl.when(c == n_chunks - 1)
    def _fin():
        so_ref[...] = st_new.T


def _hgrn_chunk(proj, lb_logits, norm_w, s0, *, layer_j, n_seq, n_chunks, chunk, tvalid):
    m = n_seq * n_chunks * chunk
    d = proj.shape[1] // 4
    n_heads = d // HGRN_DK

    def row(s, c):
        return s * n_chunks + c

    def col(off):
        return pl.BlockSpec((chunk, HGRN_DK), lambda s, h, c: (row(s, c), off * n_heads + h))

    state = pl.BlockSpec((None, None, HGRN_DK, HGRN_DK), lambda s, h, c: (s, h, 0, 0))
    kernel = functools.partial(_hgrn_chunk_kernel, chunk, tvalid, layer_j)
    return pl.pallas_call(
        kernel,
        grid=(n_seq, n_heads, n_chunks),
        in_specs=[col(0), col(1), col(2), col(3),
                  pl.BlockSpec((lb_logits.shape[0], HGRN_DK), lambda s, h, c: (0, h)),
                  pl.BlockSpec((1, HGRN_DK), lambda s, h, c: (0, 0)), state],
        out_specs=[pl.BlockSpec((chunk, HGRN_DK), lambda s, h, c: (row(s, c), h)), state],
        out_shape=[jax.ShapeDtypeStruct((m, d), BF16),
                   jax.ShapeDtypeStruct(s0.shape, F32)],
        scratch_shapes=[pltpu.VMEM((HGRN_DK, HGRN_DK), F32),
                        pltpu.VMEM((chunk, HGRN_DK), F32),
                        pltpu.VMEM((chunk, HGRN_DK), F32),
                        pltpu.VMEM((chunk, HGRN_DK), F32)],
        compiler_params=pltpu.CompilerParams(
            dimension_semantics=("arbitrary", "arbitrary", "arbitrary")),
        name="hgrn_chunk",
    )(proj, proj, proj, proj, lb_logits, norm_w.reshape(1, HGRN_DK), s0)


def _pad_lanes(w, axis):
    pad = [(0, 0)] * w.ndim
    pad[axis] = (0, LORA_PAD - w.shape[axis])
    return jnp.pad(w, pad)


def _trunk(x, shift0, wkv0, hgrn0, p, *, n_seq, tpad, tvalid, chunk):
    d = x.shape[1]
    n_chunks = tpad // chunk
    n_groups = d // GROUP
    shifts, wkvs, hgrns = [], [], []
    vf_src = None
    h = _rmsnorm(x, p["norm_pre"][0])
    for layer in range(DEPTH):
        j = layer // 2
        npre_next = p["norm_pre"][layer + 1] if layer + 1 < DEPTH else None
        if layer % 2 == 0:
            h3 = h.reshape(n_seq, tpad, d)
            hprev = jnp.concatenate([shift0[j][:, None, :], h3[:, :tvalid - 1]], axis=1)
            if tvalid < tpad:
                hprev = jnp.pad(hprev, ((0, 0), (0, tpad - tvalid), (0, 0)))
            hprev = hprev.reshape(n_seq * tpad, d)
            shifts.append(h3[:, tvalid - 1])
            has_vres = j > 0
            proj = _proj_rwkv(h, hprev, p["mu4"][j], p["w_in"][j])
            wm, am, vm = _lora(h, hprev, p["mu6"][j], p["w1"][j], p["a1"][j],
                               p["v1"][max(j - 1, 0)], has_vres)
            if vf_src is None:
                vf_src = proj
            g, s_fin = _rwkv_chunk(
                proj, vf_src, wm, am, vm, p["w2"][j], p["a2"][j], p["v2"][max(j - 1, 0)],
                p["rwkv_prm"][j], wkv0[j].reshape(n_seq, n_groups, GROUP, RWKV_HEAD),
                n_seq=n_seq, n_chunks=n_chunks, chunk=chunk, tvalid=min(tvalid, chunk),
                has_vres=has_vres)
            wkvs.append(s_fin.reshape(wkv0[j].shape))
            w_o = p["rwkv_w_o"][j]
        else:
            proj = _proj_hgrn(h, p["hgrn_w_in"][j])
            g, s_fin = _hgrn_chunk(
                proj, p["hgrn_lb_logits"], p["hgrn_norm_w"][j], hgrn0[j], layer_j=j,
                n_seq=n_seq, n_chunks=n_chunks, chunk=chunk, tvalid=min(tvalid, chunk))
            hgrns.append(s_fin)
            w_o = p["hgrn_w_o"][j]
        x, h = _outproj(g, w_o, x, p["norm_post"][layer], npre_next)
    return x, jnp.stack(shifts), jnp.stack(wkvs), jnp.stack(hgrns)


def kernel(x_prompt, x_sample, state_rwkv_shift, state_rwkv_wkv, state_hgrn, norm_pre, norm_post,
           rwkv_mu, rwkv_w_in, rwkv_w0, rwkv_w1, rwkv_w2, rwkv_a0, rwkv_a1, rwkv_a2, rwkv_v0,
           rwkv_v1, rwkv_v2, rwkv_k_k, rwkv_k_a, rwkv_r_k, rwkv_ln_w, rwkv_ln_b, rwkv_w_o,
           hgrn_w_in, hgrn_lb_logits, hgrn_norm_w, hgrn_w_o):
    n_rwkv = rwkv_mu.shape[0]
    d = x_prompt.shape[-1]
    v0_full = jnp.concatenate([jnp.zeros((1, d), F32), rwkv_v0], axis=0)
    rwkv_prm = jnp.stack([rwkv_w0, rwkv_a0, v0_full, rwkv_k_k, rwkv_k_a,
                          rwkv_r_k.reshape(n_rwkv, d), rwkv_ln_w, rwkv_ln_b], axis=1)
    p = {
        "norm_pre": norm_pre, "norm_post": norm_post,
        "mu4": rwkv_mu[:, :4, None, :], "mu6": rwkv_mu,
        "w_in": rwkv_w_in.astype(BF16),
        "w1": _pad_lanes(rwkv_w1, 2).astype(BF16), "a1": _pad_lanes(rwkv_a1, 2).astype(BF16),
        "v1": _pad_lanes(rwkv_v1, 2).astype(BF16),
        "w2": _pad_lanes(rwkv_w2, 1).astype(BF16), "a2": _pad_lanes(rwkv_a2, 1).astype(BF16),
        "v2": _pad_lanes(rwkv_v2, 1).astype(BF16),
        "rwkv_prm": rwkv_prm, "rwkv_w_o": rwkv_w_o.astype(BF16),
        "hgrn_w_in": hgrn_w_in.astype(BF16), "hgrn_lb_logits": hgrn_lb_logits,
        "hgrn_norm_w": hgrn_norm_w, "hgrn_w_o": hgrn_w_o.astype(BF16),
    }

    bp, tp, _ = x_prompt.shape
    zero_shift = jnp.zeros((n_rwkv, bp, d), F32)
    zero_wkv = jnp.zeros((n_rwkv, bp) + state_rwkv_wkv.shape[2:], F32)
    zero_hgrn = jnp.zeros((state_hgrn.shape[0], bp) + state_hgrn.shape[2:], F32)
    y_p, p_shift, p_wkv, p_hgrn = _trunk(
        x_prompt.reshape(bp * tp, d), zero_shift, zero_wkv, zero_hgrn, p,
        n_seq=bp, tpad=tp, tvalid=tp, chunk=PROMPT_CHUNK)

    bs, ts, _ = x_sample.shape
    xs = jnp.pad(x_sample, ((0, 0), (0, SAMPLE_TPAD - ts), (0, 0))).reshape(bs * SAMPLE_TPAD, d)
    y_s, s_shift, s_wkv, s_hgrn = _trunk(
        xs, state_rwkv_shift, state_rwkv_wkv, state_hgrn, p,
        n_seq=bs, tpad=SAMPLE_TPAD, tvalid=ts, chunk=SAMPLE_TPAD)

    return (y_p.reshape(bp, tp, d), y_s.reshape(bs, SAMPLE_TPAD, d)[:, :ts],
            p_shift, p_wkv, p_hgrn, s_shift, s_wkv, s_hgrn)
```

```python
import functools
import math

import jax
import jax.numpy as jnp
from jax import lax
from jax.experimental import pallas as pl
from jax.experimental.pallas import tpu as pltpu

F32 = jnp.float32
BF16 = jnp.bfloat16

D_MODEL = 2048
DEPTH = 4
RWKV_HEAD = 64
HGRN_DK = 128
RWKV_GN_EPS = 1e-5 * RWKV_HEAD
NORM_EPS = 1e-6
GATE_FLOOR = 1e-30

LANES = 128
GROUP = 256
HEADS_PER_GROUP = GROUP // RWKV_HEAD
LORA_PAD = 128
PROMPT_CHUNK = 64
SAMPLE_TPAD = 16
HGRN_SUB = 16
HGRN_HEADS_PER_STEP = 8
ROW_TILE = 256
VMEM_LIMIT = 48 * 1024 * 1024


def _mm(a, b):
    return jnp.dot(a, b, preferred_element_type=F32)


def _mm_nt(a, b):
    return lax.dot_general(a, b, (((1,), (1,)), ((), ())), preferred_element_type=F32)


def _mm_tn(a, b):
    return lax.dot_general(a, b, (((0,), (0,)), ((), ())), preferred_element_type=F32)


def _iota(shape, dim):
    return lax.broadcasted_iota(jnp.int32, shape, dim)


def _split2(x):
    hi = x.astype(BF16)
    lo = (x - hi.astype(F32)).astype(BF16)
    return hi, lo


def _segsum(x, ones_blk):
    hi, lo = _split2(x)
    return _mm(hi, ones_blk) + _mm(lo, ones_blk)


def _cumsum_rows(x):
    n = x.shape[0]
    tri = jnp.where(_iota((n, n), 0) >= _iota((n, n), 1), 1.0, 0.0).astype(BF16)
    x1 = x.astype(BF16)
    r1 = x - x1.astype(F32)
    x2 = r1.astype(BF16)
    x3 = (r1 - x2.astype(F32)).astype(BF16)
    return _mm(tri, x1) + _mm(tri, x2) + _mm(tri, x3)


def _block_diag(x, head_width, n_heads):
    shift = int(math.log2(head_width))
    lane_head = lax.shift_right_logical(_iota(x.shape, 1), shift)
    parts = [jnp.where(lane_head == h, x, 0.0).astype(BF16) for h in range(n_heads)]
    return jnp.concatenate(parts, axis=0)


def _sigmoid(x):
    return jax.nn.sigmoid(x)


def _softplus(x):
    return jnp.maximum(x, 0.0) + jnp.log(1.0 + jnp.exp(-jnp.abs(x)))


def _round_robin(gens):
    results = [None] * len(gens)
    active = list(range(len(gens)))
    while active:
        for idx in list(active):
            try:
                next(gens[idx])
            except StopIteration as stop:
                results[idx] = stop.value
                active.remove(idx)
    return results


def _rmsnorm_kernel(x_ref, w_ref, o_ref):
    x = x_ref[...]
    ms = jnp.mean(x * x, axis=-1, keepdims=True)
    o_ref[...] = x * lax.rsqrt(ms + NORM_EPS) * w_ref[...]


def _rmsnorm(x, w):
    m, d = x.shape
    return pl.pallas_call(
        _rmsnorm_kernel,
        grid=(m // ROW_TILE,),
        in_specs=[pl.BlockSpec((ROW_TILE, d), lambda i: (i, 0)),
                  pl.BlockSpec((1, d), lambda i: (0, 0))],
        out_specs=pl.BlockSpec((ROW_TILE, d), lambda i: (i, 0)),
        out_shape=jax.ShapeDtypeStruct((m, d), F32),
        name="rmsnorm",
    )(x, w.reshape(1, d))


def _proj_mix_kernel(h_ref, hp_ref, mu_ref, w_ref, o_ref):
    h = h_ref[...]
    xs = h + (hp_ref[...] - h) * mu_ref[...]
    o_ref[...] = _mm(xs.astype(BF16), w_ref[...])


def _proj_plain_kernel(h_ref, w_ref, o_ref):
    o_ref[...] = _mm(h_ref[...].astype(BF16), w_ref[...])


def _proj_rwkv(h, hprev, mu4, w4):
    m, d = h.shape
    n_proj = w4.shape[0]
    return pl.pallas_call(
        _proj_mix_kernel,
        grid=(n_proj, m // ROW_TILE),
        in_specs=[pl.BlockSpec((ROW_TILE, d), lambda n, i: (i, 0)),
                  pl.BlockSpec((ROW_TILE, d), lambda n, i: (i, 0)),
                  pl.BlockSpec((None, 1, d), lambda n, i: (n, 0, 0)),
                  pl.BlockSpec((None, d, d), lambda n, i: (n, 0, 0))],
        out_specs=pl.BlockSpec((ROW_TILE, d), lambda n, i: (i, n)),
        out_shape=jax.ShapeDtypeStruct((m, n_proj * d), F32),
        compiler_params=pltpu.CompilerParams(
            dimension_semantics=("arbitrary", "arbitrary"), vmem_limit_bytes=VMEM_LIMIT),
        name="proj_rwkv",
    )(h, hprev, mu4, w4)


def _proj_hgrn(h, w):
    m, d = h.shape
    n_proj = w.shape[1] // d
    return pl.pallas_call(
        _proj_plain_kernel,
        grid=(n_proj, m // ROW_TILE),
        in_specs=[pl.BlockSpec((ROW_TILE, d), lambda n, i: (i, 0)),
                  pl.BlockSpec((d, d), lambda n, i: (0, n))],
        out_specs=pl.BlockSpec((ROW_TILE, d), lambda n, i: (i, n)),
        out_shape=jax.ShapeDtypeStruct((m, n_proj * d), F32),
        compiler_params=pltpu.CompilerParams(
            dimension_semantics=("arbitrary", "arbitrary"), vmem_limit_bytes=VMEM_LIMIT),
        name="proj_hgrn",
    )(h, w)


def _lora_kernel(has_v, h_ref, hp_ref, mu_ref, w1_ref, a1_ref, v1_ref, wm_ref, am_ref, vm_ref):
    h = h_ref[...]
    delta = hp_ref[...] - h
    xw = h + delta * mu_ref[4:5, :]
    xa = h + delta * mu_ref[5:6, :]
    wm_ref[...] = jnp.tanh(_mm(xw.astype(BF16), w1_ref[...]))
    am_ref[...] = _mm(xa.astype(BF16), a1_ref[...])
    if has_v:
        xv = h + delta * mu_ref[2:3, :]
        vm_ref[...] = _mm(xv.astype(BF16), v1_ref[...])
    else:
        vm_ref[...] = jnp.zeros(vm_ref.shape, F32)


def _lora(h, hprev, mu6, w1p, a1p, v1p, has_v):
    m, d = h.shape
    row = pl.BlockSpec((ROW_TILE, d), lambda i: (i, 0))
    wspec = pl.BlockSpec((d, LORA_PAD), lambda i: (0, 0))
    ospec = pl.BlockSpec((ROW_TILE, LORA_PAD), lambda i: (i, 0))
    oshape = jax.ShapeDtypeStruct((m, LORA_PAD), F32)
    return pl.pallas_call(
        functools.partial(_lora_kernel, has_v),
        grid=(m // ROW_TILE,),
        in_specs=[row, row, pl.BlockSpec(mu6.shape, lambda i: (0, 0)), wspec, wspec, wspec],
        out_specs=[ospec, ospec, ospec],
        out_shape=[oshape, oshape, oshape],
        name="lora",
    )(h, hprev, mu6, w1p, a1p, v1p)


def _outproj_kernel(has_next, g_ref, w_ref, x_ref, npost_ref, npre_ref, xo_ref, ho_ref):
    out = _mm(g_ref[...], w_ref[...])
    ms = jnp.mean(out * out, axis=-1, keepdims=True)
    xn = x_ref[...] + out * lax.rsqrt(ms + NORM_EPS) * npost_ref[...]
    xo_ref[...] = xn
    if has_next:
        ms2 = jnp.mean(xn * xn, axis=-1, keepdims=True)
        ho_ref[...] = xn * lax.rsqrt(ms2 + NORM_EPS) * npre_ref[...]
    else:
        ho_ref[...] = xn


def _outproj(g, w, x, npost, npre_next):
    m, d = x.shape
    has_next = npre_next is not None
    if not has_next:
        npre_next = npost
    row = pl.BlockSpec((ROW_TILE, d), lambda i: (i, 0))
    vec = pl.BlockSpec((1, d), lambda i: (0, 0))
    oshape = jax.ShapeDtypeStruct((m, d), F32)
    return pl.pallas_call(
        functools.partial(_outproj_kernel, has_next),
        grid=(m // ROW_TILE,),
        in_specs=[row, pl.BlockSpec((d, d), lambda i: (0, 0)), row, vec, vec],
        out_specs=[row, row],
        out_shape=[oshape, oshape],
        compiler_params=pltpu.CompilerParams(
            dimension_semantics=("arbitrary",), vmem_limit_bytes=VMEM_LIMIT),
        name="outproj",
    )(g, w, x, npost.reshape(1, d), npre_next.reshape(1, d))


def _rwkv_group(chunk, tvalid, has_vres, n_double, sl, grp, refs, wm, am, vm, ones_blk, diag):
    (r_ref, k_ref, v_ref, z_ref, vf_ref, w2_ref, a2_ref, v2_ref, prm_ref, g_ref, sbd_ref) = refs
    hw = RWKV_HEAD
    nh = HEADS_PER_GROUP
    prm = prm_ref[:, sl]
    w0, a0, v0 = prm[0:1], prm[1:2], prm[2:3]
    kk_w, ka_w, rk_w, ln_w, ln_b = prm[3:4], prm[4:5], prm[5:6], prm[6:7], prm[7:8]
    r = r_ref[:, sl]
    k = k_ref[:, sl]
    v = v_ref[:, sl]

    wl = w0 + _mm(wm, w2_ref[:, sl])
    al = a0 + _mm(am, a2_ref[:, sl])
    if has_vres:
        vl = v0 + _mm(vm, v2_ref[:, sl])
    kk = k * kk_w
    kk_n2 = _segsum(kk * kk, ones_blk)
    yield
    logw = -jnp.exp(-_softplus(-wl) - 0.5)
    if tvalid < chunk:
        logw = jnp.where(_iota(logw.shape, 0) < tvalid, logw, 0.0)
    alpha = _sigmoid(al)
    if has_vres:
        v = v + (vf_ref[:, sl] - v) * _sigmoid(vl)
    kk = kk / jnp.maximum(jnp.sqrt(kk_n2), 1e-12)
    k2 = k * (1.0 + (alpha - 1.0) * ka_w)
    b = kk * alpha
    a = -kk
    cum = _cumsum_rows(logw)
    yield
    e_neg = jnp.exp(-cum)
    a_hat = a * jnp.exp(cum - logw)
    b_hat = b * e_neg
    k_hat = k2 * e_neg
    r_hat = r * jnp.exp(cum)
    cum_last = cum[chunk - 1:chunk, :]
    e_tail = jnp.exp(cum_last - cum)
    b_tail = b * e_tail
    k_tail = k2 * e_tail

    lhs = jnp.concatenate([a_hat, r_hat], axis=0).astype(BF16)
    gram_b = _mm_nt(lhs, _block_diag(b_hat, hw, nh))
    gram_k = _mm_nt(lhs, _block_diag(k_hat, hw, nh))
    s_bd = sbd_ref[grp]
    s_bf = s_bd.astype(BF16)
    x_state = _mm_nt(a_hat.astype(BF16), s_bf)
    y_state = _mm_nt(r_hat.astype(BF16), s_bf)
    yield
    shape_cc = (chunk, nh * chunk)
    t_idx = _iota(shape_cc, 0)
    i_idx = jnp.bitwise_and(_iota(shape_cc, 1), chunk - 1)
    strict = i_idx < t_idx
    incl = i_idx <= t_idx
    n_ab = jnp.where(strict, gram_b[:chunk], 0.0)
    a_ak = jnp.where(strict, gram_k[:chunk], 0.0)
    a_rb = jnp.where(incl, gram_b[chunk:], 0.0)
    a_rk = jnp.where(incl, gram_k[chunk:], 0.0)
    v_bd = _block_diag(v, hw, nh)

    x = x_state + _mm(a_ak.astype(BF16), v_bd)
    y_v = y_state + _mm(a_rk.astype(BF16), v_bd)
    p = n_ab
    if n_double > 1:
        p_next = _mm(p.astype(BF16), _block_diag(p, chunk, nh))
    yield
    for j in range(n_double):
        x = x + _mm(p.astype(BF16), _block_diag(x, hw, nh))
        if j + 1 < n_double:
            p = p_next
            if j + 2 < n_double:
                p_next = _mm(p.astype(BF16), _block_diag(p, chunk, nh))
        yield
    u = x

    y = y_v + _mm(a_rb.astype(BF16), _block_diag(u, hw, nh))
    uv = jnp.concatenate([u, v], axis=0).astype(BF16)
    bk = jnp.concatenate([b_tail, k_tail], axis=0).astype(BF16)
    sbd_ref[grp] = s_bd * jnp.exp(cum_last) + jnp.where(diag, _mm_tn(uv, bk), 0.0)
    bonus_s = _segsum(r * k2 * rk_w, ones_blk)
    yield

    inv_n = 1.0 / RWKV_HEAD
    mean = _segsum(y, ones_blk) * inv_n
    yield
    dev = y - mean
    var = _segsum(dev * dev, ones_blk) * inv_n
    yield
    yn = dev * lax.rsqrt(var + RWKV_GN_EPS) * ln_w + ln_b
    z = z_ref[:, sl]
    g_ref[:, sl] = ((yn + bonus_s * v) * (z * _sigmoid(z))).astype(BF16)


def _rwkv_chunk_kernel(chunk, tvalid, has_vres, n_double, n_groups,
                       r_ref, k_ref, v_ref, z_ref, wm_ref, am_ref, vm_ref, vf_ref,
                       w2_ref, a2_ref, v2_ref, prm_ref, s0_ref, ones_ref,
                       g_ref, so_ref, sbd_ref):
    c = pl.program_id(1)
    n_chunks = pl.num_programs(1)
    hw = RWKV_HEAD
    diag = (lax.shift_right_logical(_iota((GROUP, GROUP), 0), 6)
            == lax.shift_right_logical(_iota((GROUP, GROUP), 1), 6))

    @pl.when(c == 0)
    def _init():
        for g in range(n_groups):
            s4 = s0_ref[g]
            tiled = jnp.concatenate([s4] * HEADS_PER_GROUP, axis=1)
            sbd_ref[g] = jnp.where(diag, tiled, 0.0)

    ones_blk = ones_ref[...]
    wm = wm_ref[...].astype(BF16)
    am = am_ref[...].astype(BF16)
    vm = vm_ref[...].astype(BF16)
    refs = (r_ref, k_ref, v_ref, z_ref, vf_ref, w2_ref, a2_ref, v2_ref, prm_ref, g_ref, sbd_ref)
    _round_robin([
        _rwkv_group(chunk, tvalid, has_vres, n_double, slice(g * GROUP, (g + 1) * GROUP), g, refs,
                    wm, am, vm, ones_blk, diag)
        for g in range(n_groups)])

    @pl.when(c == n_chunks - 1)
    def _fin():
        for g in range(n_groups):
            s_new = sbd_ref[g]
            so_ref[g] = (s_new[:, 0:hw] + s_new[:, hw:2 * hw]
                         + s_new[:, 2 * hw:3 * hw] + s_new[:, 3 * hw:4 * hw])


def _rwkv_chunk(proj, vf_src, wm, am, vm, w2p, a2p, v2p, prm, s0, *, n_seq, n_chunks, chunk,
                tvalid, has_vres):
    m = n_seq * n_chunks * chunk
    d = proj.shape[1] // 4
    n_groups = d // GROUP
    n_double = max(1, math.ceil(math.log2(tvalid)))

    def col(off):
        return pl.BlockSpec((chunk, d), lambda s, c: (s * n_chunks + c, off))

    lora = pl.BlockSpec((chunk, LORA_PAD), lambda s, c: (s * n_chunks + c, 0))
    up = pl.BlockSpec((LORA_PAD, d), lambda s, c: (0, 0))
    state = pl.BlockSpec((None, n_groups, GROUP, RWKV_HEAD), lambda s, c: (s, 0, 0, 0))
    ones_blk = jnp.where(
        lax.shift_right_logical(_iota((GROUP, GROUP), 0), 6)
        == lax.shift_right_logical(_iota((GROUP, GROUP), 1), 6), 1.0, 0.0).astype(BF16)
    kernel = functools.partial(_rwkv_chunk_kernel, chunk, tvalid, has_vres, n_double, n_groups)
    return pl.pallas_call(
        kernel,
        grid=(n_seq, n_chunks),
        in_specs=[col(0), col(1), col(2), col(3), lora, lora, lora, col(2), up, up, up,
                  pl.BlockSpec((8, d), lambda s, c: (0, 0)), state,
                  pl.BlockSpec((GROUP, GROUP), lambda s, c: (0, 0))],
        out_specs=[pl.BlockSpec((chunk, d), lambda s, c: (s * n_chunks + c, 0)), state],
        out_shape=[jax.ShapeDtypeStruct((m, d), BF16),
                   jax.ShapeDtypeStruct(s0.shape, F32)],
        scratch_shapes=[pltpu.VMEM((n_groups, GROUP, GROUP), F32)],
        compiler_params=pltpu.CompilerParams(
            dimension_semantics=("arbitrary", "arbitrary"), vmem_limit_bytes=VMEM_LIMIT),
        name="rwkv_chunk",
    )(proj, proj, proj, proj, wm, am, vm, vf_src, w2p, a2p, v2p, prm, s0, ones_blk)


def _hgrn_diag_block(gb, qb, kb, vb, t_count, ones_blk):
    sub = gb.shape[0]
    shift = int(math.log2(sub))
    rows = t_count * sub
    g_t = jnp.concatenate([jnp.broadcast_to(gb[t:t + 1, :], (sub, HGRN_DK)) for t in range(t_count)], axis=0)
    q_t = jnp.concatenate([jnp.broadcast_to(qb[t:t + 1, :], (sub, HGRN_DK)) for t in range(t_count)], axis=0)
    g_s = jnp.concatenate([gb] * t_count, axis=0)
    k_s = jnp.concatenate([kb] * t_count, axis=0)
    v_s = jnp.concatenate([vb] * t_count, axis=0)
    prod = q_t * k_s * jnp.exp(jnp.minimum(g_t - g_s, 0.0))
    att = _segsum(prod, ones_blk)
    yield
    ridx = _iota((rows, HGRN_DK), 0)
    causal = jnp.bitwise_and(ridx, sub - 1) <= lax.shift_right_logical(ridx, shift)
    weighted = jnp.where(causal, att * v_s, 0.0)
    sel = jnp.where(lax.shift_right_logical(_iota((sub, rows), 1), shift) == _iota((sub, rows), 0),
                    1.0, 0.0).astype(BF16)
    hi, lo = _split2(weighted)
    return _mm(sel, hi) + _mm(sel, lo)


def _hgrn_head(chunk, tvalid, sl, head, refs, lb, nw, ones_blk):
    (q_ref, f_ref, i_ref, z_ref, g_ref, st_ref) = refs
    sub = min(HGRN_SUB, chunk)
    fl = f_ref[:, sl]
    qraw = q_ref[:, sl]
    val = i_ref[:, sl]
    f = lb + (1.0 - lb) * _sigmoid(fl)
    logf = jnp.log(jnp.maximum(f, GATE_FLOOR))
    if tvalid < chunk:
        logf = jnp.where(_iota(logf.shape, 0) < tvalid, logf, 0.0)
    kg = (1.0 - lb) * _sigmoid(-fl)
    qs = qraw * _sigmoid(qraw) * (HGRN_DK ** -0.5)
    gcum = _cumsum_rows(logf)
    yield
    st = st_ref[head]
    o_inter = _mm_nt((qs * jnp.exp(gcum)).astype(BF16), st.astype(BF16))
    g_last = gcum[chunk - 1:chunk, :]
    k_tail = kg * jnp.exp(g_last - gcum)
    val_bf = val.astype(BF16)
    st_ref[head] = st * jnp.exp(g_last) + _mm_tn(val_bf, k_tail.astype(BF16))
    yield

    parts = []
    n_blk = -(-tvalid // sub)
    for bi in range(n_blk):
        lo, hi = bi * sub, (bi + 1) * sub
        gb, qb = gcum[lo:hi], qs[lo:hi]
        diag_gen = _hgrn_diag_block(gb, qb, kg[lo:hi], val[lo:hi], min(sub, tvalid - lo), ones_blk)
        if bi > 0:
            g_edge = gcum[lo - 1:lo]
            q_n = (qb * jnp.exp(gb - g_edge)).astype(BF16)
            k_n = (kg[0:lo] * jnp.exp(g_edge - gcum[0:lo])).astype(BF16)
            att = _mm_nt(q_n, k_n)
        next(diag_gen)
        yield
        if bi > 0:
            o_off = _mm(att.astype(BF16), val_bf[0:lo])
        try:
            next(diag_gen)
        except StopIteration as stop:
            o_b = stop.value
        yield
        parts.append(o_b + o_off if bi > 0 else o_b)
    if n_blk * sub < chunk:
        parts.append(jnp.zeros((chunk - n_blk * sub, HGRN_DK), F32))
    o_intra = parts[0] if len(parts) == 1 else jnp.concatenate(parts, axis=0)

    o = o_inter + o_intra
    o = o * lax.rsqrt(jnp.mean(o * o, axis=-1, keepdims=True) + NORM_EPS) * nw
    z = z_ref[:, sl]
    g_ref[:, sl] = (o * (z * _sigmoid(z))).astype(BF16)


def _hgrn_chunk_kernel(chunk, tvalid, layer_j, heads,
                       q_ref, f_ref, i_ref, z_ref, lbl_ref, nw_ref, s0_ref,
                       g_ref, so_ref, st_ref):
    c = pl.program_id(2)
    n_chunks = pl.num_programs(2)

    @pl.when(c == 0)
    def _init():
        for hh in range(heads):
            st_ref[hh] = s0_ref[hh].T

    logits = lbl_ref[...]
    ex = jnp.exp(logits - jnp.max(logits, axis=0, keepdims=True))
    soft = ex / jnp.sum(ex, axis=0, keepdims=True)
    lb_all = jnp.sum(soft[0:layer_j + 1], axis=0, keepdims=True) - soft[0:1]

    ones_blk = jnp.ones((HGRN_DK, HGRN_DK), BF16)
    nw = nw_ref[...]
    refs = (q_ref, f_ref, i_ref, z_ref, g_ref, st_ref)
    _round_robin([
        _hgrn_head(chunk, tvalid, slice(hh * HGRN_DK, (hh + 1) * HGRN_DK), hh, refs,
                   lb_all[:, hh * HGRN_DK:(hh + 1) * HGRN_DK], nw, ones_blk)
        for hh in range(heads)])

    @pl.when(c == n_chunks - 1)
    def _fin():
        for hh in range(heads):
            so_ref[hh] = st_ref[hh].T


def _hgrn_chunk(proj, lb_logits, norm_w, s0, *, layer_j, n_seq, n_chunks, chunk, tvalid):
    m = n_seq * n_chunks * chunk
    d = proj.shape[1] // 4
    n_heads = d // HGRN_DK
    heads = HGRN_HEADS_PER_STEP
    n_hb = n_heads // heads
    width = heads * HGRN_DK

    def col(off):
        return pl.BlockSpec((chunk, width), lambda s, h, c: (s * n_chunks + c, off * n_hb + h))

    state = pl.BlockSpec((None, heads, HGRN_DK, HGRN_DK), lambda s, h, c: (s, h, 0, 0))
    kernel = functools.partial(_hgrn_chunk_kernel, chunk, tvalid, layer_j, heads)
    return pl.pallas_call(
        kernel,
        grid=(n_seq, n_hb, n_chunks),
        in_specs=[col(0), col(1), col(2), col(3),
                  pl.BlockSpec((lb_logits.shape[0], width), lambda s, h, c: (0, h)),
                  pl.BlockSpec((1, HGRN_DK), lambda s, h, c: (0, 0)), state],
        out_specs=[pl.BlockSpec((chunk, width), lambda s, h, c: (s * n_chunks + c, h)), state],
        out_shape=[jax.ShapeDtypeStruct((m, d), BF16),
                   jax.ShapeDtypeStruct(s0.shape, F32)],
        scratch_shapes=[pltpu.VMEM((heads, HGRN_DK, HGRN_DK), F32)],
        compiler_params=pltpu.CompilerParams(
            dimension_semantics=("arbitrary", "arbitrary", "arbitrary"),
            vmem_limit_bytes=VMEM_LIMIT),
        name="hgrn_chunk",
    )(proj, proj, proj, proj, lb_logits, norm_w.reshape(1, HGRN_DK), s0)


def _pad_lanes(w, axis):
    pad = [(0, 0)] * w.ndim
    pad[axis] = (0, LORA_PAD - w.shape[axis])
    return jnp.pad(w, pad)


def _trunk(x, shift0, wkv0, hgrn0, p, *, n_seq, tpad, tvalid, chunk):
    d = x.shape[1]
    n_chunks = tpad // chunk
    n_groups = d // GROUP
    shifts, wkvs, hgrns = [], [], []
    vf_src = None
    h = _rmsnorm(x, p["norm_pre"][0])
    for layer in range(DEPTH):
        j = layer // 2
        npre_next = p["norm_pre"][layer + 1] if layer + 1 < DEPTH else None
        if layer % 2 == 0:
            h3 = h.reshape(n_seq, tpad, d)
            hprev = jnp.concatenate([shift0[j][:, None, :], h3[:, :tvalid - 1]], axis=1)
            if tvalid < tpad:
                hprev = jnp.pad(hprev, ((0, 0), (0, tpad - tvalid), (0, 0)))
            hprev = hprev.reshape(n_seq * tpad, d)
            shifts.append(h3[:, tvalid - 1])
            has_vres = j > 0
            proj = _proj_rwkv(h, hprev, p["mu4"][j], p["w_in"][j])
            wm, am, vm = _lora(h, hprev, p["mu6"][j], p["w1"][j], p["a1"][j],
                               p["v1"][max(j - 1, 0)], has_vres)
            if vf_src is None:
                vf_src = proj
            g, s_fin = _rwkv_chunk(
                proj, vf_src, wm, am, vm, p["w2"][j], p["a2"][j], p["v2"][max(j - 1, 0)],
                p["rwkv_prm"][j], wkv0[j].reshape(n_seq, n_groups, GROUP, RWKV_HEAD),
                n_seq=n_seq, n_chunks=n_chunks, chunk=chunk, tvalid=min(tvalid, chunk),
                has_vres=has_vres)
            wkvs.append(s_fin.reshape(wkv0[j].shape))
            w_o = p["rwkv_w_o"][j]
        else:
            proj = _proj_hgrn(h, p["hgrn_w_in"][j])
            g, s_fin = _hgrn_chunk(
                proj, p["hgrn_lb_logits"], p["hgrn_norm_w"][j], hgrn0[j], layer_j=j,
                n_seq=n_seq, n_chunks=n_chunks, chunk=chunk, tvalid=min(tvalid, chunk))
            hgrns.append(s_fin)
            w_o = p["hgrn_w_o"][j]
        x, h = _outproj(g, w_o, x, p["norm_post"][layer], npre_next)
    return x, jnp.stack(shifts), jnp.stack(wkvs), jnp.stack(hgrns)


def kernel(x_prompt, x_sample, state_rwkv_shift, state_rwkv_wkv, state_hgrn, norm_pre, norm_post,
           rwkv_mu, rwkv_w_in, rwkv_w0, rwkv_w1, rwkv_w2, rwkv_a0, rwkv_a1, rwkv_a2, rwkv_v0,
           rwkv_v1, rwkv_v2, rwkv_k_k, rwkv_k_a, rwkv_r_k, rwkv_ln_w, rwkv_ln_b, rwkv_w_o,
           hgrn_w_in, hgrn_lb_logits, hgrn_norm_w, hgrn_w_o):
    n_rwkv = rwkv_mu.shape[0]
    d = x_prompt.shape[-1]
    v0_full = jnp.concatenate([jnp.zeros((1, d), F32), rwkv_v0], axis=0)
    rwkv_prm = jnp.stack([rwkv_w0, rwkv_a0, v0_full, rwkv_k_k, rwkv_k_a,
                          rwkv_r_k.reshape(n_rwkv, d), rwkv_ln_w, rwkv_ln_b], axis=1)
    p = {
        "norm_pre": norm_pre, "norm_post": norm_post,
        "mu4": rwkv_mu[:, :4, None, :], "mu6": rwkv_mu,
        "w_in": rwkv_w_in.astype(BF16),
        "w1": _pad_lanes(rwkv_w1, 2).astype(BF16), "a1": _pad_lanes(rwkv_a1, 2).astype(BF16),
        "v1": _pad_lanes(rwkv_v1, 2).astype(BF16),
        "w2": _pad_lanes(rwkv_w2, 1).astype(BF16), "a2": _pad_lanes(rwkv_a2, 1).astype(BF16),
        "v2": _pad_lanes(rwkv_v2, 1).astype(BF16),
        "rwkv_prm": rwkv_prm, "rwkv_w_o": rwkv_w_o.astype(BF16),
        "hgrn_w_in": hgrn_w_in.astype(BF16), "hgrn_lb_logits": hgrn_lb_logits,
        "hgrn_norm_w": hgrn_norm_w, "hgrn_w_o": hgrn_w_o.astype(BF16),
    }

    bp, tp, _ = x_prompt.shape
    zero_shift = jnp.zeros((n_rwkv, bp, d), F32)
    zero_wkv = jnp.zeros((n_rwkv, bp) + state_rwkv_wkv.shape[2:], F32)
    zero_hgrn = jnp.zeros((state_hgrn.shape[0], bp) + state_hgrn.shape[2:], F32)
    y_p, p_shift, p_wkv, p_hgrn = _trunk(
        x_prompt.reshape(bp * tp, d), zero_shift, zero_wkv, zero_hgrn, p,
        n_seq=bp, tpad=tp, tvalid=tp, chunk=PROMPT_CHUNK)

    bs, ts, _ = x_sample.shape
    xs = jnp.pad(x_sample, ((0, 0), (0, SAMPLE_TPAD - ts), (0, 0))).reshape(bs * SAMPLE_TPAD, d)
    y_s, s_shift, s_wkv, s_hgrn = _trunk(
        xs, state_rwkv_shift, state_rwkv_wkv, state_hgrn, p,
        n_seq=bs, tpad=SAMPLE_TPAD, tvalid=ts, chunk=SAMPLE_TPAD)

    return (y_p.reshape(bp, tp, d), y_s.reshape(bs, SAMPLE_TPAD, d)[:, :ts],
            p_shift, p_wkv, p_hgrn, s_shift, s_wkv, s_hgrn)
```

```python
import functools
import math

import jax
import jax.numpy as jnp
from jax import lax
from jax.experimental import pallas as pl
from jax.experimental.pallas import tpu as pltpu

F32 = jnp.float32
BF16 = jnp.bfloat16

D_MODEL = 2048
DEPTH = 4
RWKV_HEAD = 64
HGRN_DK = 128
RWKV_GN_EPS = 1e-5 * RWKV_HEAD
NORM_EPS = 1e-6
GATE_FLOOR = 1e-30

LANES = 128
GROUP = 256
HEADS_PER_GROUP = GROUP // RWKV_HEAD
LORA_PAD = 128
PROMPT_CHUNK = 64
SAMPLE_TPAD = 16
HGRN_SUB = 8
HGRN_HEADS_PER_STEP = 8
ROW_TILE = 256
VMEM_LIMIT = 48 * 1024 * 1024


def _mm(a, b):
    return jnp.dot(a, b, preferred_element_type=F32)


def _mm_nt(a, b):
    return lax.dot_general(a, b, (((1,), (1,)), ((), ())), preferred_element_type=F32)


def _mm_tn(a, b):
    return lax.dot_general(a, b, (((0,), (0,)), ((), ())), preferred_element_type=F32)


def _iota(shape, dim):
    return lax.broadcasted_iota(jnp.int32, shape, dim)


def _split2(x):
    hi = x.astype(BF16)
    lo = (x - hi.astype(F32)).astype(BF16)
    return hi, lo


def _segsum(x, ones_blk):
    hi, lo = _split2(x)
    return _mm(hi, ones_blk) + _mm(lo, ones_blk)


def _cumsum_rows(x):
    n = x.shape[0]
    tri = jnp.where(_iota((n, n), 0) >= _iota((n, n), 1), 1.0, 0.0).astype(BF16)
    x1 = x.astype(BF16)
    r1 = x - x1.astype(F32)
    x2 = r1.astype(BF16)
    x3 = (r1 - x2.astype(F32)).astype(BF16)
    return _mm(tri, x1) + _mm(tri, x2) + _mm(tri, x3)


def _block_diag(x, head_width, n_heads):
    shift = int(math.log2(head_width))
    lane_head = lax.shift_right_logical(_iota(x.shape, 1), shift)
    parts = [jnp.where(lane_head == h, x, 0.0).astype(BF16) for h in range(n_heads)]
    return jnp.concatenate(parts, axis=0)


def _sigmoid(x):
    return jax.nn.sigmoid(x)


def _softplus(x):
    return jnp.maximum(x, 0.0) + jnp.log(1.0 + jnp.exp(-jnp.abs(x)))


def _round_robin(gens):
    results = [None] * len(gens)
    active = list(range(len(gens)))
    while active:
        for idx in list(active):
            try:
                next(gens[idx])
            except StopIteration as stop:
                results[idx] = stop.value
                active.remove(idx)
    return results


def _rmsnorm_kernel(x_ref, w_ref, o_ref):
    x = x_ref[...]
    ms = jnp.mean(x * x, axis=-1, keepdims=True)
    o_ref[...] = x * lax.rsqrt(ms + NORM_EPS) * w_ref[...]


def _rmsnorm(x, w):
    m, d = x.shape
    return pl.pallas_call(
        _rmsnorm_kernel,
        grid=(m // ROW_TILE,),
        in_specs=[pl.BlockSpec((ROW_TILE, d), lambda i: (i, 0)),
                  pl.BlockSpec((1, d), lambda i: (0, 0))],
        out_specs=pl.BlockSpec((ROW_TILE, d), lambda i: (i, 0)),
        out_shape=jax.ShapeDtypeStruct((m, d), F32),
        name="rmsnorm",
    )(x, w.reshape(1, d))


def _prev_rows(h, first_ref, carry_ref, tile, tiles_per_seq):
    rolled = pltpu.roll(h, 1, axis=0)
    seq_start = lax.rem(tile, tiles_per_seq) == 0
    edge = jnp.where(seq_start, first_ref[...], carry_ref[7:8, :])
    carry_ref[...] = h[h.shape[0] - 8:, :]
    return jnp.where(_iota(h.shape, 0) == 0, edge, rolled)


def _proj_mix_kernel(tiles_per_seq, h_ref, hp_ref, mu_ref, w_ref, o_ref, *carry):
    h = h_ref[...]
    if tiles_per_seq:
        hprev = _prev_rows(h, hp_ref, carry[0], pl.program_id(1), tiles_per_seq)
    else:
        hprev = hp_ref[...]
    xs = h + (hprev - h) * mu_ref[...]
    o_ref[...] = _mm(xs.astype(BF16), w_ref[...])


def _proj_plain_kernel(h_ref, w_ref, o_ref):
    o_ref[...] = _mm(h_ref[...].astype(BF16), w_ref[...])


def _prev_spec(hprev, tiles_per_seq, d, grid_rank):
    if tiles_per_seq:
        if grid_rank == 2:
            return pl.BlockSpec((None, 1, d), lambda n, i: (i // tiles_per_seq, 0, 0))
        return pl.BlockSpec((None, 1, d), lambda i: (i // tiles_per_seq, 0, 0))
    if grid_rank == 2:
        return pl.BlockSpec((ROW_TILE, d), lambda n, i: (i, 0))
    return pl.BlockSpec((ROW_TILE, d), lambda i: (i, 0))


def _proj_rwkv(h, hprev, mu4, w4, tiles_per_seq):
    m, d = h.shape
    n_proj = w4.shape[0]
    return pl.pallas_call(
        functools.partial(_proj_mix_kernel, tiles_per_seq),
        grid=(n_proj, m // ROW_TILE),
        in_specs=[pl.BlockSpec((ROW_TILE, d), lambda n, i: (i, 0)),
                  _prev_spec(hprev, tiles_per_seq, d, 2),
                  pl.BlockSpec((None, 1, d), lambda n, i: (n, 0, 0)),
                  pl.BlockSpec((None, d, d), lambda n, i: (n, 0, 0))],
        out_specs=pl.BlockSpec((ROW_TILE, d), lambda n, i: (i, n)),
        out_shape=jax.ShapeDtypeStruct((m, n_proj * d), F32),
        scratch_shapes=[pltpu.VMEM((8, d), F32)] if tiles_per_seq else [],
        compiler_params=pltpu.CompilerParams(
            dimension_semantics=("arbitrary", "arbitrary"), vmem_limit_bytes=VMEM_LIMIT),
        name="proj_rwkv",
    )(h, hprev, mu4, w4)


def _proj_hgrn(h, w):
    m, d = h.shape
    n_proj = w.shape[1] // d
    return pl.pallas_call(
        _proj_plain_kernel,
        grid=(n_proj, m // ROW_TILE),
        in_specs=[pl.BlockSpec((ROW_TILE, d), lambda n, i: (i, 0)),
                  pl.BlockSpec((d, d), lambda n, i: (0, n))],
        out_specs=pl.BlockSpec((ROW_TILE, d), lambda n, i: (i, n)),
        out_shape=jax.ShapeDtypeStruct((m, n_proj * d), F32),
        compiler_params=pltpu.CompilerParams(
            dimension_semantics=("arbitrary", "arbitrary"), vmem_limit_bytes=VMEM_LIMIT),
        name="proj_hgrn",
    )(h, w)


def _lora_kernel(has_v, tiles_per_seq, h_ref, hp_ref, mu_ref, w1_ref, a1_ref, v1_ref,
                 wm_ref, am_ref, vm_ref, *carry):
    h = h_ref[...]
    if tiles_per_seq:
        hprev = _prev_rows(h, hp_ref, carry[0], pl.program_id(0), tiles_per_seq)
    else:
        hprev = hp_ref[...]
    delta = hprev - h
    xw = h + delta * mu_ref[4:5, :]
    xa = h + delta * mu_ref[5:6, :]
    wm_ref[...] = jnp.tanh(_mm(xw.astype(BF16), w1_ref[...]))
    am_ref[...] = _mm(xa.astype(BF16), a1_ref[...])
    if has_v:
        xv = h + delta * mu_ref[2:3, :]
        vm_ref[...] = _mm(xv.astype(BF16), v1_ref[...])
    else:
        vm_ref[...] = jnp.zeros(vm_ref.shape, F32)


def _lora(h, hprev, mu6, w1p, a1p, v1p, has_v, tiles_per_seq):
    m, d = h.shape
    row = pl.BlockSpec((ROW_TILE, d), lambda i: (i, 0))
    wspec = pl.BlockSpec((d, LORA_PAD), lambda i: (0, 0))
    ospec = pl.BlockSpec((ROW_TILE, LORA_PAD), lambda i: (i, 0))
    oshape = jax.ShapeDtypeStruct((m, LORA_PAD), F32)
    return pl.pallas_call(
        functools.partial(_lora_kernel, has_v, tiles_per_seq),
        grid=(m // ROW_TILE,),
        in_specs=[row, _prev_spec(hprev, tiles_per_seq, d, 1),
                  pl.BlockSpec(mu6.shape, lambda i: (0, 0)), wspec, wspec, wspec],
        out_specs=[ospec, ospec, ospec],
        out_shape=[oshape, oshape, oshape],
        scratch_shapes=[pltpu.VMEM((8, d), F32)] if tiles_per_seq else [],
        compiler_params=pltpu.CompilerParams(dimension_semantics=("arbitrary",)),
        name="lora",
    )(h, hprev, mu6, w1p, a1p, v1p)


def _outproj_kernel(has_next, g_ref, w_ref, x_ref, npost_ref, npre_ref, xo_ref, ho_ref):
    out = _mm(g_ref[...], w_ref[...])
    ms = jnp.mean(out * out, axis=-1, keepdims=True)
    xn = x_ref[...] + out * lax.rsqrt(ms + NORM_EPS) * npost_ref[...]
    xo_ref[...] = xn
    if has_next:
        ms2 = jnp.mean(xn * xn, axis=-1, keepdims=True)
        ho_ref[...] = xn * lax.rsqrt(ms2 + NORM_EPS) * npre_ref[...]
    else:
        ho_ref[...] = xn


def _outproj(g, w, x, npost, npre_next):
    m, d = x.shape
    has_next = npre_next is not None
    if not has_next:
        npre_next = npost
    row = pl.BlockSpec((ROW_TILE, d), lambda i: (i, 0))
    vec = pl.BlockSpec((1, d), lambda i: (0, 0))
    oshape = jax.ShapeDtypeStruct((m, d), F32)
    return pl.pallas_call(
        functools.partial(_outproj_kernel, has_next),
        grid=(m // ROW_TILE,),
        in_specs=[row, pl.BlockSpec((d, d), lambda i: (0, 0)), row, vec, vec],
        out_specs=[row, row],
        out_shape=[oshape, oshape],
        compiler_params=pltpu.CompilerParams(
            dimension_semantics=("arbitrary",), vmem_limit_bytes=VMEM_LIMIT),
        name="outproj",
    )(g, w, x, npost.reshape(1, d), npre_next.reshape(1, d))


def _rwkv_group(chunk, tvalid, has_vres, n_double, sl, grp, refs, wm, am, vm, ones_blk, diag):
    (r_ref, k_ref, v_ref, z_ref, vf_ref, w2_ref, a2_ref, v2_ref, prm_ref, g_ref, sbd_ref) = refs
    hw = RWKV_HEAD
    nh = HEADS_PER_GROUP
    prm = prm_ref[:, sl]
    w0, a0, v0 = prm[0:1], prm[1:2], prm[2:3]
    kk_w, ka_w, rk_w, ln_w, ln_b = prm[3:4], prm[4:5], prm[5:6], prm[6:7], prm[7:8]
    r = r_ref[:, sl]
    k = k_ref[:, sl]
    v = v_ref[:, sl]

    wl = w0 + _mm(wm, w2_ref[:, sl])
    al = a0 + _mm(am, a2_ref[:, sl])
    if has_vres:
        vl = v0 + _mm(vm, v2_ref[:, sl])
    kk = k * kk_w
    kk_n2 = _segsum(kk * kk, ones_blk)
    yield
    logw = -jnp.exp(-_softplus(-wl) - 0.5)
    if tvalid < chunk:
        logw = jnp.where(_iota(logw.shape, 0) < tvalid, logw, 0.0)
    alpha = _sigmoid(al)
    if has_vres:
        v = v + (vf_ref[:, sl] - v) * _sigmoid(vl)
    kk = kk / jnp.maximum(jnp.sqrt(kk_n2), 1e-12)
    k2 = k * (1.0 + (alpha - 1.0) * ka_w)
    b = kk * alpha
    a = -kk
    cum = _cumsum_rows(logw)
    yield
    e_neg = jnp.exp(-cum)
    a_hat = a * jnp.exp(cum - logw)
    b_hat = b * e_neg
    k_hat = k2 * e_neg
    r_hat = r * jnp.exp(cum)
    cum_last = cum[chunk - 1:chunk, :]
    e_tail = jnp.exp(cum_last - cum)
    b_tail = b * e_tail
    k_tail = k2 * e_tail

    lhs = jnp.concatenate([a_hat, r_hat], axis=0).astype(BF16)
    gram_b = _mm_nt(lhs, _block_diag(b_hat, hw, nh))
    gram_k = _mm_nt(lhs, _block_diag(k_hat, hw, nh))
    s_bd = sbd_ref[grp]
    s_bf = s_bd.astype(BF16)
    x_state = _mm_nt(a_hat.astype(BF16), s_bf)
    y_state = _mm_nt(r_hat.astype(BF16), s_bf)
    yield
    shape_cc = (chunk, nh * chunk)
    t_idx = _iota(shape_cc, 0)
    i_idx = jnp.bitwise_and(_iota(shape_cc, 1), chunk - 1)
    strict = i_idx < t_idx
    incl = i_idx <= t_idx
    n_ab = jnp.where(strict, gram_b[:chunk], 0.0)
    a_ak = jnp.where(strict, gram_k[:chunk], 0.0)
    a_rb = jnp.where(incl, gram_b[chunk:], 0.0)
    a_rk = jnp.where(incl, gram_k[chunk:], 0.0)
    v_bd = _block_diag(v, hw, nh)

    x = x_state + _mm(a_ak.astype(BF16), v_bd)
    y_v = y_state + _mm(a_rk.astype(BF16), v_bd)
    p = n_ab
    if n_double > 1:
        p_next = _mm(p.astype(BF16), _block_diag(p, chunk, nh))
    yield
    for j in range(n_double):
        x = x + _mm(p.astype(BF16), _block_diag(x, hw, nh))
        if j + 1 < n_double:
            p = p_next
            if j + 2 < n_double:
                p_next = _mm(p.astype(BF16), _block_diag(p, chunk, nh))
        yield
    u = x

    y = y_v + _mm(a_rb.astype(BF16), _block_diag(u, hw, nh))
    uv = jnp.concatenate([u, v], axis=0).astype(BF16)
    bk = jnp.concatenate([b_tail, k_tail], axis=0).astype(BF16)
    sbd_ref[grp] = s_bd * jnp.exp(cum_last) + jnp.where(diag, _mm_tn(uv, bk), 0.0)
    bonus_s = _segsum(r * k2 * rk_w, ones_blk)
    yield

    inv_n = 1.0 / RWKV_HEAD
    mean = _segsum(y, ones_blk) * inv_n
    yield
    dev = y - mean
    var = _segsum(dev * dev, ones_blk) * inv_n
    yield
    yn = dev * lax.rsqrt(var + RWKV_GN_EPS) * ln_w + ln_b
    z = z_ref[:, sl]
    g_ref[:, sl] = ((yn + bonus_s * v) * (z * _sigmoid(z))).astype(BF16)


def _rwkv_chunk_kernel(chunk, tvalid, has_vres, n_double, n_groups,
                       r_ref, k_ref, v_ref, z_ref, wm_ref, am_ref, vm_ref, vf_ref,
                       w2_ref, a2_ref, v2_ref, prm_ref, s0_ref, ones_ref,
                       g_ref, so_ref, sbd_ref):
    c = pl.program_id(1)
    n_chunks = pl.num_programs(1)
    hw = RWKV_HEAD
    diag = (lax.shift_right_logical(_iota((GROUP, GROUP), 0), 6)
            == lax.shift_right_logical(_iota((GROUP, GROUP), 1), 6))

    @pl.when(c == 0)
    def _init():
        for g in range(n_groups):
            s4 = s0_ref[g]
            tiled = jnp.concatenate([s4] * HEADS_PER_GROUP, axis=1)
            sbd_ref[g] = jnp.where(diag, tiled, 0.0)

    ones_blk = ones_ref[...]
    wm = wm_ref[...].astype(BF16)
    am = am_ref[...].astype(BF16)
    vm = vm_ref[...].astype(BF16)
    refs = (r_ref, k_ref, v_ref, z_ref, vf_ref, w2_ref, a2_ref, v2_ref, prm_ref, g_ref, sbd_ref)
    _round_robin([
        _rwkv_group(chunk, tvalid, has_vres, n_double, slice(g * GROUP, (g + 1) * GROUP), g, refs,
                    wm, am, vm, ones_blk, diag)
        for g in range(n_groups)])

    @pl.when(c == n_chunks - 1)
    def _fin():
        for g in range(n_groups):
            s_new = sbd_ref[g]
            so_ref[g] = (s_new[:, 0:hw] + s_new[:, hw:2 * hw]
                         + s_new[:, 2 * hw:3 * hw] + s_new[:, 3 * hw:4 * hw])


def _rwkv_chunk(proj, vf_src, wm, am, vm, w2p, a2p, v2p, prm, s0, *, n_seq, n_chunks, chunk,
                tvalid, has_vres):
    m = n_seq * n_chunks * chunk
    d = proj.shape[1] // 4
    n_groups = d // GROUP
    n_double = max(1, math.ceil(math.log2(tvalid)))

    def col(off):
        return pl.BlockSpec((chunk, d), lambda s, c: (s * n_chunks + c, off))

    lora = pl.BlockSpec((chunk, LORA_PAD), lambda s, c: (s * n_chunks + c, 0))
    up = pl.BlockSpec((LORA_PAD, d), lambda s, c: (0, 0))
    state = pl.BlockSpec((None, n_groups, GROUP, RWKV_HEAD), lambda s, c: (s, 0, 0, 0))
    ones_blk = jnp.where(
        lax.shift_right_logical(_iota((GROUP, GROUP), 0), 6)
        == lax.shift_right_logical(_iota((GROUP, GROUP), 1), 6), 1.0, 0.0).astype(BF16)
    kernel = functools.partial(_rwkv_chunk_kernel, chunk, tvalid, has_vres, n_double, n_groups)
    return pl.pallas_call(
        kernel,
        grid=(n_seq, n_chunks),
        in_specs=[col(0), col(1), col(2), col(3), lora, lora, lora, col(2), up, up, up,
                  pl.BlockSpec((8, d), lambda s, c: (0, 0)), state,
                  pl.BlockSpec((GROUP, GROUP), lambda s, c: (0, 0))],
        out_specs=[pl.BlockSpec((chunk, d), lambda s, c: (s * n_chunks + c, 0)), state],
        out_shape=[jax.ShapeDtypeStruct((m, d), BF16),
                   jax.ShapeDtypeStruct(s0.shape, F32)],
        scratch_shapes=[pltpu.VMEM((n_groups, GROUP, GROUP), F32)],
        compiler_params=pltpu.CompilerParams(
            dimension_semantics=("arbitrary", "arbitrary"), vmem_limit_bytes=VMEM_LIMIT),
        name="rwkv_chunk",
    )(proj, proj, proj, proj, wm, am, vm, vf_src, w2p, a2p, v2p, prm, s0, ones_blk)


def _hgrn_head(chunk, tvalid, sl, head, refs, lb, nw, ones_blk):
    (q_ref, f_ref, i_ref, z_ref, g_ref, st_ref) = refs
    sub = min(HGRN_SUB, chunk)
    shift = int(math.log2(sub))
    fl = f_ref[:, sl]
    qraw = q_ref[:, sl]
    val = i_ref[:, sl]
    f = lb + (1.0 - lb) * _sigmoid(fl)
    logf = jnp.log(jnp.maximum(f, GATE_FLOOR))
    if tvalid < chunk:
        logf = jnp.where(_iota(logf.shape, 0) < tvalid, logf, 0.0)
    kg = (1.0 - lb) * _sigmoid(-fl)
    qs = qraw * _sigmoid(qraw) * (HGRN_DK ** -0.5)
    gcum = _cumsum_rows(logf)
    yield

    st = st_ref[head]
    o_inter = _mm_nt((qs * jnp.exp(gcum)).astype(BF16), st.astype(BF16))
    g_last = gcum[chunk - 1:chunk, :]
    k_tail = kg * jnp.exp(g_last - gcum)
    st_ref[head] = st * jnp.exp(g_last) + _mm_tn(val.astype(BF16), k_tail.astype(BF16))

    def bcast_rows(x):
        return jnp.concatenate(
            [jnp.broadcast_to(x[t:t + 1, :], (sub, HGRN_DK)) for t in range(tvalid)], axis=0)

    def tile_blocks(x):
        return jnp.concatenate(
            [x[(t >> shift) * sub:((t >> shift) + 1) * sub] for t in range(tvalid)], axis=0)

    rows = tvalid * sub
    prod = bcast_rows(qs) * tile_blocks(kg) * jnp.exp(
        jnp.minimum(bcast_rows(gcum) - tile_blocks(gcum), 0.0))
    att_diag = _mm(prod.astype(BF16), ones_blk)
    n_blk = -(-tvalid // sub)
    att_off = []
    for bi in range(1, n_blk):
        lo = bi * sub
        g_edge = gcum[lo - 1:lo]
        q_n = (qs[lo:lo + sub] * jnp.exp(gcum[lo:lo + sub] - g_edge)).astype(BF16)
        k_n = (kg[0:lo] * jnp.exp(g_edge - gcum[0:lo])).astype(BF16)
        att_off.append(_mm_nt(q_n, k_n))
    yield

    ridx = _iota((rows, HGRN_DK), 0)
    causal = (jnp.bitwise_and(ridx, sub - 1)
              <= jnp.bitwise_and(lax.shift_right_logical(ridx, shift), sub - 1))
    weighted = jnp.where(causal, att_diag * tile_blocks(val), 0.0)
    sel = jnp.where(lax.shift_right_logical(_iota((chunk, rows), 1), shift) == _iota((chunk, rows), 0),
                    1.0, 0.0).astype(BF16)
    o = o_inter + _mm(sel, weighted.astype(BF16))
    if n_blk > 1:
        parts = [jnp.zeros((sub, HGRN_DK), F32)]
        for bi in range(1, n_blk):
            parts.append(_mm(att_off[bi - 1].astype(BF16), val[0:bi * sub].astype(BF16)))
        if n_blk * sub < chunk:
            parts.append(jnp.zeros((chunk - n_blk * sub, HGRN_DK), F32))
        o = o + jnp.concatenate(parts, axis=0)
    yield

    o = o * lax.rsqrt(jnp.mean(o * o, axis=-1, keepdims=True) + NORM_EPS) * nw
    z = z_ref[:, sl]
    g_ref[:, sl] = (o * (z * _sigmoid(z))).astype(BF16)


def _hgrn_chunk_kernel(chunk, tvalid, layer_j, heads,
                       q_ref, f_ref, i_ref, z_ref, lbl_ref, nw_ref, s0_ref,
                       g_ref, so_ref, st_ref):
    c = pl.program_id(2)
    n_chunks = pl.num_programs(2)

    @pl.when(c == 0)
    def _init():
        for hh in range(heads):
            st_ref[hh] = s0_ref[hh].T

    logits = lbl_ref[...]
    ex = jnp.exp(logits - jnp.max(logits, axis=0, keepdims=True))
    soft = ex / jnp.sum(ex, axis=0, keepdims=True)
    lb_all = jnp.sum(soft[0:layer_j + 1], axis=0, keepdims=True) - soft[0:1]

    ones_blk = jnp.ones((HGRN_DK, HGRN_DK), BF16)
    nw = nw_ref[...]
    refs = (q_ref, f_ref, i_ref, z_ref, g_ref, st_ref)
    _round_robin([
        _hgrn_head(chunk, tvalid, slice(hh * HGRN_DK, (hh + 1) * HGRN_DK), hh, refs,
                   lb_all[:, hh * HGRN_DK:(hh + 1) * HGRN_DK], nw, ones_blk)
        for hh in range(heads)])

    @pl.when(c == n_chunks - 1)
    def _fin():
        for hh in range(heads):
            so_ref[hh] = st_ref[hh].T


def _hgrn_chunk(proj, lb_logits, norm_w, s0, *, layer_j, n_seq, n_chunks, chunk, tvalid):
    m = n_seq * n_chunks * chunk
    d = proj.shape[1] // 4
    n_heads = d // HGRN_DK
    heads = HGRN_HEADS_PER_STEP
    n_hb = n_heads // heads
    width = heads * HGRN_DK

    def col(off):
        return pl.BlockSpec((chunk, width), lambda s, h, c: (s * n_chunks + c, off * n_hb + h))

    state = pl.BlockSpec((None, heads, HGRN_DK, HGRN_DK), lambda s, h, c: (s, h, 0, 0))
    kernel = functools.partial(_hgrn_chunk_kernel, chunk, tvalid, layer_j, heads)
    return pl.pallas_call(
        kernel,
        grid=(n_seq, n_hb, n_chunks),
        in_specs=[col(0), col(1), col(2), col(3),
                  pl.BlockSpec((lb_logits.shape[0], width), lambda s, h, c: (0, h)),
                  pl.BlockSpec((1, HGRN_DK), lambda s, h, c: (0, 0)), state],
        out_specs=[pl.BlockSpec((chunk, width), lambda s, h, c: (s * n_chunks + c, h)), state],
        out_shape=[jax.ShapeDtypeStruct((m, d), BF16),
                   jax.ShapeDtypeStruct(s0.shape, F32)],
        scratch_shapes=[pltpu.VMEM((heads, HGRN_DK, HGRN_DK), F32)],
        compiler_params=pltpu.CompilerParams(
            dimension_semantics=("arbitrary", "arbitrary", "arbitrary"),
            vmem_limit_bytes=VMEM_LIMIT),
        name="hgrn_chunk",
    )(proj, proj, proj, proj, lb_logits, norm_w.reshape(1, HGRN_DK), s0)


def _pad_lanes(w, axis):
    pad = [(0, 0)] * w.ndim
    pad[axis] = (0, LORA_PAD - w.shape[axis])
    return jnp.pad(w, pad)


def _trunk(x, shift0, wkv0, hgrn0, p, *, n_seq, tpad, tvalid, chunk):
    d = x.shape[1]
    n_chunks = tpad // chunk
    n_groups = d // GROUP
    shifts, wkvs, hgrns = [], [], []
    vf_src = None
    h = _rmsnorm(x, p["norm_pre"][0])
    for layer in range(DEPTH):
        j = layer // 2
        npre_next = p["norm_pre"][layer + 1] if layer + 1 < DEPTH else None
        if layer % 2 == 0:
            h3 = h.reshape(n_seq, tpad, d)
            if tvalid == tpad and tpad % ROW_TILE == 0:
                tiles_per_seq = tpad // ROW_TILE
                hprev = shift0[j][:, None, :]
            else:
                tiles_per_seq = 0
                hprev = jnp.concatenate([shift0[j][:, None, :], h3[:, :tvalid - 1]], axis=1)
                hprev = jnp.pad(hprev, ((0, 0), (0, tpad - tvalid), (0, 0)))
                hprev = hprev.reshape(n_seq * tpad, d)
            shifts.append(h3[:, tvalid - 1])
            has_vres = j > 0
            proj = _proj_rwkv(h, hprev, p["mu4"][j], p["w_in"][j], tiles_per_seq)
            wm, am, vm = _lora(h, hprev, p["mu6"][j], p["w1"][j], p["a1"][j],
                               p["v1"][max(j - 1, 0)], has_vres, tiles_per_seq)
            if vf_src is None:
                vf_src = proj
            g, s_fin = _rwkv_chunk(
                proj, vf_src, wm, am, vm, p["w2"][j], p["a2"][j], p["v2"][max(j - 1, 0)],
                p["rwkv_prm"][j], wkv0[j].reshape(n_seq, n_groups, GROUP, RWKV_HEAD),
                n_seq=n_seq, n_chunks=n_chunks, chunk=chunk, tvalid=min(tvalid, chunk),
                has_vres=has_vres)
            wkvs.append(s_fin.reshape(wkv0[j].shape))
            w_o = p["rwkv_w_o"][j]
        else:
            proj = _proj_hgrn(h, p["hgrn_w_in"][j])
            g, s_fin = _hgrn_chunk(
                proj, p["hgrn_lb_logits"], p["hgrn_norm_w"][j], hgrn0[j], layer_j=j,
                n_seq=n_seq, n_chunks=n_chunks, chunk=chunk, tvalid=min(tvalid, chunk))
            hgrns.append(s_fin)
            w_o = p["hgrn_w_o"][j]
        x, h = _outproj(g, w_o, x, p["norm_post"][layer], npre_next)
    return x, jnp.stack(shifts), jnp.stack(wkvs), jnp.stack(hgrns)


def kernel(x_prompt, x_sample, state_rwkv_shift, state_rwkv_wkv, state_hgrn, norm_pre, norm_post,
           rwkv_mu, rwkv_w_in, rwkv_w0, rwkv_w1, rwkv_w2, rwkv_a0, rwkv_a1, rwkv_a2, rwkv_v0,
           rwkv_v1, rwkv_v2, rwkv_k_k, rwkv_k_a, rwkv_r_k, rwkv_ln_w, rwkv_ln_b, rwkv_w_o,
           hgrn_w_in, hgrn_lb_logits, hgrn_norm_w, hgrn_w_o):
    n_rwkv = rwkv_mu.shape[0]
    d = x_prompt.shape[-1]
    v0_full = jnp.concatenate([jnp.zeros((1, d), F32), rwkv_v0], axis=0)
    rwkv_prm = jnp.stack([rwkv_w0, rwkv_a0, v0_full, rwkv_k_k, rwkv_k_a,
                          rwkv_r_k.reshape(n_rwkv, d), rwkv_ln_w, rwkv_ln_b], axis=1)
    p = {
        "norm_pre": norm_pre, "norm_post": norm_post,
        "mu4": rwkv_mu[:, :4, None, :], "mu6": rwkv_mu,
        "w_in": rwkv_w_in.astype(BF16),
        "w1": _pad_lanes(rwkv_w1, 2).astype(BF16), "a1": _pad_lanes(rwkv_a1, 2).astype(BF16),
        "v1": _pad_lanes(rwkv_v1, 2).astype(BF16),
        "w2": _pad_lanes(rwkv_w2, 1).astype(BF16), "a2": _pad_lanes(rwkv_a2, 1).astype(BF16),
        "v2": _pad_lanes(rwkv_v2, 1).astype(BF16),
        "rwkv_prm": rwkv_prm, "rwkv_w_o": rwkv_w_o.astype(BF16),
        "hgrn_w_in": hgrn_w_in.astype(BF16), "hgrn_lb_logits": hgrn_lb_logits,
        "hgrn_norm_w": hgrn_norm_w, "hgrn_w_o": hgrn_w_o.astype(BF16),
    }

    bp, tp, _ = x_prompt.shape
    zero_shift = jnp.zeros((n_rwkv, bp, d), F32)
    zero_wkv = jnp.zeros((n_rwkv, bp) + state_rwkv_wkv.shape[2:], F32)
    zero_hgrn = jnp.zeros((state_hgrn.shape[0], bp) + state_hgrn.shape[2:], F32)
    y_p, p_shift, p_wkv, p_hgrn = _trunk(
        x_prompt.reshape(bp * tp, d), zero_shift, zero_wkv, zero_hgrn, p,
        n_seq=bp, tpad=tp, tvalid=tp, chunk=PROMPT_CHUNK)

    bs, ts, _ = x_sample.shape
    xs = jnp.pad(x_sample, ((0, 0), (0, SAMPLE_TPAD - ts), (0, 0))).reshape(bs * SAMPLE_TPAD, d)
    y_s, s_shift, s_wkv, s_hgrn = _trunk(
        xs, state_rwkv_shift, state_rwkv_wkv, state_hgrn, p,
        n_seq=bs, tpad=SAMPLE_TPAD, tvalid=ts, chunk=SAMPLE_TPAD)

    return (y_p.reshape(bp, tp, d), y_s.reshape(bs, SAMPLE_TPAD, d)[:, :ts],
            p_shift, p_wkv, p_hgrn, s_shift, s_wkv, s_hgrn)
```

```python
import functools
import math

import jax
import jax.numpy as jnp
from jax import lax
from jax.experimental import pallas as pl
from jax.experimental.pallas import tpu as pltpu

F32 = jnp.float32
BF16 = jnp.bfloat16

D_MODEL = 2048
DEPTH = 4
RWKV_HEAD = 64
HGRN_DK = 128
RWKV_GN_EPS = 1e-5 * RWKV_HEAD
NORM_EPS = 1e-6
GATE_FLOOR = 1e-30

LANES = 128
GROUP = 256
HEADS_PER_GROUP = GROUP // RWKV_HEAD
LORA_PAD = 128
PROMPT_CHUNK = 64
SAMPLE_TPAD = 16
HGRN_SUB = 8
HGRN_HEADS_PER_STEP = 8
ROW_TILE = 256
VMEM_LIMIT = 48 * 1024 * 1024


def _mm(a, b):
    return jnp.dot(a, b, preferred_element_type=F32)


def _mm_nt(a, b):
    return lax.dot_general(a, b, (((1,), (1,)), ((), ())), preferred_element_type=F32)


def _mm_tn(a, b):
    return lax.dot_general(a, b, (((0,), (0,)), ((), ())), preferred_element_type=F32)


def _iota(shape, dim):
    return lax.broadcasted_iota(jnp.int32, shape, dim)


def _split2(x):
    hi = x.astype(BF16)
    lo = (x - hi.astype(F32)).astype(BF16)
    return hi, lo


def _segsum(x, ones_blk):
    hi, lo = _split2(x)
    return _mm(hi, ones_blk) + _mm(lo, ones_blk)


def _cumsum_rows(x):
    n = x.shape[0]
    tri = jnp.where(_iota((n, n), 0) >= _iota((n, n), 1), 1.0, 0.0).astype(BF16)
    x1 = x.astype(BF16)
    r1 = x - x1.astype(F32)
    x2 = r1.astype(BF16)
    x3 = (r1 - x2.astype(F32)).astype(BF16)
    return _mm(tri, x1) + _mm(tri, x2) + _mm(tri, x3)


def _block_diag(x, head_width, n_heads):
    shift = int(math.log2(head_width))
    lane_head = lax.shift_right_logical(_iota(x.shape, 1), shift)
    parts = [jnp.where(lane_head == h, x, 0.0).astype(BF16) for h in range(n_heads)]
    return jnp.concatenate(parts, axis=0)


def _sigmoid(x):
    return jax.nn.sigmoid(x)


def _softplus(x):
    return jnp.maximum(x, 0.0) + jnp.log(1.0 + jnp.exp(-jnp.abs(x)))


def _store_state(so_ref, fill, idx, value):
    if fill is None:
        so_ref[idx] = value
        return
    layer, n_layers = fill
    for other in range(n_layers):
        so_ref[other, idx] = value if other == layer else jnp.zeros_like(value)


def _round_robin(gens):
    results = [None] * len(gens)
    active = list(range(len(gens)))
    while active:
        for idx in list(active):
            try:
                next(gens[idx])
            except StopIteration as stop:
                results[idx] = stop.value
                active.remove(idx)
    return results


def _rmsnorm_kernel(x_ref, w_ref, o_ref):
    x = x_ref[...]
    ms = jnp.mean(x * x, axis=-1, keepdims=True)
    o_ref[...] = x * lax.rsqrt(ms + NORM_EPS) * w_ref[...]


def _rmsnorm(x, w):
    m, d = x.shape
    return pl.pallas_call(
        _rmsnorm_kernel,
        grid=(m // ROW_TILE,),
        in_specs=[pl.BlockSpec((ROW_TILE, d), lambda i: (i, 0)),
                  pl.BlockSpec((1, d), lambda i: (0, 0))],
        out_specs=pl.BlockSpec((ROW_TILE, d), lambda i: (i, 0)),
        out_shape=jax.ShapeDtypeStruct((m, d), F32),
        name="rmsnorm",
    )(x, w.reshape(1, d))


def _prev_rows(h, first_ref, carry_ref, tile, tiles_per_seq):
    rolled = pltpu.roll(h, 1, axis=0)
    seq_start = lax.rem(tile, tiles_per_seq) == 0
    edge = jnp.where(seq_start, first_ref[...], carry_ref[7:8, :])
    carry_ref[...] = h[h.shape[0] - 8:, :]
    return jnp.where(_iota(h.shape, 0) == 0, edge, rolled)


def _proj_mix_kernel(tiles_per_seq, h_ref, hp_ref, mu_ref, w_ref, o_ref, *carry):
    h = h_ref[...]
    if tiles_per_seq:
        hprev = _prev_rows(h, hp_ref, carry[0], pl.program_id(1), tiles_per_seq)
    else:
        hprev = hp_ref[...]
    xs = h + (hprev - h) * mu_ref[...]
    o_ref[...] = _mm(xs.astype(BF16), w_ref[...])


def _proj_plain_kernel(h_ref, w_ref, o_ref):
    o_ref[...] = _mm(h_ref[...].astype(BF16), w_ref[...])


def _prev_spec(hprev, tiles_per_seq, d, grid_rank):
    if tiles_per_seq:
        if grid_rank == 2:
            return pl.BlockSpec((None, 1, d), lambda n, i: (i // tiles_per_seq, 0, 0))
        return pl.BlockSpec((None, 1, d), lambda i: (i // tiles_per_seq, 0, 0))
    if grid_rank == 2:
        return pl.BlockSpec((ROW_TILE, d), lambda n, i: (i, 0))
    return pl.BlockSpec((ROW_TILE, d), lambda i: (i, 0))


def _proj_rwkv(h, hprev, mu4, w4, tiles_per_seq):
    m, d = h.shape
    n_proj = w4.shape[0]
    return pl.pallas_call(
        functools.partial(_proj_mix_kernel, tiles_per_seq),
        grid=(n_proj, m // ROW_TILE),
        in_specs=[pl.BlockSpec((ROW_TILE, d), lambda n, i: (i, 0)),
                  _prev_spec(hprev, tiles_per_seq, d, 2),
                  pl.BlockSpec((None, 1, d), lambda n, i: (n, 0, 0)),
                  pl.BlockSpec((None, d, d), lambda n, i: (n, 0, 0))],
        out_specs=pl.BlockSpec((ROW_TILE, d), lambda n, i: (i, n)),
        out_shape=jax.ShapeDtypeStruct((m, n_proj * d), F32),
        scratch_shapes=[pltpu.VMEM((8, d), F32)] if tiles_per_seq else [],
        compiler_params=pltpu.CompilerParams(
            dimension_semantics=("arbitrary", "arbitrary"), vmem_limit_bytes=VMEM_LIMIT),
        name="proj_rwkv",
    )(h, hprev, mu4, w4)


def _proj_hgrn(h, w):
    m, d = h.shape
    n_proj = w.shape[1] // d
    return pl.pallas_call(
        _proj_plain_kernel,
        grid=(n_proj, m // ROW_TILE),
        in_specs=[pl.BlockSpec((ROW_TILE, d), lambda n, i: (i, 0)),
                  pl.BlockSpec((d, d), lambda n, i: (0, n))],
        out_specs=pl.BlockSpec((ROW_TILE, d), lambda n, i: (i, n)),
        out_shape=jax.ShapeDtypeStruct((m, n_proj * d), F32),
        compiler_params=pltpu.CompilerParams(
            dimension_semantics=("arbitrary", "arbitrary"), vmem_limit_bytes=VMEM_LIMIT),
        name="proj_hgrn",
    )(h, w)


def _lora_kernel(has_v, tiles_per_seq, h_ref, hp_ref, mu_ref, w1_ref, a1_ref, v1_ref,
                 wm_ref, am_ref, vm_ref, *carry):
    h = h_ref[...]
    if tiles_per_seq:
        hprev = _prev_rows(h, hp_ref, carry[0], pl.program_id(0), tiles_per_seq)
    else:
        hprev = hp_ref[...]
    delta = hprev - h
    xw = h + delta * mu_ref[4:5, :]
    xa = h + delta * mu_ref[5:6, :]
    wm_ref[...] = jnp.tanh(_mm(xw.astype(BF16), w1_ref[...]))
    am_ref[...] = _mm(xa.astype(BF16), a1_ref[...])
    if has_v:
        xv = h + delta * mu_ref[2:3, :]
        vm_ref[...] = _mm(xv.astype(BF16), v1_ref[...])
    else:
        vm_ref[...] = jnp.zeros(vm_ref.shape, F32)


def _lora(h, hprev, mu6, w1p, a1p, v1p, has_v, tiles_per_seq):
    m, d = h.shape
    row = pl.BlockSpec((ROW_TILE, d), lambda i: (i, 0))
    wspec = pl.BlockSpec((d, LORA_PAD), lambda i: (0, 0))
    ospec = pl.BlockSpec((ROW_TILE, LORA_PAD), lambda i: (i, 0))
    oshape = jax.ShapeDtypeStruct((m, LORA_PAD), F32)
    return pl.pallas_call(
        functools.partial(_lora_kernel, has_v, tiles_per_seq),
        grid=(m // ROW_TILE,),
        in_specs=[row, _prev_spec(hprev, tiles_per_seq, d, 1),
                  pl.BlockSpec(mu6.shape, lambda i: (0, 0)), wspec, wspec, wspec],
        out_specs=[ospec, ospec, ospec],
        out_shape=[oshape, oshape, oshape],
        scratch_shapes=[pltpu.VMEM((8, d), F32)] if tiles_per_seq else [],
        compiler_params=pltpu.CompilerParams(dimension_semantics=("arbitrary",)),
        name="lora",
    )(h, hprev, mu6, w1p, a1p, v1p)


def _outproj_kernel(has_next, g_ref, w_ref, x_ref, npost_ref, npre_ref, xo_ref, ho_ref):
    out = _mm(g_ref[...], w_ref[...])
    ms = jnp.mean(out * out, axis=-1, keepdims=True)
    xn = x_ref[...] + out * lax.rsqrt(ms + NORM_EPS) * npost_ref[...]
    xo_ref[...] = xn
    if has_next:
        ms2 = jnp.mean(xn * xn, axis=-1, keepdims=True)
        ho_ref[...] = xn * lax.rsqrt(ms2 + NORM_EPS) * npre_ref[...]
    else:
        ho_ref[...] = xn


def _outproj(g, w, x, npost, npre_next):
    m, d = x.shape
    has_next = npre_next is not None
    if not has_next:
        npre_next = npost
    row = pl.BlockSpec((ROW_TILE, d), lambda i: (i, 0))
    vec = pl.BlockSpec((1, d), lambda i: (0, 0))
    oshape = jax.ShapeDtypeStruct((m, d), F32)
    return pl.pallas_call(
        functools.partial(_outproj_kernel, has_next),
        grid=(m // ROW_TILE,),
        in_specs=[row, pl.BlockSpec((d, d), lambda i: (0, 0)), row, vec, vec],
        out_specs=[row, row],
        out_shape=[oshape, oshape],
        compiler_params=pltpu.CompilerParams(
            dimension_semantics=("arbitrary",), vmem_limit_bytes=VMEM_LIMIT),
        name="outproj",
    )(g, w, x, npost.reshape(1, d), npre_next.reshape(1, d))


def _rwkv_group(chunk, tvalid, has_vres, n_double, sl, grp, refs, wm, am, vm, ones_blk, diag):
    (r_ref, k_ref, v_ref, z_ref, vf_ref, w2_ref, a2_ref, v2_ref, prm_ref, g_ref, sbd_ref) = refs
    hw = RWKV_HEAD
    nh = HEADS_PER_GROUP
    prm = prm_ref[:, sl]
    w0, a0, v0 = prm[0:1], prm[1:2], prm[2:3]
    kk_w, ka_w, rk_w, ln_w, ln_b = prm[3:4], prm[4:5], prm[5:6], prm[6:7], prm[7:8]
    r = r_ref[:, sl]
    k = k_ref[:, sl]
    v = v_ref[:, sl]

    wl = w0 + _mm(wm, w2_ref[:, sl])
    al = a0 + _mm(am, a2_ref[:, sl])
    if has_vres:
        vl = v0 + _mm(vm, v2_ref[:, sl])
    kk = k * kk_w
    kk_n2 = _segsum(kk * kk, ones_blk)
    yield
    logw = -jnp.exp(-_softplus(-wl) - 0.5)
    if tvalid < chunk:
        logw = jnp.where(_iota(logw.shape, 0) < tvalid, logw, 0.0)
    alpha = _sigmoid(al)
    if has_vres:
        v = v + (vf_ref[:, sl] - v) * _sigmoid(vl)
    kk = kk / jnp.maximum(jnp.sqrt(kk_n2), 1e-12)
    k2 = k * (1.0 + (alpha - 1.0) * ka_w)
    b = kk * alpha
    a = -kk
    cum = _cumsum_rows(logw)
    yield
    e_neg = jnp.exp(-cum)
    a_hat = a * jnp.exp(cum - logw)
    b_hat = b * e_neg
    k_hat = k2 * e_neg
    r_hat = r * jnp.exp(cum)
    cum_last = cum[chunk - 1:chunk, :]
    e_tail = jnp.exp(cum_last - cum)
    b_tail = b * e_tail
    k_tail = k2 * e_tail

    lhs = jnp.concatenate([a_hat, r_hat], axis=0).astype(BF16)
    gram_b = _mm_nt(lhs, _block_diag(b_hat, hw, nh))
    gram_k = _mm_nt(lhs, _block_diag(k_hat, hw, nh))
    s_bd = sbd_ref[grp]
    s_bf = s_bd.astype(BF16)
    x_state = _mm_nt(a_hat.astype(BF16), s_bf)
    y_state = _mm_nt(r_hat.astype(BF16), s_bf)
    yield
    shape_cc = (chunk, nh * chunk)
    t_idx = _iota(shape_cc, 0)
    i_idx = jnp.bitwise_and(_iota(shape_cc, 1), chunk - 1)
    strict = i_idx < t_idx
    incl = i_idx <= t_idx
    n_ab = jnp.where(strict, gram_b[:chunk], 0.0)
    a_ak = jnp.where(strict, gram_k[:chunk], 0.0)
    a_rb = jnp.where(incl, gram_b[chunk:], 0.0)
    a_rk = jnp.where(incl, gram_k[chunk:], 0.0)
    v_bd = _block_diag(v, hw, nh)

    x = x_state + _mm(a_ak.astype(BF16), v_bd)
    y_v = y_state + _mm(a_rk.astype(BF16), v_bd)
    p = n_ab
    if n_double > 1:
        p_next = _mm(p.astype(BF16), _block_diag(p, chunk, nh))
    yield
    for j in range(n_double):
        x = x + _mm(p.astype(BF16), _block_diag(x, hw, nh))
        if j + 1 < n_double:
            p = p_next
            if j + 2 < n_double:
                p_next = _mm(p.astype(BF16), _block_diag(p, chunk, nh))
        yield
    u = x

    y = y_v + _mm(a_rb.astype(BF16), _block_diag(u, hw, nh))
    uv = jnp.concatenate([u, v], axis=0).astype(BF16)
    bk = jnp.concatenate([b_tail, k_tail], axis=0).astype(BF16)
    sbd_ref[grp] = s_bd * jnp.exp(cum_last) + jnp.where(diag, _mm_tn(uv, bk), 0.0)
    bonus_s = _segsum(r * k2 * rk_w, ones_blk)
    yield

    inv_n = 1.0 / RWKV_HEAD
    mean = _segsum(y, ones_blk) * inv_n
    yield
    dev = y - mean
    var = _segsum(dev * dev, ones_blk) * inv_n
    yield
    yn = dev * lax.rsqrt(var + RWKV_GN_EPS) * ln_w + ln_b
    z = z_ref[:, sl]
    g_ref[:, sl] = ((yn + bonus_s * v) * (z * _sigmoid(z))).astype(BF16)


def _rwkv_chunk_kernel(chunk, tvalid, has_vres, n_double, n_groups, fill,
                       r_ref, k_ref, v_ref, z_ref, wm_ref, am_ref, vm_ref, vf_ref,
                       w2_ref, a2_ref, v2_ref, prm_ref, s0_ref, ones_ref, *rest):
    g_ref, so_ref, sbd_ref = rest[-3:]
    c = pl.program_id(1)
    n_chunks = pl.num_programs(1)
    hw = RWKV_HEAD
    diag = (lax.shift_right_logical(_iota((GROUP, GROUP), 0), 6)
            == lax.shift_right_logical(_iota((GROUP, GROUP), 1), 6))

    @pl.when(c == 0)
    def _init():
        for g in range(n_groups):
            s4 = s0_ref[g]
            tiled = jnp.concatenate([s4] * HEADS_PER_GROUP, axis=1)
            sbd_ref[g] = jnp.where(diag, tiled, 0.0)

    ones_blk = ones_ref[...]
    wm = wm_ref[...].astype(BF16)
    am = am_ref[...].astype(BF16)
    vm = vm_ref[...].astype(BF16)
    refs = (r_ref, k_ref, v_ref, z_ref, vf_ref, w2_ref, a2_ref, v2_ref, prm_ref, g_ref, sbd_ref)
    _round_robin([
        _rwkv_group(chunk, tvalid, has_vres, n_double, slice(g * GROUP, (g + 1) * GROUP), g, refs,
                    wm, am, vm, ones_blk, diag)
        for g in range(n_groups)])

    @pl.when(c == n_chunks - 1)
    def _fin():
        for g in range(n_groups):
            s_new = sbd_ref[g]
            _store_state(so_ref, fill, g, s_new[:, 0:hw] + s_new[:, hw:2 * hw]
                         + s_new[:, 2 * hw:3 * hw] + s_new[:, 3 * hw:4 * hw])


def _rwkv_chunk(proj, vf_src, wm, am, vm, w2p, a2p, v2p, prm, s0, s_out_prev, *, layer_j, n_seq,
                n_chunks, chunk, tvalid, has_vres):
    m = n_seq * n_chunks * chunk
    d = proj.shape[1] // 4
    n_groups = d // GROUP
    n_double = max(1, math.ceil(math.log2(tvalid)))

    def col(off):
        return pl.BlockSpec((chunk, d), lambda s, c: (s * n_chunks + c, off))

    lora = pl.BlockSpec((chunk, LORA_PAD), lambda s, c: (s * n_chunks + c, 0))
    up = pl.BlockSpec((LORA_PAD, d), lambda s, c: (0, 0))
    state = pl.BlockSpec((None, None, n_groups, GROUP, RWKV_HEAD),
                         lambda s, c: (layer_j, s, 0, 0, 0))
    prev = [] if s_out_prev is None else [s_out_prev]
    fill = None if prev else (layer_j, s0.shape[0])
    state_out = state if prev else pl.BlockSpec(
        (s0.shape[0], None, n_groups, GROUP, RWKV_HEAD), lambda s, c: (0, s, 0, 0, 0))
    ones_blk = jnp.where(
        lax.shift_right_logical(_iota((GROUP, GROUP), 0), 6)
        == lax.shift_right_logical(_iota((GROUP, GROUP), 1), 6), 1.0, 0.0).astype(BF16)
    kernel = functools.partial(_rwkv_chunk_kernel, chunk, tvalid, has_vres, n_double, n_groups, fill)
    return pl.pallas_call(
        kernel,
        grid=(n_seq, n_chunks),
        in_specs=[col(0), col(1), col(2), col(3), lora, lora, lora, col(2), up, up, up,
                  pl.BlockSpec((8, d), lambda s, c: (0, 0)), state,
                  pl.BlockSpec((GROUP, GROUP), lambda s, c: (0, 0))]
        + [pl.BlockSpec(memory_space=pl.ANY)] * len(prev),
        out_specs=[pl.BlockSpec((chunk, d), lambda s, c: (s * n_chunks + c, 0)), state_out],
        out_shape=[jax.ShapeDtypeStruct((m, d), BF16),
                   jax.ShapeDtypeStruct(s0.shape, F32)],
        scratch_shapes=[pltpu.VMEM((n_groups, GROUP, GROUP), F32)],
        input_output_aliases={14: 1} if prev else {},
        compiler_params=pltpu.CompilerParams(
            dimension_semantics=("arbitrary", "arbitrary"), vmem_limit_bytes=VMEM_LIMIT),
        name="rwkv_chunk",
    )(proj, proj, proj, proj, wm, am, vm, vf_src, w2p, a2p, v2p, prm, s0, ones_blk, *prev)


def _hgrn_head(chunk, tvalid, sl, head, refs, lb, nw, ones_blk):
    (q_ref, f_ref, i_ref, z_ref, g_ref, st_ref) = refs
    sub = min(HGRN_SUB, chunk)
    shift = int(math.log2(sub))
    fl = f_ref[:, sl]
    qraw = q_ref[:, sl]
    val = i_ref[:, sl]
    f = lb + (1.0 - lb) * _sigmoid(fl)
    logf = jnp.log(jnp.maximum(f, GATE_FLOOR))
    if tvalid < chunk:
        logf = jnp.where(_iota(logf.shape, 0) < tvalid, logf, 0.0)
    kg = (1.0 - lb) * _sigmoid(-fl)
    qs = qraw * _sigmoid(qraw) * (HGRN_DK ** -0.5)
    gcum = _cumsum_rows(logf)
    yield

    st = st_ref[head]
    o_inter = _mm_nt((qs * jnp.exp(gcum)).astype(BF16), st.astype(BF16))
    g_last = gcum[chunk - 1:chunk, :]
    k_tail = kg * jnp.exp(g_last - gcum)
    st_ref[head] = st * jnp.exp(g_last) + _mm_tn(val.astype(BF16), k_tail.astype(BF16))

    def bcast_rows(x):
        return jnp.concatenate(
            [jnp.broadcast_to(x[t:t + 1, :], (sub, HGRN_DK)) for t in range(tvalid)], axis=0)

    def tile_blocks(x):
        return jnp.concatenate(
            [x[(t >> shift) * sub:((t >> shift) + 1) * sub] for t in range(tvalid)], axis=0)

    rows = tvalid * sub
    prod = bcast_rows(qs) * tile_blocks(kg) * jnp.exp(
        jnp.minimum(bcast_rows(gcum) - tile_blocks(gcum), 0.0))
    att_diag = _mm(prod.astype(BF16), ones_blk)
    n_blk = -(-tvalid // sub)
    att_off = []
    for bi in range(1, n_blk):
        lo = bi * sub
        g_edge = gcum[lo - 1:lo]
        q_n = (qs[lo:lo + sub] * jnp.exp(gcum[lo:lo + sub] - g_edge)).astype(BF16)
        k_n = (kg[0:lo] * jnp.exp(g_edge - gcum[0:lo])).astype(BF16)
        att_off.append(_mm_nt(q_n, k_n))
    yield

    ridx = _iota((rows, HGRN_DK), 0)
    causal = (jnp.bitwise_and(ridx, sub - 1)
              <= jnp.bitwise_and(lax.shift_right_logical(ridx, shift), sub - 1))
    weighted = jnp.where(causal, att_diag * tile_blocks(val), 0.0)
    sel = jnp.where(lax.shift_right_logical(_iota((chunk, rows), 1), shift) == _iota((chunk, rows), 0),
                    1.0, 0.0).astype(BF16)
    o = o_inter + _mm(sel, weighted.astype(BF16))
    if n_blk > 1:
        parts = [jnp.zeros((sub, HGRN_DK), F32)]
        for bi in range(1, n_blk):
            parts.append(_mm(att_off[bi - 1].astype(BF16), val[0:bi * sub].astype(BF16)))
        if n_blk * sub < chunk:
            parts.append(jnp.zeros((chunk - n_blk * sub, HGRN_DK), F32))
        o = o + jnp.concatenate(parts, axis=0)
    yield

    o = o * lax.rsqrt(jnp.mean(o * o, axis=-1, keepdims=True) + NORM_EPS) * nw
    z = z_ref[:, sl]
    g_ref[:, sl] = (o * (z * _sigmoid(z))).astype(BF16)


def _hgrn_chunk_kernel(chunk, tvalid, layer_j, heads, fill,
                       q_ref, f_ref, i_ref, z_ref, lbl_ref, nw_ref, s0_ref, *rest):
    g_ref, so_ref, st_ref = rest[-3:]
    c = pl.program_id(2)
    n_chunks = pl.num_programs(2)

    @pl.when(c == 0)
    def _init():
        for hh in range(heads):
            st_ref[hh] = s0_ref[hh].T

    logits = lbl_ref[...]
    ex = jnp.exp(logits - jnp.max(logits, axis=0, keepdims=True))
    soft = ex / jnp.sum(ex, axis=0, keepdims=True)
    lb_all = jnp.sum(soft[0:layer_j + 1], axis=0, keepdims=True) - soft[0:1]

    ones_blk = jnp.ones((HGRN_DK, HGRN_DK), BF16)
    nw = nw_ref[...]
    refs = (q_ref, f_ref, i_ref, z_ref, g_ref, st_ref)
    _round_robin([
        _hgrn_head(chunk, tvalid, slice(hh * HGRN_DK, (hh + 1) * HGRN_DK), hh, refs,
                   lb_all[:, hh * HGRN_DK:(hh + 1) * HGRN_DK], nw, ones_blk)
        for hh in range(heads)])

    @pl.when(c == n_chunks - 1)
    def _fin():
        for hh in range(heads):
            _store_state(so_ref, fill, hh, st_ref[hh].T)


def _hgrn_chunk(proj, lb_logits, norm_w, s0, s_out_prev, *, layer_j, n_seq, n_chunks, chunk, tvalid):
    m = n_seq * n_chunks * chunk
    d = proj.shape[1] // 4
    n_heads = d // HGRN_DK
    heads = HGRN_HEADS_PER_STEP
    n_hb = n_heads // heads
    width = heads * HGRN_DK

    def col(off):
        return pl.BlockSpec((chunk, width), lambda s, h, c: (s * n_chunks + c, off * n_hb + h))

    state = pl.BlockSpec((None, None, heads, HGRN_DK, HGRN_DK), lambda s, h, c: (layer_j, s, h, 0, 0))
    prev = [] if s_out_prev is None else [s_out_prev]
    fill = None if prev else (layer_j, s0.shape[0])
    state_out = state if prev else pl.BlockSpec(
        (s0.shape[0], None, heads, HGRN_DK, HGRN_DK), lambda s, h, c: (0, s, h, 0, 0))
    kernel = functools.partial(_hgrn_chunk_kernel, chunk, tvalid, layer_j, heads, fill)
    return pl.pallas_call(
        kernel,
        grid=(n_seq, n_hb, n_chunks),
        in_specs=[col(0), col(1), col(2), col(3),
                  pl.BlockSpec((lb_logits.shape[0], width), lambda s, h, c: (0, h)),
                  pl.BlockSpec((1, HGRN_DK), lambda s, h, c: (0, 0)), state]
        + [pl.BlockSpec(memory_space=pl.ANY)] * len(prev),
        out_specs=[pl.BlockSpec((chunk, width), lambda s, h, c: (s * n_chunks + c, h)), state_out],
        out_shape=[jax.ShapeDtypeStruct((m, d), BF16),
                   jax.ShapeDtypeStruct(s0.shape, F32)],
        scratch_shapes=[pltpu.VMEM((heads, HGRN_DK, HGRN_DK), F32)],
        input_output_aliases={7: 1} if prev else {},
        compiler_params=pltpu.CompilerParams(
            dimension_semantics=("arbitrary", "arbitrary", "arbitrary"),
            vmem_limit_bytes=VMEM_LIMIT),
        name="hgrn_chunk",
    )(proj, proj, proj, proj, lb_logits, norm_w.reshape(1, HGRN_DK), s0, *prev)


def _pad_lanes(w, axis):
    pad = [(0, 0)] * w.ndim
    pad[axis] = (0, LORA_PAD - w.shape[axis])
    return jnp.pad(w, pad)


def _trunk(x, shift0, wkv0, hgrn0, p, *, n_seq, tpad, tvalid, chunk):
    d = x.shape[1]
    n_chunks = tpad // chunk
    n_groups = d // GROUP
    shifts = []
    wkv_in = wkv0.reshape(wkv0.shape[:2] + (n_groups, GROUP, RWKV_HEAD))
    wkv_out, hgrn_out = None, None
    vf_src = None
    h = _rmsnorm(x, p["norm_pre"][0])
    for layer in range(DEPTH):
        j = layer // 2
        npre_next = p["norm_pre"][layer + 1] if layer + 1 < DEPTH else None
        if layer % 2 == 0:
            h3 = h.reshape(n_seq, tpad, d)
            if tvalid == tpad and tpad % ROW_TILE == 0:
                tiles_per_seq = tpad // ROW_TILE
                hprev = shift0[j][:, None, :]
            else:
                tiles_per_seq = 0
                hprev = jnp.concatenate([shift0[j][:, None, :], h3[:, :tvalid - 1]], axis=1)
                hprev = jnp.pad(hprev, ((0, 0), (0, tpad - tvalid), (0, 0)))
                hprev = hprev.reshape(n_seq * tpad, d)
            shifts.append(h3[:, tvalid - 1])
            has_vres = j > 0
            proj = _proj_rwkv(h, hprev, p["mu4"][j], p["w_in"][j], tiles_per_seq)
            wm, am, vm = _lora(h, hprev, p["mu6"][j], p["w1"][j], p["a1"][j],
                               p["v1"][max(j - 1, 0)], has_vres, tiles_per_seq)
            if vf_src is None:
                vf_src = proj
            g, wkv_out = _rwkv_chunk(
                proj, vf_src, wm, am, vm, p["w2"][j], p["a2"][j], p["v2"][max(j - 1, 0)],
                p["rwkv_prm"][j], wkv_in, wkv_out, layer_j=j, n_seq=n_seq, n_chunks=n_chunks,
                chunk=chunk, tvalid=min(tvalid, chunk), has_vres=has_vres)
            w_o = p["rwkv_w_o"][j]
        else:
            proj = _proj_hgrn(h, p["hgrn_w_in"][j])
            g, hgrn_out = _hgrn_chunk(
                proj, p["hgrn_lb_logits"], p["hgrn_norm_w"][j], hgrn0, hgrn_out, layer_j=j,
                n_seq=n_seq, n_chunks=n_chunks, chunk=chunk, tvalid=min(tvalid, chunk))
            w_o = p["hgrn_w_o"][j]
        x, h = _outproj(g, w_o, x, p["norm_post"][layer], npre_next)
    return x, jnp.stack(shifts), wkv_out.reshape(wkv0.shape), hgrn_out


def kernel(x_prompt, x_sample, state_rwkv_shift, state_rwkv_wkv, state_hgrn, norm_pre, norm_post,
           rwkv_mu, rwkv_w_in, rwkv_w0, rwkv_w1, rwkv_w2, rwkv_a0, rwkv_a1, rwkv_a2, rwkv_v0,
           rwkv_v1, rwkv_v2, rwkv_k_k, rwkv_k_a, rwkv_r_k, rwkv_ln_w, rwkv_ln_b, rwkv_w_o,
           hgrn_w_in, hgrn_lb_logits, hgrn_norm_w, hgrn_w_o):
    n_rwkv = rwkv_mu.shape[0]
    d = x_prompt.shape[-1]
    v0_full = jnp.concatenate([jnp.zeros((1, d), F32), rwkv_v0], axis=0)
    rwkv_prm = jnp.stack([rwkv_w0, rwkv_a0, v0_full, rwkv_k_k, rwkv_k_a,
                          rwkv_r_k.reshape(n_rwkv, d), rwkv_ln_w, rwkv_ln_b], axis=1)
    p = {
        "norm_pre": norm_pre, "norm_post": norm_post,
        "mu4": rwkv_mu[:, :4, None, :], "mu6": rwkv_mu,
        "w_in": rwkv_w_in.astype(BF16),
        "w1": _pad_lanes(rwkv_w1, 2).astype(BF16), "a1": _pad_lanes(rwkv_a1, 2).astype(BF16),
        "v1": _pad_lanes(rwkv_v1, 2).astype(BF16),
        "w2": _pad_lanes(rwkv_w2, 1).astype(BF16), "a2": _pad_lanes(rwkv_a2, 1).astype(BF16),
        "v2": _pad_lanes(rwkv_v2, 1).astype(BF16),
        "rwkv_prm": rwkv_prm, "rwkv_w_o": rwkv_w_o.astype(BF16),
        "hgrn_w_in": hgrn_w_in.astype(BF16), "hgrn_lb_logits": hgrn_lb_logits,
        "hgrn_norm_w": hgrn_norm_w, "hgrn_w_o": hgrn_w_o.astype(BF16),
    }

    bp, tp, _ = x_prompt.shape
    zero_shift = jnp.zeros((n_rwkv, bp, d), F32)
    zero_wkv = jnp.zeros((n_rwkv, bp) + state_rwkv_wkv.shape[2:], F32)
    zero_hgrn = jnp.zeros((state_hgrn.shape[0], bp) + state_hgrn.shape[2:], F32)
    y_p, p_shift, p_wkv, p_hgrn = _trunk(
        x_prompt.reshape(bp * tp, d), zero_shift, zero_wkv, zero_hgrn, p,
        n_seq=bp, tpad=tp, tvalid=tp, chunk=PROMPT_CHUNK)

    bs, ts, _ = x_sample.shape
    xs = jnp.pad(x_sample, ((0, 0), (0, SAMPLE_TPAD - ts), (0, 0))).reshape(bs * SAMPLE_TPAD, d)
    y_s, s_shift, s_wkv, s_hgrn = _trunk(
        xs, state_rwkv_shift, state_rwkv_wkv, state_hgrn, p,
        n_seq=bs, tpad=SAMPLE_TPAD, tvalid=ts, chunk=SAMPLE_TPAD)

    return (y_p.reshape(bp, tp, d), y_s.reshape(bs, SAMPLE_TPAD, d)[:, :ts],
            p_shift, p_wkv, p_hgrn, s_shift, s_wkv, s_hgrn)
```

```python
import functools
import math

import jax
import jax.numpy as jnp
from jax import lax
from jax.experimental import pallas as pl
from jax.experimental.pallas import tpu as pltpu

F32 = jnp.float32
BF16 = jnp.bfloat16

D_MODEL = 2048
DEPTH = 4
RWKV_HEAD = 64
HGRN_DK = 128
RWKV_GN_EPS = 1e-5 * RWKV_HEAD
NORM_EPS = 1e-6
GATE_FLOOR = 1e-30

LANES = 128
GROUP = 256
HEADS_PER_GROUP = GROUP // RWKV_HEAD
LORA_PAD = 128
PROMPT_CHUNK = 64
SAMPLE_TPAD = 16
HGRN_SUB = 8
HGRN_HEADS_PER_STEP = 8
ROW_TILE = 512
VMEM_LIMIT = 48 * 1024 * 1024


def _mm(a, b):
    return jnp.dot(a, b, preferred_element_type=F32)


def _mm_nt(a, b):
    return lax.dot_general(a, b, (((1,), (1,)), ((), ())), preferred_element_type=F32)


def _mm_tn(a, b):
    return lax.dot_general(a, b, (((0,), (0,)), ((), ())), preferred_element_type=F32)


def _iota(shape, dim):
    return lax.broadcasted_iota(jnp.int32, shape, dim)


def _split2(x):
    hi = x.astype(BF16)
    lo = (x - hi.astype(F32)).astype(BF16)
    return hi, lo


def _segsum(x, ones_blk):
    hi, lo = _split2(x)
    return _mm(hi, ones_blk) + _mm(lo, ones_blk)


def _segsum_many(xs, ones_blk):
    pieces = []
    for x in xs:
        pieces.extend(_split2(x))
    out = _mm(jnp.concatenate(pieces, axis=0), ones_blk)
    n = xs[0].shape[0]
    return [out[2 * i * n:(2 * i + 1) * n] + out[(2 * i + 1) * n:(2 * i + 2) * n] for i in range(len(xs))]


def _cumsum_rows(x):
    n = x.shape[0]
    tri = jnp.where(_iota((n, n), 0) >= _iota((n, n), 1), 1.0, 0.0).astype(BF16)
    x1 = x.astype(BF16)
    r1 = x - x1.astype(F32)
    x2 = r1.astype(BF16)
    x3 = (r1 - x2.astype(F32)).astype(BF16)
    return _mm(tri, x1) + _mm(tri, x2) + _mm(tri, x3)


def _block_diag(x, head_width, n_heads):
    shift = int(math.log2(head_width))
    lane_head = lax.shift_right_logical(_iota(x.shape, 1), shift)
    parts = [jnp.where(lane_head == h, x, 0.0).astype(BF16) for h in range(n_heads)]
    return jnp.concatenate(parts, axis=0)


def _sigmoid(x):
    return jax.nn.sigmoid(x)


def _softplus(x):
    return jnp.maximum(x, 0.0) + jnp.log(1.0 + jnp.exp(-jnp.abs(x)))


def _store_state(so_ref, fill, idx, value):
    if fill is None:
        so_ref[idx] = value
        return
    layer, n_layers = fill
    for other in range(n_layers):
        so_ref[other, idx] = value if other == layer else jnp.zeros_like(value)


def _round_robin(gens):
    results = [None] * len(gens)
    active = list(range(len(gens)))
    while active:
        for idx in list(active):
            try:
                next(gens[idx])
            except StopIteration as stop:
                results[idx] = stop.value
                active.remove(idx)
    return results


def _rmsnorm_kernel(x_ref, w_ref, o_ref):
    x = x_ref[...]
    ms = jnp.mean(x * x, axis=-1, keepdims=True)
    o_ref[...] = x * lax.rsqrt(ms + NORM_EPS) * w_ref[...]


def _rmsnorm(x, w):
    m, d = x.shape
    return pl.pallas_call(
        _rmsnorm_kernel,
        grid=(m // ROW_TILE,),
        in_specs=[pl.BlockSpec((ROW_TILE, d), lambda i: (i, 0)),
                  pl.BlockSpec((1, d), lambda i: (0, 0))],
        out_specs=pl.BlockSpec((ROW_TILE, d), lambda i: (i, 0)),
        out_shape=jax.ShapeDtypeStruct((m, d), F32),
        name="rmsnorm",
    )(x, w.reshape(1, d))


def _prev_rows(h, first_ref, carry_ref, tile, tiles_per_seq):
    rolled = pltpu.roll(h, 1, axis=0)
    seq_start = lax.rem(tile, tiles_per_seq) == 0
    edge = jnp.where(seq_start, first_ref[...], carry_ref[7:8, :])
    carry_ref[...] = h[h.shape[0] - 8:, :]
    return jnp.where(_iota(h.shape, 0) == 0, edge, rolled)


def _proj_mix_kernel(tiles_per_seq, h_ref, hp_ref, mu_ref, w_ref, o_ref, *carry):
    h = h_ref[...]
    if tiles_per_seq:
        hprev = _prev_rows(h, hp_ref, carry[0], pl.program_id(1), tiles_per_seq)
    else:
        hprev = hp_ref[...]
    xs = h + (hprev - h) * mu_ref[...]
    o_ref[...] = _mm(xs.astype(BF16), w_ref[...])


def _proj_plain_kernel(h_ref, w_ref, o_ref):
    o_ref[...] = _mm(h_ref[...].astype(BF16), w_ref[...])


def _prev_spec(hprev, tiles_per_seq, d, grid_rank):
    if tiles_per_seq:
        if grid_rank == 2:
            return pl.BlockSpec((None, 1, d), lambda n, i: (i // tiles_per_seq, 0, 0))
        return pl.BlockSpec((None, 1, d), lambda i: (i // tiles_per_seq, 0, 0))
    if grid_rank == 2:
        return pl.BlockSpec((ROW_TILE, d), lambda n, i: (i, 0))
    return pl.BlockSpec((ROW_TILE, d), lambda i: (i, 0))


def _proj_rwkv(h, hprev, mu4, w4, tiles_per_seq):
    m, d = h.shape
    n_proj = w4.shape[0]
    return pl.pallas_call(
        functools.partial(_proj_mix_kernel, tiles_per_seq),
        grid=(n_proj, m // ROW_TILE),
        in_specs=[pl.BlockSpec((ROW_TILE, d), lambda n, i: (i, 0)),
                  _prev_spec(hprev, tiles_per_seq, d, 2),
                  pl.BlockSpec((None, 1, d), lambda n, i: (n, 0, 0)),
                  pl.BlockSpec((None, d, d), lambda n, i: (n, 0, 0))],
        out_specs=pl.BlockSpec((ROW_TILE, d), lambda n, i: (i, n)),
        out_shape=jax.ShapeDtypeStruct((m, n_proj * d), F32),
        scratch_shapes=[pltpu.VMEM((8, d), F32)] if tiles_per_seq else [],
        compiler_params=pltpu.CompilerParams(
            dimension_semantics=("arbitrary", "arbitrary"), vmem_limit_bytes=VMEM_LIMIT),
        name="proj_rwkv",
    )(h, hprev, mu4, w4)


def _proj_hgrn(h, w):
    m, d = h.shape
    n_proj = w.shape[1] // d
    return pl.pallas_call(
        _proj_plain_kernel,
        grid=(n_proj, m // ROW_TILE),
        in_specs=[pl.BlockSpec((ROW_TILE, d), lambda n, i: (i, 0)),
                  pl.BlockSpec((d, d), lambda n, i: (0, n))],
        out_specs=pl.BlockSpec((ROW_TILE, d), lambda n, i: (i, n)),
        out_shape=jax.ShapeDtypeStruct((m, n_proj * d), F32),
        compiler_params=pltpu.CompilerParams(
            dimension_semantics=("arbitrary", "arbitrary"), vmem_limit_bytes=VMEM_LIMIT),
        name="proj_hgrn",
    )(h, w)


def _lora_kernel(has_v, tiles_per_seq, h_ref, hp_ref, mu_ref, w1_ref, a1_ref, v1_ref,
                 wm_ref, am_ref, vm_ref, *carry):
    h = h_ref[...]
    if tiles_per_seq:
        hprev = _prev_rows(h, hp_ref, carry[0], pl.program_id(0), tiles_per_seq)
    else:
        hprev = hp_ref[...]
    delta = hprev - h
    xw = h + delta * mu_ref[4:5, :]
    xa = h + delta * mu_ref[5:6, :]
    wm_ref[...] = jnp.tanh(_mm(xw.astype(BF16), w1_ref[...]))
    am_ref[...] = _mm(xa.astype(BF16), a1_ref[...])
    if has_v:
        xv = h + delta * mu_ref[2:3, :]
        vm_ref[...] = _mm(xv.astype(BF16), v1_ref[...])
    else:
        vm_ref[...] = jnp.zeros(vm_ref.shape, F32)


def _lora(h, hprev, mu6, w1p, a1p, v1p, has_v, tiles_per_seq):
    m, d = h.shape
    row = pl.BlockSpec((ROW_TILE, d), lambda i: (i, 0))
    wspec = pl.BlockSpec((d, LORA_PAD), lambda i: (0, 0))
    ospec = pl.BlockSpec((ROW_TILE, LORA_PAD), lambda i: (i, 0))
    oshape = jax.ShapeDtypeStruct((m, LORA_PAD), F32)
    return pl.pallas_call(
        functools.partial(_lora_kernel, has_v, tiles_per_seq),
        grid=(m // ROW_TILE,),
        in_specs=[row, _prev_spec(hprev, tiles_per_seq, d, 1),
                  pl.BlockSpec(mu6.shape, lambda i: (0, 0)), wspec, wspec, wspec],
        out_specs=[ospec, ospec, ospec],
        out_shape=[oshape, oshape, oshape],
        scratch_shapes=[pltpu.VMEM((8, d), F32)] if tiles_per_seq else [],
        compiler_params=pltpu.CompilerParams(dimension_semantics=("arbitrary",)),
        name="lora",
    )(h, hprev, mu6, w1p, a1p, v1p)


def _outproj_kernel(has_next, g_ref, w_ref, x_ref, npost_ref, npre_ref, xo_ref, ho_ref):
    out = _mm(g_ref[...], w_ref[...])
    ms = jnp.mean(out * out, axis=-1, keepdims=True)
    xn = x_ref[...] + out * lax.rsqrt(ms + NORM_EPS) * npost_ref[...]
    xo_ref[...] = xn
    if has_next:
        ms2 = jnp.mean(xn * xn, axis=-1, keepdims=True)
        ho_ref[...] = xn * lax.rsqrt(ms2 + NORM_EPS) * npre_ref[...]
    else:
        ho_ref[...] = xn


def _outproj(g, w, x, npost, npre_next):
    m, d = x.shape
    has_next = npre_next is not None
    if not has_next:
        npre_next = npost
    row = pl.BlockSpec((ROW_TILE, d), lambda i: (i, 0))
    vec = pl.BlockSpec((1, d), lambda i: (0, 0))
    oshape = jax.ShapeDtypeStruct((m, d), F32)
    return pl.pallas_call(
        functools.partial(_outproj_kernel, has_next),
        grid=(m // ROW_TILE,),
        in_specs=[row, pl.BlockSpec((d, d), lambda i: (0, 0)), row, vec, vec],
        out_specs=[row, row],
        out_shape=[oshape, oshape],
        compiler_params=pltpu.CompilerParams(
            dimension_semantics=("arbitrary",), vmem_limit_bytes=VMEM_LIMIT),
        name="outproj",
    )(g, w, x, npost.reshape(1, d), npre_next.reshape(1, d))


def _rwkv_group(chunk, tvalid, has_vres, n_double, sl, grp, refs, wm, am, vm, ones_blk, diag):
    (r_ref, k_ref, v_ref, z_ref, vf_ref, w2_ref, a2_ref, v2_ref, prm_ref, g_ref, sbd_ref) = refs
    hw = RWKV_HEAD
    nh = HEADS_PER_GROUP
    prm = prm_ref[:, sl]
    w0, a0, v0 = prm[0:1], prm[1:2], prm[2:3]
    kk_w, ka_w, rk_w, ln_w, ln_b = prm[3:4], prm[4:5], prm[5:6], prm[6:7], prm[7:8]
    r = r_ref[:, sl]
    k = k_ref[:, sl]
    v = v_ref[:, sl]

    wl = w0 + _mm(wm, w2_ref[:, sl])
    al = a0 + _mm(am, a2_ref[:, sl])
    if has_vres:
        vl = v0 + _mm(vm, v2_ref[:, sl])
    yield
    logw = -jnp.exp(-_softplus(-wl) - 0.5)
    if tvalid < chunk:
        logw = jnp.where(_iota(logw.shape, 0) < tvalid, logw, 0.0)
    alpha = _sigmoid(al)
    if has_vres:
        v = v + (vf_ref[:, sl] - v) * _sigmoid(vl)
    kk = k * kk_w
    k2 = k * (1.0 + (alpha - 1.0) * ka_w)
    kk_n2, bonus_s = _segsum_many([kk * kk, r * k2 * rk_w], ones_blk)
    cum = _cumsum_rows(logw)
    yield
    kk = kk / jnp.maximum(jnp.sqrt(kk_n2), 1e-12)
    b = kk * alpha
    a = -kk
    e_neg = jnp.exp(-cum)
    a_hat = a * jnp.exp(cum - logw)
    b_hat = b * e_neg
    k_hat = k2 * e_neg
    r_hat = r * jnp.exp(cum)
    cum_last = cum[chunk - 1:chunk, :]
    e_tail = jnp.exp(cum_last - cum)
    b_tail = b * e_tail
    k_tail = k2 * e_tail

    lhs = jnp.concatenate([a_hat, r_hat], axis=0).astype(BF16)
    gram_b = _mm_nt(lhs, _block_diag(b_hat, hw, nh))
    gram_k = _mm_nt(lhs, _block_diag(k_hat, hw, nh))
    s_bd = sbd_ref[grp]
    xy_state = _mm_nt(lhs, s_bd.astype(BF16))
    yield
    shape_cc = (chunk, nh * chunk)
    t_idx = _iota(shape_cc, 0)
    i_idx = jnp.bitwise_and(_iota(shape_cc, 1), chunk - 1)
    strict = i_idx < t_idx
    incl = i_idx <= t_idx
    n_ab = jnp.where(strict, gram_b[:chunk], 0.0)
    a_ak = jnp.where(strict, gram_k[:chunk], 0.0)
    a_rb = jnp.where(incl, gram_b[chunk:], 0.0)
    a_rk = jnp.where(incl, gram_k[chunk:], 0.0)
    v_bd = _block_diag(v, hw, nh)

    xy = xy_state + _mm(jnp.concatenate([a_ak, a_rk], axis=0).astype(BF16), v_bd)
    x = xy[:chunk]
    y_v = xy[chunk:]
    p = n_ab
    if n_double > 1:
        p_next = _mm(p.astype(BF16), _block_diag(p, chunk, nh))
    yield
    for j in range(n_double):
        x = x + _mm(p.astype(BF16), _block_diag(x, hw, nh))
        if j + 1 < n_double:
            p = p_next
            if j + 2 < n_double:
                p_next = _mm(p.astype(BF16), _block_diag(p, chunk, nh))
        yield
    u = x

    y = y_v + _mm(a_rb.astype(BF16), _block_diag(u, hw, nh))
    uv = jnp.concatenate([u, v], axis=0).astype(BF16)
    bk = jnp.concatenate([b_tail, k_tail], axis=0).astype(BF16)
    sbd_ref[grp] = s_bd * jnp.exp(cum_last) + jnp.where(diag, _mm_tn(uv, bk), 0.0)
    yield

    inv_n = 1.0 / RWKV_HEAD
    sum_y, sum_yy = _segsum_many([y, y * y], ones_blk)
    yield
    mean = sum_y * inv_n
    var = sum_yy * inv_n - mean * mean
    yn = (y - mean) * lax.rsqrt(var + RWKV_GN_EPS) * ln_w + ln_b
    z = z_ref[:, sl]
    g_ref[:, sl] = ((yn + bonus_s * v) * (z * _sigmoid(z))).astype(BF16)


def _rwkv_chunk_kernel(chunk, tvalid, has_vres, n_double, n_groups, fill,
                       r_ref, k_ref, v_ref, z_ref, wm_ref, am_ref, vm_ref, vf_ref,
                       w2_ref, a2_ref, v2_ref, prm_ref, s0_ref, ones_ref, *rest):
    g_ref, so_ref, sbd_ref = rest[-3:]
    c = pl.program_id(1)
    n_chunks = pl.num_programs(1)
    hw = RWKV_HEAD
    diag = (lax.shift_right_logical(_iota((GROUP, GROUP), 0), 6)
            == lax.shift_right_logical(_iota((GROUP, GROUP), 1), 6))

    @pl.when(c == 0)
    def _init():
        for g in range(n_groups):
            s4 = s0_ref[g * HEADS_PER_GROUP:(g + 1) * HEADS_PER_GROUP].reshape(GROUP, hw)
            tiled = jnp.concatenate([s4] * HEADS_PER_GROUP, axis=1)
            sbd_ref[g] = jnp.where(diag, tiled, 0.0)

    ones_blk = ones_ref[...]
    wm = wm_ref[...].astype(BF16)
    am = am_ref[...].astype(BF16)
    vm = vm_ref[...].astype(BF16)
    refs = (r_ref, k_ref, v_ref, z_ref, vf_ref, w2_ref, a2_ref, v2_ref, prm_ref, g_ref, sbd_ref)
    _round_robin([
        _rwkv_group(chunk, tvalid, has_vres, n_double, slice(g * GROUP, (g + 1) * GROUP), g, refs,
                    wm, am, vm, ones_blk, diag)
        for g in range(n_groups)])

    @pl.when(c == n_chunks - 1)
    def _fin():
        for g in range(n_groups):
            s_new = sbd_ref[g]
            dense = (s_new[:, 0:hw] + s_new[:, hw:2 * hw]
                     + s_new[:, 2 * hw:3 * hw] + s_new[:, 3 * hw:4 * hw])
            _store_state(so_ref, fill, slice(g * HEADS_PER_GROUP, (g + 1) * HEADS_PER_GROUP),
                         dense.reshape(HEADS_PER_GROUP, hw, hw))


def _rwkv_chunk(proj, vf_src, wm, am, vm, w2p, a2p, v2p, prm, s0, s_out_prev, *, layer_j, n_seq,
                n_chunks, chunk, tvalid, has_vres):
    m = n_seq * n_chunks * chunk
    d = proj.shape[1] // 4
    n_groups = d // GROUP
    n_double = max(1, math.ceil(math.log2(tvalid)))

    def col(off):
        return pl.BlockSpec((chunk, d), lambda s, c: (s * n_chunks + c, off))

    lora = pl.BlockSpec((chunk, LORA_PAD), lambda s, c: (s * n_chunks + c, 0))
    up = pl.BlockSpec((LORA_PAD, d), lambda s, c: (0, 0))
    state = pl.BlockSpec((None, None) + s0.shape[2:], lambda s, c: (layer_j, s, 0, 0, 0))
    prev = [] if s_out_prev is None else [s_out_prev]
    fill = None if prev else (layer_j, s0.shape[0])
    state_out = state if prev else pl.BlockSpec(
        (s0.shape[0], None) + s0.shape[2:], lambda s, c: (0, s, 0, 0, 0))
    ones_blk = jnp.where(
        lax.shift_right_logical(_iota((GROUP, GROUP), 0), 6)
        == lax.shift_right_logical(_iota((GROUP, GROUP), 1), 6), 1.0, 0.0).astype(BF16)
    kernel = functools.partial(_rwkv_chunk_kernel, chunk, tvalid, has_vres, n_double, n_groups, fill)
    return pl.pallas_call(
        kernel,
        grid=(n_seq, n_chunks),
        in_specs=[col(0), col(1), col(2), col(3), lora, lora, lora, col(2), up, up, up,
                  pl.BlockSpec((8, d), lambda s, c: (0, 0)), state,
                  pl.BlockSpec((GROUP, GROUP), lambda s, c: (0, 0))]
        + [pl.BlockSpec(memory_space=pl.ANY)] * len(prev),
        out_specs=[pl.BlockSpec((chunk, d), lambda s, c: (s * n_chunks + c, 0)), state_out],
        out_shape=[jax.ShapeDtypeStruct((m, d), BF16),
                   jax.ShapeDtypeStruct(s0.shape, F32)],
        scratch_shapes=[pltpu.VMEM((n_groups, GROUP, GROUP), F32)],
        input_output_aliases={14: 1} if prev else {},
        compiler_params=pltpu.CompilerParams(
            dimension_semantics=("arbitrary", "arbitrary"), vmem_limit_bytes=VMEM_LIMIT),
        name="rwkv_chunk",
    )(proj, proj, proj, proj, wm, am, vm, vf_src, w2p, a2p, v2p, prm, s0, ones_blk, *prev)


def _hgrn_head(chunk, tvalid, sl, head, refs, lb, nw, ones_blk):
    (q_ref, f_ref, i_ref, z_ref, g_ref, st_ref) = refs
    sub = min(HGRN_SUB, chunk)
    shift = int(math.log2(sub))
    fl = f_ref[:, sl]
    qraw = q_ref[:, sl]
    val = i_ref[:, sl]
    f = lb + (1.0 - lb) * _sigmoid(fl)
    logf = jnp.log(jnp.maximum(f, GATE_FLOOR))
    if tvalid < chunk:
        logf = jnp.where(_iota(logf.shape, 0) < tvalid, logf, 0.0)
    kg = (1.0 - lb) * _sigmoid(-fl)
    qs = qraw * _sigmoid(qraw) * (HGRN_DK ** -0.5)
    gcum = _cumsum_rows(logf)
    yield

    st = st_ref[head]
    o_inter = _mm_nt((qs * jnp.exp(gcum)).astype(BF16), st.astype(BF16))
    g_last = gcum[chunk - 1:chunk, :]
    k_tail = kg * jnp.exp(g_last - gcum)
    st_ref[head] = st * jnp.exp(g_last) + _mm_tn(val.astype(BF16), k_tail.astype(BF16))

    def bcast_rows(x):
        return jnp.concatenate(
            [jnp.broadcast_to(x[t:t + 1, :], (sub, HGRN_DK)) for t in range(tvalid)], axis=0)

    def tile_blocks(x):
        return jnp.concatenate(
            [x[(t >> shift) * sub:((t >> shift) + 1) * sub] for t in range(tvalid)], axis=0)

    rows = tvalid * sub
    prod = bcast_rows(qs) * tile_blocks(kg) * jnp.exp(
        jnp.minimum(bcast_rows(gcum) - tile_blocks(gcum), 0.0))
    att_diag = _mm(prod.astype(BF16), ones_blk)
    n_blk = -(-tvalid // sub)
    att_off = []
    for bi in range(1, n_blk):
        lo = bi * sub
        g_edge = gcum[lo - 1:lo]
        q_n = (qs[lo:lo + sub] * jnp.exp(gcum[lo:lo + sub] - g_edge)).astype(BF16)
        k_n = (kg[0:lo] * jnp.exp(g_edge - gcum[0:lo])).astype(BF16)
        att_off.append(_mm_nt(q_n, k_n))
    yield

    ridx = _iota((rows, HGRN_DK), 0)
    causal = (jnp.bitwise_and(ridx, sub - 1)
              <= jnp.bitwise_and(lax.shift_right_logical(ridx, shift), sub - 1))
    weighted = jnp.where(causal, att_diag * tile_blocks(val), 0.0)
    sel = jnp.where(lax.shift_right_logical(_iota((chunk, rows), 1), shift) == _iota((chunk, rows), 0),
                    1.0, 0.0).astype(BF16)
    o = o_inter + _mm(sel, weighted.astype(BF16))
    if n_blk > 1:
        parts = [jnp.zeros((sub, HGRN_DK), F32)]
        for bi in range(1, n_blk):
            parts.append(_mm(att_off[bi - 1].astype(BF16), val[0:bi * sub].astype(BF16)))
        if n_blk * sub < chunk:
            parts.append(jnp.zeros((chunk - n_blk * sub, HGRN_DK), F32))
        o = o + jnp.concatenate(parts, axis=0)
    yield

    o = o * lax.rsqrt(jnp.mean(o * o, axis=-1, keepdims=True) + NORM_EPS) * nw
    z = z_ref[:, sl]
    g_ref[:, sl] = (o * (z * _sigmoid(z))).astype(BF16)


def _hgrn_chunk_kernel(chunk, tvalid, layer_j, heads, fill,
                       q_ref, f_ref, i_ref, z_ref, lbl_ref, nw_ref, s0_ref, *rest):
    g_ref, so_ref, st_ref = rest[-3:]
    c = pl.program_id(2)
    n_chunks = pl.num_programs(2)

    @pl.when(c == 0)
    def _init():
        for hh in range(heads):
            st_ref[hh] = s0_ref[hh].T

    logits = lbl_ref[...]
    ex = jnp.exp(logits - jnp.max(logits, axis=0, keepdims=True))
    soft = ex / jnp.sum(ex, axis=0, keepdims=True)
    lb_all = jnp.sum(soft[0:layer_j + 1], axis=0, keepdims=True) - soft[0:1]

    ones_blk = jnp.ones((HGRN_DK, HGRN_DK), BF16)
    nw = nw_ref[...]
    refs = (q_ref, f_ref, i_ref, z_ref, g_ref, st_ref)
    _round_robin([
        _hgrn_head(chunk, tvalid, slice(hh * HGRN_DK, (hh + 1) * HGRN_DK), hh, refs,
                   lb_all[:, hh * HGRN_DK:(hh + 1) * HGRN_DK], nw, ones_blk)
        for hh in range(heads)])

    @pl.when(c == n_chunks - 1)
    def _fin():
        for hh in range(heads):
            _store_state(so_ref, fill, hh, st_ref[hh].T)


def _hgrn_chunk(proj, lb_logits, norm_w, s0, s_out_prev, *, layer_j, n_seq, n_chunks, chunk, tvalid):
    m = n_seq * n_chunks * chunk
    d = proj.shape[1] // 4
    n_heads = d // HGRN_DK
    heads = HGRN_HEADS_PER_STEP
    n_hb = n_heads // heads
    width = heads * HGRN_DK

    def col(off):
        return pl.BlockSpec((chunk, width), lambda s, h, c: (s * n_chunks + c, off * n_hb + h))

    state = pl.BlockSpec((None, None, heads, HGRN_DK, HGRN_DK), lambda s, h, c: (layer_j, s, h, 0, 0))
    prev = [] if s_out_prev is None else [s_out_prev]
    fill = None if prev else (layer_j, s0.shape[0])
    state_out = state if prev else pl.BlockSpec(
        (s0.shape[0], None, heads, HGRN_DK, HGRN_DK), lambda s, h, c: (0, s, h, 0, 0))
    kernel = functools.partial(_hgrn_chunk_kernel, chunk, tvalid, layer_j, heads, fill)
    return pl.pallas_call(
        kernel,
        grid=(n_seq, n_hb, n_chunks),
        in_specs=[col(0), col(1), col(2), col(3),
                  pl.BlockSpec((lb_logits.shape[0], width), lambda s, h, c: (0, h)),
                  pl.BlockSpec((1, HGRN_DK), lambda s, h, c: (0, 0)), state]
        + [pl.BlockSpec(memory_space=pl.ANY)] * len(prev),
        out_specs=[pl.BlockSpec((chunk, width), lambda s, h, c: (s * n_chunks + c, h)), state_out],
        out_shape=[jax.ShapeDtypeStruct((m, d), BF16),
                   jax.ShapeDtypeStruct(s0.shape, F32)],
        scratch_shapes=[pltpu.VMEM((heads, HGRN_DK, HGRN_DK), F32)],
        input_output_aliases={7: 1} if prev else {},
        compiler_params=pltpu.CompilerParams(
            dimension_semantics=("arbitrary", "arbitrary", "arbitrary"),
            vmem_limit_bytes=VMEM_LIMIT),
        name="hgrn_chunk",
    )(proj, proj, proj, proj, lb_logits, norm_w.reshape(1, HGRN_DK), s0, *prev)


def _pad_lanes(w, axis):
    pad = [(0, 0)] * w.ndim
    pad[axis] = (0, LORA_PAD - w.shape[axis])
    return jnp.pad(w, pad)


def _trunk(x, shift0, wkv0, hgrn0, p, *, n_seq, tpad, tvalid, chunk):
    d = x.shape[1]
    n_chunks = tpad // chunk
    shifts = []
    wkv_out, hgrn_out = None, None
    vf_src = None
    h = _rmsnorm(x, p["norm_pre"][0])
    for layer in range(DEPTH):
        j = layer // 2
        npre_next = p["norm_pre"][layer + 1] if layer + 1 < DEPTH else None
        if layer % 2 == 0:
            h3 = h.reshape(n_seq, tpad, d)
            if tvalid == tpad and tpad % ROW_TILE == 0:
                tiles_per_seq = tpad // ROW_TILE
                hprev = shift0[j][:, None, :]
            else:
                tiles_per_seq = 0
                hprev = jnp.concatenate([shift0[j][:, None, :], h3[:, :tvalid - 1]], axis=1)
                hprev = jnp.pad(hprev, ((0, 0), (0, tpad - tvalid), (0, 0)))
                hprev = hprev.reshape(n_seq * tpad, d)
            shifts.append(h3[:, tvalid - 1])
            has_vres = j > 0
            proj = _proj_rwkv(h, hprev, p["mu4"][j], p["w_in"][j], tiles_per_seq)
            wm, am, vm = _lora(h, hprev, p["mu6"][j], p["w1"][j], p["a1"][j],
                               p["v1"][max(j - 1, 0)], has_vres, tiles_per_seq)
            if vf_src is None:
                vf_src = proj
            g, wkv_out = _rwkv_chunk(
                proj, vf_src, wm, am, vm, p["w2"][j], p["a2"][j], p["v2"][max(j - 1, 0)],
                p["rwkv_prm"][j], wkv0, wkv_out, layer_j=j, n_seq=n_seq, n_chunks=n_chunks,
                chunk=chunk, tvalid=min(tvalid, chunk), has_vres=has_vres)
            w_o = p["rwkv_w_o"][j]
        else:
            proj = _proj_hgrn(h, p["hgrn_w_in"][j])
            g, hgrn_out = _hgrn_chunk(
                proj, p["hgrn_lb_logits"], p["hgrn_norm_w"][j], hgrn0, hgrn_out, layer_j=j,
                n_seq=n_seq, n_chunks=n_chunks, chunk=chunk, tvalid=min(tvalid, chunk))
            w_o = p["hgrn_w_o"][j]
        x, h = _outproj(g, w_o, x, p["norm_post"][layer], npre_next)
    return x, jnp.stack(shifts), wkv_out, hgrn_out


def kernel(x_prompt, x_sample, state_rwkv_shift, state_rwkv_wkv, state_hgrn, norm_pre, norm_post,
           rwkv_mu, rwkv_w_in, rwkv_w0, rwkv_w1, rwkv_w2, rwkv_a0, rwkv_a1, rwkv_a2, rwkv_v0,
           rwkv_v1, rwkv_v2, rwkv_k_k, rwkv_k_a, rwkv_r_k, rwkv_ln_w, rwkv_ln_b, rwkv_w_o,
           hgrn_w_in, hgrn_lb_logits, hgrn_norm_w, hgrn_w_o):
    n_rwkv = rwkv_mu.shape[0]
    d = x_prompt.shape[-1]
    v0_full = jnp.concatenate([jnp.zeros((1, d), F32), rwkv_v0], axis=0)
    rwkv_prm = jnp.stack([rwkv_w0, rwkv_a0, v0_full, rwkv_k_k, rwkv_k_a,
                          rwkv_r_k.reshape(n_rwkv, d), rwkv_ln_w, rwkv_ln_b], axis=1)
    p = {
        "norm_pre": norm_pre, "norm_post": norm_post,
        "mu4": rwkv_mu[:, :4, None, :], "mu6": rwkv_mu,
        "w_in": rwkv_w_in.astype(BF16),
        "w1": _pad_lanes(rwkv_w1, 2).astype(BF16), "a1": _pad_lanes(rwkv_a1, 2).astype(BF16),
        "v1": _pad_lanes(rwkv_v1, 2).astype(BF16),
        "w2": _pad_lanes(rwkv_w2, 1).astype(BF16), "a2": _pad_lanes(rwkv_a2, 1).astype(BF16),
        "v2": _pad_lanes(rwkv_v2, 1).astype(BF16),
        "rwkv_prm": rwkv_prm, "rwkv_w_o": rwkv_w_o.astype(BF16),
        "hgrn_w_in": hgrn_w_in.astype(BF16), "hgrn_lb_logits": hgrn_lb_logits,
        "hgrn_norm_w": hgrn_norm_w, "hgrn_w_o": hgrn_w_o.astype(BF16),
    }

    bp, tp, _ = x_prompt.shape
    zero_shift = jnp.zeros((n_rwkv, bp, d), F32)
    zero_wkv = jnp.zeros((n_rwkv, bp) + state_rwkv_wkv.shape[2:], F32)
    zero_hgrn = jnp.zeros((state_hgrn.shape[0], bp) + state_hgrn.shape[2:], F32)
    y_p, p_shift, p_wkv, p_hgrn = _trunk(
        x_prompt.reshape(bp * tp, d), zero_shift, zero_wkv, zero_hgrn, p,
        n_seq=bp, tpad=tp, tvalid=tp, chunk=PROMPT_CHUNK)

    bs, ts, _ = x_sample.shape
    xs = jnp.pad(x_sample, ((0, 0), (0, SAMPLE_TPAD - ts), (0, 0))).reshape(bs * SAMPLE_TPAD, d)
    y_s, s_shift, s_wkv, s_hgrn = _trunk(
        xs, state_rwkv_shift, state_rwkv_wkv, state_hgrn, p,
        n_seq=bs, tpad=SAMPLE_TPAD, tvalid=ts, chunk=SAMPLE_TPAD)

    return (y_p.reshape(bp, tp, d), y_s.reshape(bs, SAMPLE_TPAD, d)[:, :ts],
            p_shift, p_wkv, p_hgrn, s_shift, s_wkv, s_hgrn)
```

```python
import functools
import math

import jax
import jax.numpy as jnp
from jax import lax
from jax.experimental import pallas as pl
from jax.experimental.pallas import tpu as pltpu

F32 = jnp.float32
BF16 = jnp.bfloat16

D_MODEL = 2048
DEPTH = 4
RWKV_HEAD = 64
HGRN_DK = 128
RWKV_GN_EPS = 1e-5 * RWKV_HEAD
NORM_EPS = 1e-6
GATE_FLOOR = 1e-30

LANES = 128
GROUP = 256
HEADS_PER_GROUP = GROUP // RWKV_HEAD
LORA_PAD = 128
PROMPT_CHUNK = 64
SAMPLE_TPAD = 16
HGRN_SUB = 8
HGRN_HEADS_PER_STEP = 16
ROW_TILE = 512
VMEM_LIMIT = 48 * 1024 * 1024


def _mm(a, b):
    return jnp.dot(a, b, preferred_element_type=F32)


def _mm_nt(a, b):
    return lax.dot_general(a, b, (((1,), (1,)), ((), ())), preferred_element_type=F32)


def _mm_tn(a, b):
    return lax.dot_general(a, b, (((0,), (0,)), ((), ())), preferred_element_type=F32)


def _iota(shape, dim):
    return lax.broadcasted_iota(jnp.int32, shape, dim)


def _split2(x):
    hi = x.astype(BF16)
    lo = (x - hi.astype(F32)).astype(BF16)
    return hi, lo


def _segsum(x, ones_blk):
    hi, lo = _split2(x)
    return _mm(hi, ones_blk) + _mm(lo, ones_blk)


def _segsum_many(xs, ones_blk):
    pieces = []
    for x in xs:
        pieces.extend(_split2(x))
    out = _mm(jnp.concatenate(pieces, axis=0), ones_blk)
    n = xs[0].shape[0]
    return [out[2 * i * n:(2 * i + 1) * n] + out[(2 * i + 1) * n:(2 * i + 2) * n] for i in range(len(xs))]


def _cumsum_rows(x):
    n = x.shape[0]
    tri = jnp.where(_iota((n, n), 0) >= _iota((n, n), 1), 1.0, 0.0).astype(BF16)
    x1 = x.astype(BF16)
    r1 = x - x1.astype(F32)
    x2 = r1.astype(BF16)
    x3 = (r1 - x2.astype(F32)).astype(BF16)
    return _mm(tri, x1) + _mm(tri, x2) + _mm(tri, x3)


def _block_diag(x, head_width, n_heads):
    shift = int(math.log2(head_width))
    lane_head = lax.shift_right_logical(_iota(x.shape, 1), shift)
    parts = [jnp.where(lane_head == h, x, 0.0).astype(BF16) for h in range(n_heads)]
    return jnp.concatenate(parts, axis=0)


def _sigmoid(x):
    return jax.nn.sigmoid(x)


def _softplus(x):
    return jnp.maximum(x, 0.0) + jnp.log(1.0 + jnp.exp(-jnp.abs(x)))


def _store_state(so_ref, fill, idx, value):
    if fill is None:
        so_ref[idx] = value
        return
    layer, n_layers = fill
    for other in range(n_layers):
        so_ref[(other,) + idx] = value if other == layer else jnp.zeros_like(value)


def _round_robin(gens):
    results = [None] * len(gens)
    active = list(range(len(gens)))
    while active:
        for idx in list(active):
            try:
                next(gens[idx])
            except StopIteration as stop:
                results[idx] = stop.value
                active.remove(idx)
    return results


def _rmsnorm_kernel(x_ref, w_ref, o_ref):
    x = x_ref[...]
    ms = jnp.mean(x * x, axis=-1, keepdims=True)
    o_ref[...] = x * lax.rsqrt(ms + NORM_EPS) * w_ref[...]


def _rmsnorm(x, w):
    m, d = x.shape
    return pl.pallas_call(
        _rmsnorm_kernel,
        grid=(m // ROW_TILE,),
        in_specs=[pl.BlockSpec((ROW_TILE, d), lambda i: (i, 0)),
                  pl.BlockSpec((1, d), lambda i: (0, 0))],
        out_specs=pl.BlockSpec((ROW_TILE, d), lambda i: (i, 0)),
        out_shape=jax.ShapeDtypeStruct((m, d), F32),
        name="rmsnorm",
    )(x, w.reshape(1, d))


def _prev_rows(h, first_ref, carry_ref, tile, tiles_per_seq):
    rolled = pltpu.roll(h, 1, axis=0)
    seq_start = lax.rem(tile, tiles_per_seq) == 0
    edge = jnp.where(seq_start, first_ref[...], carry_ref[7:8, :])
    carry_ref[...] = h[h.shape[0] - 8:, :]
    return jnp.where(_iota(h.shape, 0) == 0, edge, rolled)


def _proj_mix_kernel(tiles_per_seq, h_ref, hp_ref, mu_ref, w_ref, o_ref, *carry):
    h = h_ref[...]
    if tiles_per_seq:
        hprev = _prev_rows(h, hp_ref, carry[0], pl.program_id(1), tiles_per_seq)
    else:
        hprev = hp_ref[...]
    xs = h + (hprev - h) * mu_ref[...]
    o_ref[...] = _mm(xs.astype(BF16), w_ref[...])


def _proj_plain_kernel(h_ref, w_ref, o_ref):
    o_ref[...] = _mm(h_ref[...].astype(BF16), w_ref[...])


def _prev_spec(hprev, tiles_per_seq, d, grid_rank):
    if tiles_per_seq:
        if grid_rank == 2:
            return pl.BlockSpec((None, 1, d), lambda n, i: (i // tiles_per_seq, 0, 0))
        return pl.BlockSpec((None, 1, d), lambda i: (i // tiles_per_seq, 0, 0))
    if grid_rank == 2:
        return pl.BlockSpec((ROW_TILE, d), lambda n, i: (i, 0))
    return pl.BlockSpec((ROW_TILE, d), lambda i: (i, 0))


def _proj_rwkv(h, hprev, mu4, w4, tiles_per_seq):
    m, d = h.shape
    n_proj = w4.shape[0]
    return pl.pallas_call(
        functools.partial(_proj_mix_kernel, tiles_per_seq),
        grid=(n_proj, m // ROW_TILE),
        in_specs=[pl.BlockSpec((ROW_TILE, d), lambda n, i: (i, 0)),
                  _prev_spec(hprev, tiles_per_seq, d, 2),
                  pl.BlockSpec((None, 1, d), lambda n, i: (n, 0, 0)),
                  pl.BlockSpec((None, d, d), lambda n, i: (n, 0, 0))],
        out_specs=pl.BlockSpec((ROW_TILE, d), lambda n, i: (i, n)),
        out_shape=jax.ShapeDtypeStruct((m, n_proj * d), F32),
        scratch_shapes=[pltpu.VMEM((8, d), F32)] if tiles_per_seq else [],
        compiler_params=pltpu.CompilerParams(
            dimension_semantics=("arbitrary", "arbitrary"), vmem_limit_bytes=VMEM_LIMIT),
        name="proj_rwkv",
    )(h, hprev, mu4, w4)


def _proj_hgrn(h, w):
    m, d = h.shape
    n_proj = w.shape[1] // d
    return pl.pallas_call(
        _proj_plain_kernel,
        grid=(n_proj, m // ROW_TILE),
        in_specs=[pl.BlockSpec((ROW_TILE, d), lambda n, i: (i, 0)),
                  pl.BlockSpec((d, d), lambda n, i: (0, n))],
        out_specs=pl.BlockSpec((ROW_TILE, d), lambda n, i: (i, n)),
        out_shape=jax.ShapeDtypeStruct((m, n_proj * d), F32),
        compiler_params=pltpu.CompilerParams(
            dimension_semantics=("arbitrary", "arbitrary"), vmem_limit_bytes=VMEM_LIMIT),
        name="proj_hgrn",
    )(h, w)


def _lora_kernel(has_v, tiles_per_seq, h_ref, hp_ref, mu_ref, w1_ref, a1_ref, v1_ref,
                 wm_ref, am_ref, vm_ref, *carry):
    h = h_ref[...]
    if tiles_per_seq:
        hprev = _prev_rows(h, hp_ref, carry[0], pl.program_id(0), tiles_per_seq)
    else:
        hprev = hp_ref[...]
    delta = hprev - h
    xw = h + delta * mu_ref[4:5, :]
    xa = h + delta * mu_ref[5:6, :]
    wm_ref[...] = jnp.tanh(_mm(xw.astype(BF16), w1_ref[...]))
    am_ref[...] = _mm(xa.astype(BF16), a1_ref[...])
    if has_v:
        xv = h + delta * mu_ref[2:3, :]
        vm_ref[...] = _mm(xv.astype(BF16), v1_ref[...])
    else:
        vm_ref[...] = jnp.zeros(vm_ref.shape, F32)


def _lora(h, hprev, mu6, w1p, a1p, v1p, has_v, tiles_per_seq):
    m, d = h.shape
    row = pl.BlockSpec((ROW_TILE, d), lambda i: (i, 0))
    wspec = pl.BlockSpec((d, LORA_PAD), lambda i: (0, 0))
    ospec = pl.BlockSpec((ROW_TILE, LORA_PAD), lambda i: (i, 0))
    oshape = jax.ShapeDtypeStruct((m, LORA_PAD), F32)
    return pl.pallas_call(
        functools.partial(_lora_kernel, has_v, tiles_per_seq),
        grid=(m // ROW_TILE,),
        in_specs=[row, _prev_spec(hprev, tiles_per_seq, d, 1),
                  pl.BlockSpec(mu6.shape, lambda i: (0, 0)), wspec, wspec, wspec],
        out_specs=[ospec, ospec, ospec],
        out_shape=[oshape, oshape, oshape],
        scratch_shapes=[pltpu.VMEM((8, d), F32)] if tiles_per_seq else [],
        compiler_params=pltpu.CompilerParams(dimension_semantics=("arbitrary",)),
        name="lora",
    )(h, hprev, mu6, w1p, a1p, v1p)


def _outproj_kernel(has_next, g_ref, w_ref, x_ref, npost_ref, npre_ref, xo_ref, ho_ref):
    out = _mm(g_ref[...], w_ref[...])
    ms = jnp.mean(out * out, axis=-1, keepdims=True)
    xn = x_ref[...] + out * lax.rsqrt(ms + NORM_EPS) * npost_ref[...]
    xo_ref[...] = xn
    if has_next:
        ms2 = jnp.mean(xn * xn, axis=-1, keepdims=True)
        ho_ref[...] = xn * lax.rsqrt(ms2 + NORM_EPS) * npre_ref[...]
    else:
        ho_ref[...] = xn


def _outproj(g, w, x, npost, npre_next):
    m, d = x.shape
    has_next = npre_next is not None
    if not has_next:
        npre_next = npost
    row = pl.BlockSpec((ROW_TILE, d), lambda i: (i, 0))
    vec = pl.BlockSpec((1, d), lambda i: (0, 0))
    oshape = jax.ShapeDtypeStruct((m, d), F32)
    return pl.pallas_call(
        functools.partial(_outproj_kernel, has_next),
        grid=(m // ROW_TILE,),
        in_specs=[row, pl.BlockSpec((d, d), lambda i: (0, 0)), row, vec, vec],
        out_specs=[row, row],
        out_shape=[oshape, oshape],
        compiler_params=pltpu.CompilerParams(
            dimension_semantics=("arbitrary",), vmem_limit_bytes=VMEM_LIMIT),
        name="outproj",
    )(g, w, x, npost.reshape(1, d), npre_next.reshape(1, d))


def _rwkv_group(chunk, tvalid, has_vres, n_double, sl, grp, refs, wm, am, vm, ones_blk, diag):
    (r_ref, k_ref, v_ref, z_ref, vf_ref, w2_ref, a2_ref, v2_ref, prm_ref, g_ref, sbd_ref) = refs
    hw = RWKV_HEAD
    nh = HEADS_PER_GROUP
    prm = prm_ref[:, sl]
    w0, a0, v0 = prm[0:1], prm[1:2], prm[2:3]
    kk_w, ka_w, rk_w, ln_w, ln_b = prm[3:4], prm[4:5], prm[5:6], prm[6:7], prm[7:8]
    r = r_ref[:, sl]
    k = k_ref[:, sl]
    v = v_ref[:, sl]

    wl = w0 + _mm(wm, w2_ref[:, sl])
    al = a0 + _mm(am, a2_ref[:, sl])
    if has_vres:
        vl = v0 + _mm(vm, v2_ref[:, sl])
    yield
    logw = -jnp.exp(-_softplus(-wl) - 0.5)
    if tvalid < chunk:
        logw = jnp.where(_iota(logw.shape, 0) < tvalid, logw, 0.0)
    alpha = _sigmoid(al)
    if has_vres:
        v = v + (vf_ref[:, sl] - v) * _sigmoid(vl)
    kk = k * kk_w
    k2 = k * (1.0 + (alpha - 1.0) * ka_w)
    kk_n2, bonus_s = _segsum_many([kk * kk, r * k2 * rk_w], ones_blk)
    cum = _cumsum_rows(logw)
    yield
    kk = kk / jnp.maximum(jnp.sqrt(kk_n2), 1e-12)
    b = kk * alpha
    a = -kk
    e_neg = jnp.exp(-cum)
    a_hat = a * jnp.exp(cum - logw)
    b_hat = b * e_neg
    k_hat = k2 * e_neg
    r_hat = r * jnp.exp(cum)
    cum_last = cum[chunk - 1:chunk, :]
    e_tail = jnp.exp(cum_last - cum)
    b_tail = b * e_tail
    k_tail = k2 * e_tail

    lhs = jnp.concatenate([a_hat, r_hat], axis=0).astype(BF16)
    gram_b = _mm_nt(lhs, _block_diag(b_hat, hw, nh))
    gram_k = _mm_nt(lhs, _block_diag(k_hat, hw, nh))
    s_bd = sbd_ref[grp]
    xy_state = _mm_nt(lhs, s_bd.astype(BF16))
    yield
    shape_cc = (chunk, nh * chunk)
    t_idx = _iota(shape_cc, 0)
    i_idx = jnp.bitwise_and(_iota(shape_cc, 1), chunk - 1)
    strict = i_idx < t_idx
    incl = i_idx <= t_idx
    n_ab = jnp.where(strict, gram_b[:chunk], 0.0)
    a_ak = jnp.where(strict, gram_k[:chunk], 0.0)
    a_rb = jnp.where(incl, gram_b[chunk:], 0.0)
    a_rk = jnp.where(incl, gram_k[chunk:], 0.0)
    v_bd = _block_diag(v, hw, nh)

    xy = xy_state + _mm(jnp.concatenate([a_ak, a_rk], axis=0).astype(BF16), v_bd)
    x = xy[:chunk]
    y_v = xy[chunk:]
    p = n_ab
    if n_double > 1:
        p_next = _mm(p.astype(BF16), _block_diag(p, chunk, nh))
    yield
    for j in range(n_double):
        x = x + _mm(p.astype(BF16), _block_diag(x, hw, nh))
        if j + 1 < n_double:
            p = p_next
            if j + 2 < n_double:
                p_next = _mm(p.astype(BF16), _block_diag(p, chunk, nh))
        yield
    u = x

    y = y_v + _mm(a_rb.astype(BF16), _block_diag(u, hw, nh))
    uv = jnp.concatenate([u, v], axis=0).astype(BF16)
    bk = jnp.concatenate([b_tail, k_tail], axis=0).astype(BF16)
    sbd_ref[grp] = s_bd * jnp.exp(cum_last) + jnp.where(diag, _mm_tn(uv, bk), 0.0)
    yield

    inv_n = 1.0 / RWKV_HEAD
    sum_y, sum_yy = _segsum_many([y, y * y], ones_blk)
    yield
    mean = sum_y * inv_n
    var = sum_yy * inv_n - mean * mean
    yn = (y - mean) * lax.rsqrt(var + RWKV_GN_EPS) * ln_w + ln_b
    z = z_ref[:, sl]
    g_ref[:, sl] = ((yn + bonus_s * v) * (z * _sigmoid(z))).astype(BF16)


def _rwkv_chunk_kernel(chunk, tvalid, has_vres, n_double, n_groups, fill,
                       r_ref, k_ref, v_ref, z_ref, wm_ref, am_ref, vm_ref, vf_ref,
                       w2_ref, a2_ref, v2_ref, prm_ref, s0_ref, ones_ref, *rest):
    g_ref, so_ref, sbd_ref = rest[-3:]
    c = pl.program_id(1)
    n_chunks = pl.num_programs(1)
    hw = RWKV_HEAD
    diag = (lax.shift_right_logical(_iota((GROUP, GROUP), 0), 6)
            == lax.shift_right_logical(_iota((GROUP, GROUP), 1), 6))

    @pl.when(c == 0)
    def _init():
        for g in range(n_groups):
            s4 = s0_ref[g * HEADS_PER_GROUP:(g + 1) * HEADS_PER_GROUP].reshape(GROUP, hw)
            tiled = jnp.concatenate([s4] * HEADS_PER_GROUP, axis=1)
            sbd_ref[g] = jnp.where(diag, tiled, 0.0)

    ones_blk = ones_ref[...]
    wm = wm_ref[...].astype(BF16)
    am = am_ref[...].astype(BF16)
    vm = vm_ref[...].astype(BF16)
    refs = (r_ref, k_ref, v_ref, z_ref, vf_ref, w2_ref, a2_ref, v2_ref, prm_ref, g_ref, sbd_ref)
    _round_robin([
        _rwkv_group(chunk, tvalid, has_vres, n_double, slice(g * GROUP, (g + 1) * GROUP), g, refs,
                    wm, am, vm, ones_blk, diag)
        for g in range(n_groups)])

    @pl.when(c == n_chunks - 1)
    def _fin():
        for g in range(n_groups):
            s_new = sbd_ref[g]
            dense = (s_new[:, 0:hw] + s_new[:, hw:2 * hw]
                     + s_new[:, 2 * hw:3 * hw] + s_new[:, 3 * hw:4 * hw])
            _store_state(so_ref, fill, (slice(g * HEADS_PER_GROUP, (g + 1) * HEADS_PER_GROUP),),
                         dense.reshape(HEADS_PER_GROUP, hw, hw))


def _rwkv_chunk(proj, vf_src, wm, am, vm, w2p, a2p, v2p, prm, s0, s_out_prev, *, layer_j, n_seq,
                n_chunks, chunk, tvalid, has_vres):
    m = n_seq * n_chunks * chunk
    d = proj.shape[1] // 4
    n_groups = d // GROUP
    n_double = max(1, math.ceil(math.log2(tvalid)))

    def col(off):
        return pl.BlockSpec((chunk, d), lambda s, c: (s * n_chunks + c, off))

    lora = pl.BlockSpec((chunk, LORA_PAD), lambda s, c: (s * n_chunks + c, 0))
    up = pl.BlockSpec((LORA_PAD, d), lambda s, c: (0, 0))
    state = pl.BlockSpec((None, None) + s0.shape[2:], lambda s, c: (layer_j, s, 0, 0, 0))
    prev = [] if s_out_prev is None else [s_out_prev]
    fill = None if prev else (layer_j, s0.shape[0])
    state_out = state if prev else pl.BlockSpec(
        (s0.shape[0], None) + s0.shape[2:], lambda s, c: (0, s, 0, 0, 0))
    ones_blk = jnp.where(
        lax.shift_right_logical(_iota((GROUP, GROUP), 0), 6)
        == lax.shift_right_logical(_iota((GROUP, GROUP), 1), 6), 1.0, 0.0).astype(BF16)
    kernel = functools.partial(_rwkv_chunk_kernel, chunk, tvalid, has_vres, n_double, n_groups, fill)
    return pl.pallas_call(
        kernel,
        grid=(n_seq, n_chunks),
        in_specs=[col(0), col(1), col(2), col(3), lora, lora, lora, col(2), up, up, up,
                  pl.BlockSpec((8, d), lambda s, c: (0, 0)), state,
                  pl.BlockSpec((GROUP, GROUP), lambda s, c: (0, 0))]
        + [pl.BlockSpec(memory_space=pl.ANY)] * len(prev),
        out_specs=[pl.BlockSpec((chunk, d), lambda s, c: (s * n_chunks + c, 0)), state_out],
        out_shape=[jax.ShapeDtypeStruct((m, d), BF16),
                   jax.ShapeDtypeStruct(s0.shape, F32)],
        scratch_shapes=[pltpu.VMEM((n_groups, GROUP, GROUP), F32)],
        input_output_aliases={14: 1} if prev else {},
        compiler_params=pltpu.CompilerParams(
            dimension_semantics=("arbitrary", "arbitrary"), vmem_limit_bytes=VMEM_LIMIT),
        name="rwkv_chunk",
    )(proj, proj, proj, proj, wm, am, vm, vf_src, w2p, a2p, v2p, prm, s0, ones_blk, *prev)


def _rwkv_step_kernel(n_tok, has_vres, fill,
                      r_ref, k_ref, v_ref, z_ref, wm_ref, am_ref, vm_ref, vf_ref,
                      w2_ref, a2_ref, v2_ref, prm_ref, s0_ref, ones_ref, *rest):
    g_ref, so_ref, tr_ref, y_ref = rest[-4:]
    hw = RWKV_HEAD
    n_seq = s0_ref.shape[-1]
    ones_blk = ones_ref[...]
    prm = prm_ref[...]
    w0, a0, v0 = prm[0:1], prm[1:2], prm[2:3]
    kk_w, ka_w, rk_w, ln_w, ln_b = prm[3:4], prm[4:5], prm[5:6], prm[6:7], prm[7:8]
    r = r_ref[...]
    k = k_ref[...]
    v = v_ref[...]

    wl = w0 + _mm(wm_ref[...].astype(BF16), w2_ref[...])
    decay = jnp.exp(-jnp.exp(-_softplus(-wl) - 0.5))
    alpha = _sigmoid(a0 + _mm(am_ref[...].astype(BF16), a2_ref[...]))
    if has_vres:
        v = v + (vf_ref[...] - v) * _sigmoid(v0 + _mm(vm_ref[...].astype(BF16), v2_ref[...]))
    kk = k * kk_w
    k2 = k * (1.0 + (alpha - 1.0) * ka_w)
    kk_n2, bonus_s = _segsum_many([kk * kk, r * k2 * rk_w], ones_blk)
    kk = kk / jnp.maximum(jnp.sqrt(kk_n2), 1e-12)
    b = kk * alpha

    for idx, arr in enumerate((decay, k2, v, kk, b, r)):
        for t in range(n_tok):
            tr_ref[idx, t] = arr[t * n_seq:(t + 1) * n_seq, :].T

    def make_body(hh):
        lo = hh * hw

        def body(vi, carry):
            s = s0_ref[hh, vi]
            for t in range(n_tok):
                v_row = tr_ref[2, t, pl.ds(lo + vi, 1), :]
                sa = -jnp.sum(s * tr_ref[3, t, lo:lo + hw, :], axis=0, keepdims=True)
                s = (s * tr_ref[0, t, lo:lo + hw, :] + sa * tr_ref[4, t, lo:lo + hw, :]
                     + v_row * tr_ref[1, t, lo:lo + hw, :])
                y_ref[t, pl.ds(lo + vi, 1), :] = jnp.sum(
                    s * tr_ref[5, t, lo:lo + hw, :], axis=0, keepdims=True)
            _store_state(so_ref, fill, (hh, vi), s)
            return carry

        return body

    for hh in range(s0_ref.shape[0]):
        lax.fori_loop(0, hw, make_body(hh), 0, unroll=4)

    inv_n = 1.0 / hw
    for t in range(n_tok):
        rows = slice(t * n_seq, (t + 1) * n_seq)
        y = y_ref[t].T
        sum_y, sum_yy = _segsum_many([y, y * y], ones_blk)
        mean = sum_y * inv_n
        var = sum_yy * inv_n - mean * mean
        yn = (y - mean) * lax.rsqrt(var + RWKV_GN_EPS) * ln_w + ln_b
        z = z_ref[rows, :]
        g_ref[rows, :] = ((yn + bonus_s[rows] * v[rows]) * (z * _sigmoid(z))).astype(BF16)


def _rwkv_step(proj, vf_src, wm, am, vm, w2p, a2p, v2p, prm, s0, s_out_prev, *, layer_j, n_tok,
               has_vres):
    m = proj.shape[0]
    d = proj.shape[1] // 4
    pair = 2 * RWKV_HEAD
    n_pairs = d // pair
    n_seq = s0.shape[-1]

    def col(off):
        return pl.BlockSpec((m, pair), lambda hp: (0, off * n_pairs + hp))

    lora = pl.BlockSpec((m, LORA_PAD), lambda hp: (0, 0))
    up = pl.BlockSpec((LORA_PAD, pair), lambda hp: (0, hp))
    blk = (2, RWKV_HEAD, RWKV_HEAD, n_seq)
    state = pl.BlockSpec((None,) + blk, lambda hp: (layer_j, hp, 0, 0, 0))
    prev = [] if s_out_prev is None else [s_out_prev]
    fill = None if prev else (layer_j, s0.shape[0])
    state_out = state if prev else pl.BlockSpec((s0.shape[0],) + blk, lambda hp: (0, hp, 0, 0, 0))
    ones_blk = jnp.where(
        lax.shift_right_logical(_iota((pair, pair), 0), 6)
        == lax.shift_right_logical(_iota((pair, pair), 1), 6), 1.0, 0.0).astype(BF16)
    return pl.pallas_call(
        functools.partial(_rwkv_step_kernel, n_tok, has_vres, fill),
        grid=(n_pairs,),
        in_specs=[col(0), col(1), col(2), col(3), lora, lora, lora, col(2), up, up, up,
                  pl.BlockSpec((8, pair), lambda hp: (0, hp)), state,
                  pl.BlockSpec((pair, pair), lambda hp: (0, 0))]
        + [pl.BlockSpec(memory_space=pl.ANY)] * len(prev),
        out_specs=[pl.BlockSpec((m, pair), lambda hp: (0, hp)), state_out],
        out_shape=[jax.ShapeDtypeStruct((m, d), BF16), jax.ShapeDtypeStruct(s0.shape, F32)],
        scratch_shapes=[pltpu.VMEM((6, n_tok, pair, n_seq), F32),
                        pltpu.VMEM((n_tok, pair, n_seq), F32)],
        input_output_aliases={14: 1} if prev else {},
        compiler_params=pltpu.CompilerParams(
            dimension_semantics=("arbitrary",), vmem_limit_bytes=VMEM_LIMIT),
        name="rwkv_step",
    )(proj, proj, proj, proj, wm, am, vm, vf_src, w2p, a2p, v2p, prm, s0, ones_blk, *prev)


def _hgrn_head(chunk, tvalid, sl, head, refs, lb, nw, ones_blk):
    (q_ref, f_ref, i_ref, z_ref, g_ref, st_ref) = refs
    sub = min(HGRN_SUB, chunk)
    shift = int(math.log2(sub))
    fl = f_ref[:, sl]
    qraw = q_ref[:, sl]
    val = i_ref[:, sl]
    f = lb + (1.0 - lb) * _sigmoid(fl)
    logf = jnp.log(jnp.maximum(f, GATE_FLOOR))
    if tvalid < chunk:
        logf = jnp.where(_iota(logf.shape, 0) < tvalid, logf, 0.0)
    kg = (1.0 - lb) * _sigmoid(-fl)
    qs = qraw * _sigmoid(qraw) * (HGRN_DK ** -0.5)
    gcum = _cumsum_rows(logf)
    yield

    st = st_ref[head]
    o_inter = _mm_nt((qs * jnp.exp(gcum)).astype(BF16), st.astype(BF16))
    g_last = gcum[chunk - 1:chunk, :]
    k_tail = kg * jnp.exp(g_last - gcum)
    st_ref[head] = st * jnp.exp(g_last) + _mm_tn(val.astype(BF16), k_tail.astype(BF16))

    def bcast_rows(x):
        return jnp.concatenate(
            [jnp.broadcast_to(x[t:t + 1, :], (sub, HGRN_DK)) for t in range(tvalid)], axis=0)

    def tile_blocks(x):
        return jnp.concatenate(
            [x[(t >> shift) * sub:((t >> shift) + 1) * sub] for t in range(tvalid)], axis=0)

    rows = tvalid * sub
    prod = bcast_rows(qs) * tile_blocks(kg) * jnp.exp(
        jnp.minimum(bcast_rows(gcum) - tile_blocks(gcum), 0.0))
    att_diag = _mm(prod.astype(BF16), ones_blk)
    n_blk = -(-tvalid // sub)
    att_off = []
    for bi in range(1, n_blk):
        lo = bi * sub
        g_edge = gcum[lo - 1:lo]
        q_n = (qs[lo:lo + sub] * jnp.exp(gcum[lo:lo + sub] - g_edge)).astype(BF16)
        k_n = (kg[0:lo] * jnp.exp(g_edge - gcum[0:lo])).astype(BF16)
        att_off.append(_mm_nt(q_n, k_n))
    yield

    ridx = _iota((rows, HGRN_DK), 0)
    causal = (jnp.bitwise_and(ridx, sub - 1)
              <= jnp.bitwise_and(lax.shift_right_logical(ridx, shift), sub - 1))
    weighted = jnp.where(causal, att_diag * tile_blocks(val), 0.0)
    sel = jnp.where(lax.shift_right_logical(_iota((chunk, rows), 1), shift) == _iota((chunk, rows), 0),
                    1.0, 0.0).astype(BF16)
    o = o_inter + _mm(sel, weighted.astype(BF16))
    if n_blk > 1:
        parts = [jnp.zeros((sub, HGRN_DK), F32)]
        for bi in range(1, n_blk):
            parts.append(_mm(att_off[bi - 1].astype(BF16), val[0:bi * sub].astype(BF16)))
        if n_blk * sub < chunk:
            parts.append(jnp.zeros((chunk - n_blk * sub, HGRN_DK), F32))
        o = o + jnp.concatenate(parts, axis=0)
    yield

    o = o * lax.rsqrt(jnp.mean(o * o, axis=-1, keepdims=True) + NORM_EPS) * nw
    z = z_ref[:, sl]
    g_ref[:, sl] = (o * (z * _sigmoid(z))).astype(BF16)


def _hgrn_chunk_kernel(chunk, tvalid, layer_j, heads, fill,
                       q_ref, f_ref, i_ref, z_ref, lbl_ref, nw_ref, s0_ref, *rest):
    g_ref, so_ref, st_ref = rest[-3:]
    c = pl.program_id(2)
    n_chunks = pl.num_programs(2)

    @pl.when(c == 0)
    def _init():
        for hh in range(heads):
            st_ref[hh] = s0_ref[hh].T

    logits = lbl_ref[...]
    ex = jnp.exp(logits - jnp.max(logits, axis=0, keepdims=True))
    soft = ex / jnp.sum(ex, axis=0, keepdims=True)
    lb_all = jnp.sum(soft[0:layer_j + 1], axis=0, keepdims=True) - soft[0:1]

    ones_blk = jnp.ones((HGRN_DK, HGRN_DK), BF16)
    nw = nw_ref[...]
    refs = (q_ref, f_ref, i_ref, z_ref, g_ref, st_ref)
    _round_robin([
        _hgrn_head(chunk, tvalid, slice(hh * HGRN_DK, (hh + 1) * HGRN_DK), hh, refs,
                   lb_all[:, hh * HGRN_DK:(hh + 1) * HGRN_DK], nw, ones_blk)
        for hh in range(heads)])

    @pl.when(c == n_chunks - 1)
    def _fin():
        for hh in range(heads):
            _store_state(so_ref, fill, (hh,), st_ref[hh].T)


def _hgrn_chunk(proj, lb_logits, norm_w, s0, s_out_prev, *, layer_j, n_seq, n_chunks, chunk, tvalid):
    m = n_seq * n_chunks * chunk
    d = proj.shape[1] // 4
    n_heads = d // HGRN_DK
    heads = HGRN_HEADS_PER_STEP
    n_hb = n_heads // heads
    width = heads * HGRN_DK

    def col(off):
        return pl.BlockSpec((chunk, width), lambda s, h, c: (s * n_chunks + c, off * n_hb + h))

    state = pl.BlockSpec((None, None, heads, HGRN_DK, HGRN_DK), lambda s, h, c: (layer_j, s, h, 0, 0))
    prev = [] if s_out_prev is None else [s_out_prev]
    fill = None if prev else (layer_j, s0.shape[0])
    state_out = state if prev else pl.BlockSpec(
        (s0.shape[0], None, heads, HGRN_DK, HGRN_DK), lambda s, h, c: (0, s, h, 0, 0))
    kernel = functools.partial(_hgrn_chunk_kernel, chunk, tvalid, layer_j, heads, fill)
    return pl.pallas_call(
        kernel,
        grid=(n_seq, n_hb, n_chunks),
        in_specs=[col(0), col(1), col(2), col(3),
                  pl.BlockSpec((lb_logits.shape[0], width), lambda s, h, c: (0, h)),
                  pl.BlockSpec((1, HGRN_DK), lambda s, h, c: (0, 0)), state]
        + [pl.BlockSpec(memory_space=pl.ANY)] * len(prev),
        out_specs=[pl.BlockSpec((chunk, width), lambda s, h, c: (s * n_chunks + c, h)), state_out],
        out_shape=[jax.ShapeDtypeStruct((m, d), BF16),
                   jax.ShapeDtypeStruct(s0.shape, F32)],
        scratch_shapes=[pltpu.VMEM((heads, HGRN_DK, HGRN_DK), F32)],
        input_output_aliases={7: 1} if prev else {},
        compiler_params=pltpu.CompilerParams(
            dimension_semantics=("arbitrary", "arbitrary", "arbitrary"),
            vmem_limit_bytes=VMEM_LIMIT),
        name="hgrn_chunk",
    )(proj, proj, proj, proj, lb_logits, norm_w.reshape(1, HGRN_DK), s0, *prev)


def _pad_lanes(w, axis):
    pad = [(0, 0)] * w.ndim
    pad[axis] = (0, LORA_PAD - w.shape[axis])
    return jnp.pad(w, pad)


def _rwkv_layer_inputs(h, hprev, p, j, tiles_per_seq):
    has_vres = j > 0
    proj = _proj_rwkv(h, hprev, p["mu4"][j], p["w_in"][j], tiles_per_seq)
    wm, am, vm = _lora(h, hprev, p["mu6"][j], p["w1"][j], p["a1"][j], p["v1"][max(j - 1, 0)],
                       has_vres, tiles_per_seq)
    return proj, wm, am, vm, has_vres


def _trunk_prompt(x, shift0, wkv0, hgrn0, p, *, n_seq, n_tok):
    d = x.shape[1]
    n_chunks = n_tok // PROMPT_CHUNK
    shifts = []
    wkv_out, hgrn_out, vf_src = None, None, None
    h = _rmsnorm(x, p["norm_pre"][0])
    for layer in range(DEPTH):
        j = layer // 2
        npre_next = p["norm_pre"][layer + 1] if layer + 1 < DEPTH else None
        if layer % 2 == 0:
            shifts.append(h.reshape(n_seq, n_tok, d)[:, n_tok - 1])
            proj, wm, am, vm, has_vres = _rwkv_layer_inputs(
                h, shift0[j][:, None, :], p, j, n_tok // ROW_TILE)
            vf_src = proj if vf_src is None else vf_src
            g, wkv_out = _rwkv_chunk(
                proj, vf_src, wm, am, vm, p["w2"][j], p["a2"][j], p["v2"][max(j - 1, 0)],
                p["rwkv_prm"][j], wkv0, wkv_out, layer_j=j, n_seq=n_seq, n_chunks=n_chunks,
                chunk=PROMPT_CHUNK, tvalid=PROMPT_CHUNK, has_vres=has_vres)
            w_o = p["rwkv_w_o"][j]
        else:
            proj = _proj_hgrn(h, p["hgrn_w_in"][j])
            g, hgrn_out = _hgrn_chunk(
                proj, p["hgrn_lb_logits"], p["hgrn_norm_w"][j], hgrn0, hgrn_out, layer_j=j,
                n_seq=n_seq, n_chunks=n_chunks, chunk=PROMPT_CHUNK, tvalid=PROMPT_CHUNK)
            w_o = p["hgrn_w_o"][j]
        x, h = _outproj(g, w_o, x, p["norm_post"][layer], npre_next)
    return x, jnp.stack(shifts), wkv_out, hgrn_out


def _trunk_sample(x, shift0, wkv0, hgrn0, p, *, n_seq, n_tok):
    d = x.shape[1]
    tpad = SAMPLE_TPAD

    def to_padded(a):
        a3 = a.reshape(n_tok, n_seq, a.shape[1]).transpose(1, 0, 2)
        return jnp.pad(a3, ((0, 0), (0, tpad - n_tok), (0, 0))).reshape(n_seq * tpad, a.shape[1])

    def from_padded(a):
        a3 = a.reshape(n_seq, tpad, a.shape[1])[:, :n_tok]
        return a3.transpose(1, 0, 2).reshape(n_tok * n_seq, a.shape[1])

    shifts = []
    wkv_out, hgrn_out, vf_src = None, None, None
    h = _rmsnorm(x, p["norm_pre"][0])
    for layer in range(DEPTH):
        j = layer // 2
        npre_next = p["norm_pre"][layer + 1] if layer + 1 < DEPTH else None
        if layer % 2 == 0:
            hprev = jnp.concatenate([shift0[j], h[:(n_tok - 1) * n_seq]], axis=0)
            shifts.append(h[(n_tok - 1) * n_seq:])
            proj, wm, am, vm, has_vres = _rwkv_layer_inputs(h, hprev, p, j, 0)
            vf_src = proj if vf_src is None else vf_src
            g, wkv_out = _rwkv_step(
                proj, vf_src, wm, am, vm, p["w2"][j], p["a2"][j], p["v2"][max(j - 1, 0)],
                p["rwkv_prm"][j], wkv0, wkv_out, layer_j=j, n_tok=n_tok, has_vres=has_vres)
            w_o = p["rwkv_w_o"][j]
        else:
            proj = to_padded(_proj_hgrn(h, p["hgrn_w_in"][j]))
            g, hgrn_out = _hgrn_chunk(
                proj, p["hgrn_lb_logits"], p["hgrn_norm_w"][j], hgrn0, hgrn_out, layer_j=j,
                n_seq=n_seq, n_chunks=1, chunk=tpad, tvalid=n_tok)
            g = from_padded(g)
            w_o = p["hgrn_w_o"][j]
        x, h = _outproj(g, w_o, x, p["norm_post"][layer], npre_next)
    return x, jnp.stack(shifts), wkv_out, hgrn_out


def kernel(x_prompt, x_sample, state_rwkv_shift, state_rwkv_wkv, state_hgrn, norm_pre, norm_post,
           rwkv_mu, rwkv_w_in, rwkv_w0, rwkv_w1, rwkv_w2, rwkv_a0, rwkv_a1, rwkv_a2, rwkv_v0,
           rwkv_v1, rwkv_v2, rwkv_k_k, rwkv_k_a, rwkv_r_k, rwkv_ln_w, rwkv_ln_b, rwkv_w_o,
           hgrn_w_in, hgrn_lb_logits, hgrn_norm_w, hgrn_w_o):
    n_rwkv = rwkv_mu.shape[0]
    d = x_prompt.shape[-1]
    v0_full = jnp.concatenate([jnp.zeros((1, d), F32), rwkv_v0], axis=0)
    rwkv_prm = jnp.stack([rwkv_w0, rwkv_a0, v0_full, rwkv_k_k, rwkv_k_a,
                          rwkv_r_k.reshape(n_rwkv, d), rwkv_ln_w, rwkv_ln_b], axis=1)
    p = {
        "norm_pre": norm_pre, "norm_post": norm_post,
        "mu4": rwkv_mu[:, :4, None, :], "mu6": rwkv_mu,
        "w_in": rwkv_w_in.astype(BF16),
        "w1": _pad_lanes(rwkv_w1, 2).astype(BF16), "a1": _pad_lanes(rwkv_a1, 2).astype(BF16),
        "v1": _pad_lanes(rwkv_v1, 2).astype(BF16),
        "w2": _pad_lanes(rwkv_w2, 1).astype(BF16), "a2": _pad_lanes(rwkv_a2, 1).astype(BF16),
        "v2": _pad_lanes(rwkv_v2, 1).astype(BF16),
        "rwkv_prm": rwkv_prm, "rwkv_w_o": rwkv_w_o.astype(BF16),
        "hgrn_w_in": hgrn_w_in.astype(BF16), "hgrn_lb_logits": hgrn_lb_logits,
        "hgrn_norm_w": hgrn_norm_w, "hgrn_w_o": hgrn_w_o.astype(BF16),
    }

    bp, tp, _ = x_prompt.shape
    zero_shift = jnp.zeros((n_rwkv, bp, d), F32)
    zero_wkv = jnp.zeros((n_rwkv, bp) + state_rwkv_wkv.shape[2:], F32)
    zero_hgrn = jnp.zeros((state_hgrn.shape[0], bp) + state_hgrn.shape[2:], F32)
    y_p, p_shift, p_wkv, p_hgrn = _trunk_prompt(
        x_prompt.reshape(bp * tp, d), zero_shift, zero_wkv, zero_hgrn, p, n_seq=bp, n_tok=tp)

    bs, ts, _ = x_sample.shape
    y_s, s_shift, s_wkv, s_hgrn = _trunk_sample(
        x_sample.transpose(1, 0, 2).reshape(ts * bs, d), state_rwkv_shift,
        state_rwkv_wkv.transpose(0, 2, 3, 4, 1), state_hgrn, p, n_seq=bs, n_tok=ts)

    return (y_p.reshape(bp, tp, d), y_s.reshape(ts, bs, d).transpose(1, 0, 2),
            p_shift, p_wkv, p_hgrn, s_shift, s_wkv.transpose(0, 4, 1, 2, 3), s_hgrn)
```

```python
import functools
import math

import jax
import jax.numpy as jnp
from jax import lax
from jax.experimental import pallas as pl
from jax.experimental.pallas import tpu as pltpu

F32 = jnp.float32
BF16 = jnp.bfloat16

D_MODEL = 2048
DEPTH = 4
RWKV_HEAD = 64
HGRN_DK = 128
RWKV_GN_EPS = 1e-5 * RWKV_HEAD
NORM_EPS = 1e-6
GATE_FLOOR = 1e-30

LANES = 128
GROUP = 256
HEADS_PER_GROUP = GROUP // RWKV_HEAD
LORA_PAD = 128
PROMPT_CHUNK = 64
SAMPLE_TPAD = 16
HGRN_SUB = 8
HGRN_HEADS_PER_STEP = 16
ROW_TILE = 512
VMEM_LIMIT = 48 * 1024 * 1024


def _mm(a, b):
    return jnp.dot(a, b, preferred_element_type=F32)


def _mm_nt(a, b):
    return lax.dot_general(a, b, (((1,), (1,)), ((), ())), preferred_element_type=F32)


def _mm_tn(a, b):
    return lax.dot_general(a, b, (((0,), (0,)), ((), ())), preferred_element_type=F32)


def _iota(shape, dim):
    return lax.broadcasted_iota(jnp.int32, shape, dim)


def _split2(x):
    hi = x.astype(BF16)
    lo = (x - hi.astype(F32)).astype(BF16)
    return hi, lo


def _segsum(x, ones_blk):
    hi, lo = _split2(x)
    return _mm(hi, ones_blk) + _mm(lo, ones_blk)


def _segsum_many(xs, ones_blk):
    pieces = []
    for x in xs:
        pieces.extend(_split2(x))
    out = _mm(jnp.concatenate(pieces, axis=0), ones_blk)
    n = xs[0].shape[0]
    return [out[2 * i * n:(2 * i + 1) * n] + out[(2 * i + 1) * n:(2 * i + 2) * n] for i in range(len(xs))]


def _cumsum_rows(x):
    n = x.shape[0]
    tri = jnp.where(_iota((n, n), 0) >= _iota((n, n), 1), 1.0, 0.0).astype(BF16)
    x1 = x.astype(BF16)
    r1 = x - x1.astype(F32)
    x2 = r1.astype(BF16)
    x3 = (r1 - x2.astype(F32)).astype(BF16)
    return _mm(tri, x1) + _mm(tri, x2) + _mm(tri, x3)


def _block_diag(x, head_width, n_heads):
    shift = int(math.log2(head_width))
    lane_head = lax.shift_right_logical(_iota(x.shape, 1), shift)
    parts = [jnp.where(lane_head == h, x, 0.0).astype(BF16) for h in range(n_heads)]
    return jnp.concatenate(parts, axis=0)


def _sigmoid(x):
    return jax.nn.sigmoid(x)


def _log_decay(wl):
    return _sigmoid(wl) * (-math.exp(-0.5))


def _store_state(so_ref, fill, idx, value):
    if fill is None:
        so_ref[idx] = value
        return
    layer, n_layers = fill
    for other in range(n_layers):
        so_ref[(other,) + idx] = value if other == layer else jnp.zeros_like(value)


def _round_robin(gens):
    results = [None] * len(gens)
    active = list(range(len(gens)))
    while active:
        for idx in list(active):
            try:
                next(gens[idx])
            except StopIteration as stop:
                results[idx] = stop.value
                active.remove(idx)
    return results


def _rmsnorm_kernel(x_ref, w_ref, o_ref):
    x = x_ref[...]
    ms = jnp.mean(x * x, axis=-1, keepdims=True)
    o_ref[...] = x * lax.rsqrt(ms + NORM_EPS) * w_ref[...]


def _rmsnorm(x, w):
    m, d = x.shape
    return pl.pallas_call(
        _rmsnorm_kernel,
        grid=(m // ROW_TILE,),
        in_specs=[pl.BlockSpec((ROW_TILE, d), lambda i: (i, 0)),
                  pl.BlockSpec((1, d), lambda i: (0, 0))],
        out_specs=pl.BlockSpec((ROW_TILE, d), lambda i: (i, 0)),
        out_shape=jax.ShapeDtypeStruct((m, d), F32),
        name="rmsnorm",
    )(x, w.reshape(1, d))


def _prev_rows(h, first_ref, carry_ref, tile, tiles_per_seq):
    rolled = pltpu.roll(h, 1, axis=0)
    seq_start = lax.rem(tile, tiles_per_seq) == 0
    edge = jnp.where(seq_start, first_ref[...], carry_ref[7:8, :])
    carry_ref[...] = h[h.shape[0] - 8:, :]
    return jnp.where(_iota(h.shape, 0) == 0, edge, rolled)


def _proj_mix_kernel(tiles_per_seq, h_ref, hp_ref, mu_ref, w_ref, o_ref, *carry):
    h = h_ref[...]
    if tiles_per_seq:
        hprev = _prev_rows(h, hp_ref, carry[0], pl.program_id(1), tiles_per_seq)
    else:
        hprev = hp_ref[...]
    xs = h + (hprev - h) * mu_ref[...]
    o_ref[...] = _mm(xs.astype(BF16), w_ref[...])


def _proj_plain_kernel(h_ref, w_ref, o_ref):
    o_ref[...] = _mm(h_ref[...].astype(BF16), w_ref[...])


def _prev_spec(hprev, tiles_per_seq, d, grid_rank):
    if tiles_per_seq:
        if grid_rank == 2:
            return pl.BlockSpec((None, 1, d), lambda n, i: (i // tiles_per_seq, 0, 0))
        return pl.BlockSpec((None, 1, d), lambda i: (i // tiles_per_seq, 0, 0))
    if grid_rank == 2:
        return pl.BlockSpec((ROW_TILE, d), lambda n, i: (i, 0))
    return pl.BlockSpec((ROW_TILE, d), lambda i: (i, 0))


def _proj_rwkv(h, hprev, mu4, w4, tiles_per_seq):
    m, d = h.shape
    n_proj = w4.shape[0]
    return pl.pallas_call(
        functools.partial(_proj_mix_kernel, tiles_per_seq),
        grid=(n_proj, m // ROW_TILE),
        in_specs=[pl.BlockSpec((ROW_TILE, d), lambda n, i: (i, 0)),
                  _prev_spec(hprev, tiles_per_seq, d, 2),
                  pl.BlockSpec((None, 1, d), lambda n, i: (n, 0, 0)),
                  pl.BlockSpec((None, d, d), lambda n, i: (n, 0, 0))],
        out_specs=pl.BlockSpec((ROW_TILE, d), lambda n, i: (i, n)),
        out_shape=jax.ShapeDtypeStruct((m, n_proj * d), F32),
        scratch_shapes=[pltpu.VMEM((8, d), F32)] if tiles_per_seq else [],
        compiler_params=pltpu.CompilerParams(
            dimension_semantics=("arbitrary", "arbitrary"), vmem_limit_bytes=VMEM_LIMIT),
        name="proj_rwkv",
    )(h, hprev, mu4, w4)


def _proj_hgrn(h, w):
    m, d = h.shape
    n_proj = w.shape[1] // d
    return pl.pallas_call(
        _proj_plain_kernel,
        grid=(n_proj, m // ROW_TILE),
        in_specs=[pl.BlockSpec((ROW_TILE, d), lambda n, i: (i, 0)),
                  pl.BlockSpec((d, d), lambda n, i: (0, n))],
        out_specs=pl.BlockSpec((ROW_TILE, d), lambda n, i: (i, n)),
        out_shape=jax.ShapeDtypeStruct((m, n_proj * d), F32),
        compiler_params=pltpu.CompilerParams(
            dimension_semantics=("arbitrary", "arbitrary"), vmem_limit_bytes=VMEM_LIMIT),
        name="proj_hgrn",
    )(h, w)


def _lora_kernel(has_v, tiles_per_seq, h_ref, hp_ref, mu_ref, w1_ref, a1_ref, v1_ref,
                 wm_ref, am_ref, vm_ref, *carry):
    h = h_ref[...]
    if tiles_per_seq:
        hprev = _prev_rows(h, hp_ref, carry[0], pl.program_id(0), tiles_per_seq)
    else:
        hprev = hp_ref[...]
    delta = hprev - h
    xw = h + delta * mu_ref[4:5, :]
    xa = h + delta * mu_ref[5:6, :]
    wm_ref[...] = jnp.tanh(_mm(xw.astype(BF16), w1_ref[...]))
    am_ref[...] = _mm(xa.astype(BF16), a1_ref[...])
    if has_v:
        xv = h + delta * mu_ref[2:3, :]
        vm_ref[...] = _mm(xv.astype(BF16), v1_ref[...])
    else:
        vm_ref[...] = jnp.zeros(vm_ref.shape, F32)


def _lora(h, hprev, mu6, w1p, a1p, v1p, has_v, tiles_per_seq):
    m, d = h.shape
    row = pl.BlockSpec((ROW_TILE, d), lambda i: (i, 0))
    wspec = pl.BlockSpec((d, LORA_PAD), lambda i: (0, 0))
    ospec = pl.BlockSpec((ROW_TILE, LORA_PAD), lambda i: (i, 0))
    oshape = jax.ShapeDtypeStruct((m, LORA_PAD), F32)
    return pl.pallas_call(
        functools.partial(_lora_kernel, has_v, tiles_per_seq),
        grid=(m // ROW_TILE,),
        in_specs=[row, _prev_spec(hprev, tiles_per_seq, d, 1),
                  pl.BlockSpec(mu6.shape, lambda i: (0, 0)), wspec, wspec, wspec],
        out_specs=[ospec, ospec, ospec],
        out_shape=[oshape, oshape, oshape],
        scratch_shapes=[pltpu.VMEM((8, d), F32)] if tiles_per_seq else [],
        compiler_params=pltpu.CompilerParams(dimension_semantics=("arbitrary",)),
        name="lora",
    )(h, hprev, mu6, w1p, a1p, v1p)


def _outproj_kernel(g_ref, w_ref, x_ref, npost_ref, npre_ref, xo_ref, *ho_ref):
    out = _mm(g_ref[...], w_ref[...])
    ms = jnp.mean(out * out, axis=-1, keepdims=True)
    xn = x_ref[...] + out * lax.rsqrt(ms + NORM_EPS) * npost_ref[...]
    xo_ref[...] = xn
    if ho_ref:
        ms2 = jnp.mean(xn * xn, axis=-1, keepdims=True)
        ho_ref[0][...] = (xn * lax.rsqrt(ms2 + NORM_EPS) * npre_ref[...]).astype(ho_ref[0].dtype)


def _outproj(g, w, x, npost, npre_next, h_dtype):
    m, d = x.shape
    n_out = 1 if npre_next is None else 2
    if npre_next is None:
        npre_next = npost
    row = pl.BlockSpec((ROW_TILE, d), lambda i: (i, 0))
    vec = pl.BlockSpec((1, d), lambda i: (0, 0))
    out = pl.pallas_call(
        _outproj_kernel,
        grid=(m // ROW_TILE,),
        in_specs=[row, pl.BlockSpec((d, d), lambda i: (0, 0)), row, vec, vec],
        out_specs=[row, row][:n_out],
        out_shape=[jax.ShapeDtypeStruct((m, d), F32), jax.ShapeDtypeStruct((m, d), h_dtype)][:n_out],
        compiler_params=pltpu.CompilerParams(
            dimension_semantics=("arbitrary",), vmem_limit_bytes=VMEM_LIMIT),
        name="outproj",
    )(g, w, x, npost.reshape(1, d), npre_next.reshape(1, d))
    return (out[0], None) if n_out == 1 else out


def _rwkv_group(chunk, tvalid, has_vres, n_double, sl, grp, refs, wm, am, vm, ones_blk, diag):
    (r_ref, k_ref, v_ref, z_ref, vf_ref, w2_ref, a2_ref, v2_ref, prm_ref, g_ref, sbd_ref) = refs
    hw = RWKV_HEAD
    nh = HEADS_PER_GROUP
    prm = prm_ref[:, sl]
    w0, a0, v0 = prm[0:1], prm[1:2], prm[2:3]
    kk_w, ka_w, rk_w, ln_w, ln_b = prm[3:4], prm[4:5], prm[5:6], prm[6:7], prm[7:8]
    r = r_ref[:, sl]
    k = k_ref[:, sl]
    v = v_ref[:, sl]

    wl = w0 + _mm(wm, w2_ref[:, sl])
    al = a0 + _mm(am, a2_ref[:, sl])
    if has_vres:
        vl = v0 + _mm(vm, v2_ref[:, sl])
    yield
    logw = _log_decay(wl)
    if tvalid < chunk:
        logw = jnp.where(_iota(logw.shape, 0) < tvalid, logw, 0.0)
    alpha = _sigmoid(al)
    if has_vres:
        v = v + (vf_ref[:, sl] - v) * _sigmoid(vl)
    kk = k * kk_w
    k2 = k * (1.0 + (alpha - 1.0) * ka_w)
    kk_n2, bonus_s = _segsum_many([kk * kk, r * k2 * rk_w], ones_blk)
    cum = _cumsum_rows(logw)
    yield
    kk = kk / jnp.maximum(jnp.sqrt(kk_n2), 1e-12)
    b = kk * alpha
    a = -kk
    e_neg = jnp.exp(-cum)
    a_hat = a * jnp.exp(cum - logw)
    b_hat = b * e_neg
    k_hat = k2 * e_neg
    r_hat = r * jnp.exp(cum)
    cum_last = cum[chunk - 1:chunk, :]
    e_tail = jnp.exp(cum_last - cum)
    b_tail = b * e_tail
    k_tail = k2 * e_tail

    lhs = jnp.concatenate([a_hat, r_hat], axis=0).astype(BF16)
    gram_b = _mm_nt(lhs, _block_diag(b_hat, hw, nh))
    gram_k = _mm_nt(lhs, _block_diag(k_hat, hw, nh))
    s_bd = sbd_ref[grp]
    xy_state = _mm_nt(lhs, s_bd.astype(BF16))
    yield
    shape_cc = (chunk, nh * chunk)
    t_idx = _iota(shape_cc, 0)
    i_idx = jnp.bitwise_and(_iota(shape_cc, 1), chunk - 1)
    strict = i_idx < t_idx
    incl = i_idx <= t_idx
    n_ab = jnp.where(strict, gram_b[:chunk], 0.0)
    a_ak = jnp.where(strict, gram_k[:chunk], 0.0)
    a_rb = jnp.where(incl, gram_b[chunk:], 0.0)
    a_rk = jnp.where(incl, gram_k[chunk:], 0.0)
    v_bd = _block_diag(v, hw, nh)

    xy = xy_state + _mm(jnp.concatenate([a_ak, a_rk], axis=0).astype(BF16), v_bd)
    x = xy[:chunk]
    y_v = xy[chunk:]
    p = n_ab
    if n_double > 1:
        p_next = _mm(p.astype(BF16), _block_diag(p, chunk, nh))
    yield
    for j in range(n_double):
        x = x + _mm(p.astype(BF16), _block_diag(x, hw, nh))
        if j + 1 < n_double:
            p = p_next
            if j + 2 < n_double:
                p_next = _mm(p.astype(BF16), _block_diag(p, chunk, nh))
        yield
    u = x

    y = y_v + _mm(a_rb.astype(BF16), _block_diag(u, hw, nh))
    uv = jnp.concatenate([u, v], axis=0).astype(BF16)
    bk = jnp.concatenate([b_tail, k_tail], axis=0).astype(BF16)
    sbd_ref[grp] = s_bd * jnp.exp(cum_last) + jnp.where(diag, _mm_tn(uv, bk), 0.0)
    yield

    inv_n = 1.0 / RWKV_HEAD
    sum_y, sum_yy = _segsum_many([y, y * y], ones_blk)
    yield
    mean = sum_y * inv_n
    var = sum_yy * inv_n - mean * mean
    yn = (y - mean) * lax.rsqrt(var + RWKV_GN_EPS) * ln_w + ln_b
    z = z_ref[:, sl]
    g_ref[:, sl] = ((yn + bonus_s * v) * (z * _sigmoid(z))).astype(BF16)


def _rwkv_chunk_kernel(chunk, tvalid, has_vres, n_double, n_groups, fill,
                       r_ref, k_ref, v_ref, z_ref, wm_ref, am_ref, vm_ref, vf_ref,
                       w2_ref, a2_ref, v2_ref, prm_ref, s0_ref, ones_ref, *rest):
    g_ref, so_ref, sbd_ref = rest[-3:]
    c = pl.program_id(1)
    n_chunks = pl.num_programs(1)
    hw = RWKV_HEAD
    diag = (lax.shift_right_logical(_iota((GROUP, GROUP), 0), 6)
            == lax.shift_right_logical(_iota((GROUP, GROUP), 1), 6))

    @pl.when(c == 0)
    def _init():
        for g in range(n_groups):
            s4 = s0_ref[g * HEADS_PER_GROUP:(g + 1) * HEADS_PER_GROUP].reshape(GROUP, hw)
            tiled = jnp.concatenate([s4] * HEADS_PER_GROUP, axis=1)
            sbd_ref[g] = jnp.where(diag, tiled, 0.0)

    ones_blk = ones_ref[...]
    wm = wm_ref[...].astype(BF16)
    am = am_ref[...].astype(BF16)
    vm = vm_ref[...].astype(BF16)
    refs = (r_ref, k_ref, v_ref, z_ref, vf_ref, w2_ref, a2_ref, v2_ref, prm_ref, g_ref, sbd_ref)
    _round_robin([
        _rwkv_group(chunk, tvalid, has_vres, n_double, slice(g * GROUP, (g + 1) * GROUP), g, refs,
                    wm, am, vm, ones_blk, diag)
        for g in range(n_groups)])

    @pl.when(c == n_chunks - 1)
    def _fin():
        for g in range(n_groups):
            s_new = sbd_ref[g]
            dense = (s_new[:, 0:hw] + s_new[:, hw:2 * hw]
                     + s_new[:, 2 * hw:3 * hw] + s_new[:, 3 * hw:4 * hw])
            _store_state(so_ref, fill, (slice(g * HEADS_PER_GROUP, (g + 1) * HEADS_PER_GROUP),),
                         dense.reshape(HEADS_PER_GROUP, hw, hw))


def _rwkv_chunk(proj, vf_src, wm, am, vm, w2p, a2p, v2p, prm, s0, s_out_prev, *, layer_j, n_seq,
                n_chunks, chunk, tvalid, has_vres):
    m = n_seq * n_chunks * chunk
    d = proj.shape[1] // 4
    n_groups = d // GROUP
    n_double = max(1, math.ceil(math.log2(tvalid)))

    def col(off):
        return pl.BlockSpec((chunk, d), lambda s, c: (s * n_chunks + c, off))

    lora = pl.BlockSpec((chunk, LORA_PAD), lambda s, c: (s * n_chunks + c, 0))
    up = pl.BlockSpec((LORA_PAD, d), lambda s, c: (0, 0))
    state = pl.BlockSpec((None, None) + s0.shape[2:], lambda s, c: (layer_j, s, 0, 0, 0))
    prev = [] if s_out_prev is None else [s_out_prev]
    fill = None if prev else (layer_j, s0.shape[0])
    state_out = state if prev else pl.BlockSpec(
        (s0.shape[0], None) + s0.shape[2:], lambda s, c: (0, s, 0, 0, 0))
    ones_blk = jnp.where(
        lax.shift_right_logical(_iota((GROUP, GROUP), 0), 6)
        == lax.shift_right_logical(_iota((GROUP, GROUP), 1), 6), 1.0, 0.0).astype(BF16)
    kernel = functools.partial(_rwkv_chunk_kernel, chunk, tvalid, has_vres, n_double, n_groups, fill)
    return pl.pallas_call(
        kernel,
        grid=(n_seq, n_chunks),
        in_specs=[col(0), col(1), col(2), col(3), lora, lora, lora, col(2), up, up, up,
                  pl.BlockSpec((8, d), lambda s, c: (0, 0)), state,
                  pl.BlockSpec((GROUP, GROUP), lambda s, c: (0, 0))]
        + [pl.BlockSpec(memory_space=pl.ANY)] * len(prev),
        out_specs=[pl.BlockSpec((chunk, d), lambda s, c: (s * n_chunks + c, 0)), state_out],
        out_shape=[jax.ShapeDtypeStruct((m, d), BF16),
                   jax.ShapeDtypeStruct(s0.shape, F32)],
        scratch_shapes=[pltpu.VMEM((n_groups, GROUP, GROUP), F32)],
        input_output_aliases={14: 1} if prev else {},
        compiler_params=pltpu.CompilerParams(
            dimension_semantics=("arbitrary", "arbitrary"), vmem_limit_bytes=VMEM_LIMIT),
        name="rwkv_chunk",
    )(proj, proj, proj, proj, wm, am, vm, vf_src, w2p, a2p, v2p, prm, s0, ones_blk, *prev)


def _rwkv_step_kernel(n_tok, has_vres, fill,
                      r_ref, k_ref, v_ref, z_ref, wm_ref, am_ref, vm_ref, vf_ref,
                      w2_ref, a2_ref, v2_ref, prm_ref, s0_ref, ones_ref, *rest):
    g_ref, so_ref, tr_ref, y_ref = rest[-4:]
    hw = RWKV_HEAD
    n_seq = s0_ref.shape[-1]
    ones_blk = ones_ref[...]
    prm = prm_ref[...]
    w0, a0, v0 = prm[0:1], prm[1:2], prm[2:3]
    kk_w, ka_w, rk_w, ln_w, ln_b = prm[3:4], prm[4:5], prm[5:6], prm[6:7], prm[7:8]
    r = r_ref[...]
    k = k_ref[...]
    v = v_ref[...]

    wl = w0 + _mm(wm_ref[...].astype(BF16), w2_ref[...])
    decay = jnp.exp(_log_decay(wl))
    alpha = _sigmoid(a0 + _mm(am_ref[...].astype(BF16), a2_ref[...]))
    if has_vres:
        v = v + (vf_ref[...] - v) * _sigmoid(v0 + _mm(vm_ref[...].astype(BF16), v2_ref[...]))
    kk = k * kk_w
    k2 = k * (1.0 + (alpha - 1.0) * ka_w)
    kk_n2, bonus_s = _segsum_many([kk * kk, r * k2 * rk_w], ones_blk)
    kk = kk / jnp.maximum(jnp.sqrt(kk_n2), 1e-12)
    b = kk * alpha

    for idx, arr in enumerate((decay, k2, v, kk, b, r)):
        for t in range(n_tok):
            tr_ref[idx, t] = arr[t * n_seq:(t + 1) * n_seq, :].T

    def make_body(hh):
        lo = hh * hw

        def body(vi, carry):
            s = s0_ref[hh, vi]
            for t in range(n_tok):
                v_row = tr_ref[2, t, pl.ds(lo + vi, 1), :]
                sa = -jnp.sum(s * tr_ref[3, t, lo:lo + hw, :], axis=0, keepdims=True)
                s = (s * tr_ref[0, t, lo:lo + hw, :] + sa * tr_ref[4, t, lo:lo + hw, :]
                     + v_row * tr_ref[1, t, lo:lo + hw, :])
                y_ref[t, pl.ds(lo + vi, 1), :] = jnp.sum(
                    s * tr_ref[5, t, lo:lo + hw, :], axis=0, keepdims=True)
            _store_state(so_ref, fill, (hh, vi), s)
            return carry

        return body

    for hh in range(s0_ref.shape[0]):
        lax.fori_loop(0, hw, make_body(hh), 0, unroll=4)

    inv_n = 1.0 / hw
    for t in range(n_tok):
        rows = slice(t * n_seq, (t + 1) * n_seq)
        y = y_ref[t].T
        sum_y, sum_yy = _segsum_many([y, y * y], ones_blk)
        mean = sum_y * inv_n
        var = sum_yy * inv_n - mean * mean
        yn = (y - mean) * lax.rsqrt(var + RWKV_GN_EPS) * ln_w + ln_b
        z = z_ref[rows, :]
        g_ref[rows, :] = ((yn + bonus_s[rows] * v[rows]) * (z * _sigmoid(z))).astype(BF16)


def _rwkv_step(proj, vf_src, wm, am, vm, w2p, a2p, v2p, prm, s0, s_out_prev, *, layer_j, n_tok,
               has_vres):
    m = proj.shape[0]
    d = proj.shape[1] // 4
    pair = 2 * RWKV_HEAD
    n_pairs = d // pair
    n_seq = s0.shape[-1]

    def col(off):
        return pl.BlockSpec((m, pair), lambda hp: (0, off * n_pairs + hp))

    lora = pl.BlockSpec((m, LORA_PAD), lambda hp: (0, 0))
    up = pl.BlockSpec((LORA_PAD, pair), lambda hp: (0, hp))
    blk = (2, RWKV_HEAD, RWKV_HEAD, n_seq)
    state = pl.BlockSpec((None,) + blk, lambda hp: (layer_j, hp, 0, 0, 0))
    prev = [] if s_out_prev is None else [s_out_prev]
    fill = None if prev else (layer_j, s0.shape[0])
    state_out = state if prev else pl.BlockSpec((s0.shape[0],) + blk, lambda hp: (0, hp, 0, 0, 0))
    ones_blk = jnp.where(
        lax.shift_right_logical(_iota((pair, pair), 0), 6)
        == lax.shift_right_logical(_iota((pair, pair), 1), 6), 1.0, 0.0).astype(BF16)
    return pl.pallas_call(
        functools.partial(_rwkv_step_kernel, n_tok, has_vres, fill),
        grid=(n_pairs,),
        in_specs=[col(0), col(1), col(2), col(3), lora, lora, lora, col(2), up, up, up,
                  pl.BlockSpec((8, pair), lambda hp: (0, hp)), state,
                  pl.BlockSpec((pair, pair), lambda hp: (0, 0))]
        + [pl.BlockSpec(memory_space=pl.ANY)] * len(prev),
        out_specs=[pl.BlockSpec((m, pair), lambda hp: (0, hp)), state_out],
        out_shape=[jax.ShapeDtypeStruct((m, d), BF16), jax.ShapeDtypeStruct(s0.shape, F32)],
        scratch_shapes=[pltpu.VMEM((6, n_tok, pair, n_seq), F32),
                        pltpu.VMEM((n_tok, pair, n_seq), F32)],
        input_output_aliases={14: 1} if prev else {},
        compiler_params=pltpu.CompilerParams(
            dimension_semantics=("arbitrary",), vmem_limit_bytes=VMEM_LIMIT),
        name="rwkv_step",
    )(proj, proj, proj, proj, wm, am, vm, vf_src, w2p, a2p, v2p, prm, s0, ones_blk, *prev)


def _hgrn_head(chunk, tvalid, sl, head, refs, lb, nw, ones_blk):
    (q_ref, f_ref, i_ref, z_ref, g_ref, st_ref) = refs
    sub = min(HGRN_SUB, chunk)
    shift = int(math.log2(sub))
    fl = f_ref[:, sl]
    qraw = q_ref[:, sl]
    val = i_ref[:, sl]
    f = lb + (1.0 - lb) * _sigmoid(fl)
    logf = jnp.log(jnp.maximum(f, GATE_FLOOR))
    if tvalid < chunk:
        logf = jnp.where(_iota(logf.shape, 0) < tvalid, logf, 0.0)
    kg = (1.0 - lb) * _sigmoid(-fl)
    qs = qraw * _sigmoid(qraw) * (HGRN_DK ** -0.5)
    gcum = _cumsum_rows(logf)
    yield

    st = st_ref[head]
    o_inter = _mm_nt((qs * jnp.exp(gcum)).astype(BF16), st.astype(BF16))
    g_last = gcum[chunk - 1:chunk, :]
    k_tail = kg * jnp.exp(g_last - gcum)
    st_ref[head] = st * jnp.exp(g_last) + _mm_tn(val.astype(BF16), k_tail.astype(BF16))

    def bcast_rows(x):
        return jnp.concatenate(
            [jnp.broadcast_to(x[t:t + 1, :], (sub, HGRN_DK)) for t in range(tvalid)], axis=0)

    def tile_blocks(x):
        return jnp.concatenate(
            [x[(t >> shift) * sub:((t >> shift) + 1) * sub] for t in range(tvalid)], axis=0)

    rows = tvalid * sub
    prod = bcast_rows(qs) * tile_blocks(kg) * jnp.exp(
        jnp.minimum(bcast_rows(gcum) - tile_blocks(gcum), 0.0))
    att_diag = _mm(prod.astype(BF16), ones_blk)
    n_blk = -(-tvalid // sub)
    att_off = []
    for bi in range(1, n_blk):
        lo = bi * sub
        g_edge = gcum[lo - 1:lo]
        q_n = (qs[lo:lo + sub] * jnp.exp(gcum[lo:lo + sub] - g_edge)).astype(BF16)
        k_n = (kg[0:lo] * jnp.exp(g_edge - gcum[0:lo])).astype(BF16)
        att_off.append(_mm_nt(q_n, k_n))
    yield

    ridx = _iota((rows, HGRN_DK), 0)
    causal = (jnp.bitwise_and(ridx, sub - 1)
              <= jnp.bitwise_and(lax.shift_right_logical(ridx, shift), sub - 1))
    weighted = jnp.where(causal, att_diag * tile_blocks(val), 0.0)
    sel = jnp.where(lax.shift_right_logical(_iota((chunk, rows), 1), shift) == _iota((chunk, rows), 0),
                    1.0, 0.0).astype(BF16)
    o = o_inter + _mm(sel, weighted.astype(BF16))
    if n_blk > 1:
        parts = [jnp.zeros((sub, HGRN_DK), F32)]
        for bi in range(1, n_blk):
            parts.append(_mm(att_off[bi - 1].astype(BF16), val[0:bi * sub].astype(BF16)))
        if n_blk * sub < chunk:
            parts.append(jnp.zeros((chunk - n_blk * sub, HGRN_DK), F32))
        o = o + jnp.concatenate(parts, axis=0)
    yield

    o = o * lax.rsqrt(jnp.mean(o * o, axis=-1, keepdims=True) + NORM_EPS) * nw
    z = z_ref[:, sl]
    g_ref[:, sl] = (o * (z * _sigmoid(z))).astype(BF16)


def _hgrn_chunk_kernel(chunk, tvalid, layer_j, heads, fill,
                       q_ref, f_ref, i_ref, z_ref, lbl_ref, nw_ref, s0_ref, *rest):
    g_ref, so_ref, st_ref = rest[-3:]
    c = pl.program_id(2)
    n_chunks = pl.num_programs(2)

    @pl.when(c == 0)
    def _init():
        for hh in range(heads):
            st_ref[hh] = s0_ref[hh].T

    logits = lbl_ref[...]
    ex = jnp.exp(logits - jnp.max(logits, axis=0, keepdims=True))
    soft = ex / jnp.sum(ex, axis=0, keepdims=True)
    lb_all = jnp.sum(soft[0:layer_j + 1], axis=0, keepdims=True) - soft[0:1]

    ones_blk = jnp.ones((HGRN_DK, HGRN_DK), BF16)
    nw = nw_ref[...]
    refs = (q_ref, f_ref, i_ref, z_ref, g_ref, st_ref)
    _round_robin([
        _hgrn_head(chunk, tvalid, slice(hh * HGRN_DK, (hh + 1) * HGRN_DK), hh, refs,
                   lb_all[:, hh * HGRN_DK:(hh + 1) * HGRN_DK], nw, ones_blk)
        for hh in range(heads)])

    @pl.when(c == n_chunks - 1)
    def _fin():
        for hh in range(heads):
            _store_state(so_ref, fill, (hh,), st_ref[hh].T)


def _hgrn_chunk(proj, lb_logits, norm_w, s0, s_out_prev, *, layer_j, n_seq, n_chunks, chunk, tvalid):
    m = n_seq * n_chunks * chunk
    d = proj.shape[1] // 4
    n_heads = d // HGRN_DK
    heads = HGRN_HEADS_PER_STEP
    n_hb = n_heads // heads
    width = heads * HGRN_DK

    def col(off):
        return pl.BlockSpec((chunk, width), lambda s, h, c: (s * n_chunks + c, off * n_hb + h))

    state = pl.BlockSpec((None, None, heads, HGRN_DK, HGRN_DK), lambda s, h, c: (layer_j, s, h, 0, 0))
    prev = [] if s_out_prev is None else [s_out_prev]
    fill = None if prev else (layer_j, s0.shape[0])
    state_out = state if prev else pl.BlockSpec(
        (s0.shape[0], None, heads, HGRN_DK, HGRN_DK), lambda s, h, c: (0, s, h, 0, 0))
    kernel = functools.partial(_hgrn_chunk_kernel, chunk, tvalid, layer_j, heads, fill)
    return pl.pallas_call(
        kernel,
        grid=(n_seq, n_hb, n_chunks),
        in_specs=[col(0), col(1), col(2), col(3),
                  pl.BlockSpec((lb_logits.shape[0], width), lambda s, h, c: (0, h)),
                  pl.BlockSpec((1, HGRN_DK), lambda s, h, c: (0, 0)), state]
        + [pl.BlockSpec(memory_space=pl.ANY)] * len(prev),
        out_specs=[pl.BlockSpec((chunk, width), lambda s, h, c: (s * n_chunks + c, h)), state_out],
        out_shape=[jax.ShapeDtypeStruct((m, d), BF16),
                   jax.ShapeDtypeStruct(s0.shape, F32)],
        scratch_shapes=[pltpu.VMEM((heads, HGRN_DK, HGRN_DK), F32)],
        input_output_aliases={7: 1} if prev else {},
        compiler_params=pltpu.CompilerParams(
            dimension_semantics=("arbitrary", "arbitrary", "arbitrary"),
            vmem_limit_bytes=VMEM_LIMIT),
        name="hgrn_chunk",
    )(proj, proj, proj, proj, lb_logits, norm_w.reshape(1, HGRN_DK), s0, *prev)


def _pad_lanes(w, axis):
    pad = [(0, 0)] * w.ndim
    pad[axis] = (0, LORA_PAD - w.shape[axis])
    return jnp.pad(w, pad)


def _rwkv_layer_inputs(h, hprev, p, j, tiles_per_seq):
    has_vres = j > 0
    proj = _proj_rwkv(h, hprev, p["mu4"][j], p["w_in"][j], tiles_per_seq)
    wm, am, vm = _lora(h, hprev, p["mu6"][j], p["w1"][j], p["a1"][j], p["v1"][max(j - 1, 0)],
                       has_vres, tiles_per_seq)
    return proj, wm, am, vm, has_vres


def _trunk_prompt(x, shift0, wkv0, hgrn0, p, *, n_seq, n_tok):
    d = x.shape[1]
    n_chunks = n_tok // PROMPT_CHUNK
    shifts = []
    wkv_out, hgrn_out, vf_src = None, None, None
    h = _rmsnorm(x, p["norm_pre"][0])
    for layer in range(DEPTH):
        j = layer // 2
        npre_next = p["norm_pre"][layer + 1] if layer + 1 < DEPTH else None
        if layer % 2 == 0:
            shifts.append(h.reshape(n_seq, n_tok, d)[:, n_tok - 1])
            proj, wm, am, vm, has_vres = _rwkv_layer_inputs(
                h, shift0[j][:, None, :], p, j, n_tok // ROW_TILE)
            vf_src = proj if vf_src is None else vf_src
            g, wkv_out = _rwkv_chunk(
                proj, vf_src, wm, am, vm, p["w2"][j], p["a2"][j], p["v2"][max(j - 1, 0)],
                p["rwkv_prm"][j], wkv0, wkv_out, layer_j=j, n_seq=n_seq, n_chunks=n_chunks,
                chunk=PROMPT_CHUNK, tvalid=PROMPT_CHUNK, has_vres=has_vres)
            w_o = p["rwkv_w_o"][j]
        else:
            proj = _proj_hgrn(h, p["hgrn_w_in"][j])
            g, hgrn_out = _hgrn_chunk(
                proj, p["hgrn_lb_logits"], p["hgrn_norm_w"][j], hgrn0, hgrn_out, layer_j=j,
                n_seq=n_seq, n_chunks=n_chunks, chunk=PROMPT_CHUNK, tvalid=PROMPT_CHUNK)
            w_o = p["hgrn_w_o"][j]
        x, h = _outproj(g, w_o, x, p["norm_post"][layer], npre_next, BF16 if layer % 2 == 0 else F32)
    return x, jnp.stack(shifts), wkv_out, hgrn_out


def _trunk_sample(x, shift0, wkv0, hgrn0, p, *, n_seq, n_tok):
    d = x.shape[1]
    tpad = SAMPLE_TPAD

    def to_padded(a):
        a3 = a.reshape(n_tok, n_seq, a.shape[1]).transpose(1, 0, 2)
        return jnp.pad(a3, ((0, 0), (0, tpad - n_tok), (0, 0))).reshape(n_seq * tpad, a.shape[1])

    def from_padded(a):
        a3 = a.reshape(n_seq, tpad, a.shape[1])[:, :n_tok]
        return a3.transpose(1, 0, 2).reshape(n_tok * n_seq, a.shape[1])

    shifts = []
    wkv_out, hgrn_out, vf_src = None, None, None
    h = _rmsnorm(x, p["norm_pre"][0])
    for layer in range(DEPTH):
        j = layer // 2
        npre_next = p["norm_pre"][layer + 1] if layer + 1 < DEPTH else None
        if layer % 2 == 0:
            hprev = jnp.concatenate([shift0[j], h[:(n_tok - 1) * n_seq]], axis=0)
            shifts.append(h[(n_tok - 1) * n_seq:])
            proj, wm, am, vm, has_vres = _rwkv_layer_inputs(h, hprev, p, j, 0)
            vf_src = proj if vf_src is None else vf_src
            g, wkv_out = _rwkv_step(
                proj, vf_src, wm, am, vm, p["w2"][j], p["a2"][j], p["v2"][max(j - 1, 0)],
                p["rwkv_prm"][j], wkv0, wkv_out, layer_j=j, n_tok=n_tok, has_vres=has_vres)
            w_o = p["rwkv_w_o"][j]
        else:
            proj = to_padded(_proj_hgrn(h, p["hgrn_w_in"][j]))
            g, hgrn_out = _hgrn_chunk(
                proj, p["hgrn_lb_logits"], p["hgrn_norm_w"][j], hgrn0, hgrn_out, layer_j=j,
                n_seq=n_seq, n_chunks=1, chunk=tpad, tvalid=n_tok)
            g = from_padded(g)
            w_o = p["hgrn_w_o"][j]
        x, h = _outproj(g, w_o, x, p["norm_post"][layer], npre_next, BF16 if layer % 2 == 0 else F32)
    return x, jnp.stack(shifts), wkv_out, hgrn_out


def kernel(x_prompt, x_sample, state_rwkv_shift, state_rwkv_wkv, state_hgrn, norm_pre, norm_post,
           rwkv_mu, rwkv_w_in, rwkv_w0, rwkv_w1, rwkv_w2, rwkv_a0, rwkv_a1, rwkv_a2, rwkv_v0,
           rwkv_v1, rwkv_v2, rwkv_k_k, rwkv_k_a, rwkv_r_k, rwkv_ln_w, rwkv_ln_b, rwkv_w_o,
           hgrn_w_in, hgrn_lb_logits, hgrn_norm_w, hgrn_w_o):
    n_rwkv = rwkv_mu.shape[0]
    d = x_prompt.shape[-1]
    v0_full = jnp.concatenate([jnp.zeros((1, d), F32), rwkv_v0], axis=0)
    rwkv_prm = jnp.stack([rwkv_w0, rwkv_a0, v0_full, rwkv_k_k, rwkv_k_a,
                          rwkv_r_k.reshape(n_rwkv, d), rwkv_ln_w, rwkv_ln_b], axis=1)
    p = {
        "norm_pre": norm_pre, "norm_post": norm_post,
        "mu4": rwkv_mu[:, :4, None, :], "mu6": rwkv_mu,
        "w_in": rwkv_w_in.astype(BF16),
        "w1": _pad_lanes(rwkv_w1, 2).astype(BF16), "a1": _pad_lanes(rwkv_a1, 2).astype(BF16),
        "v1": _pad_lanes(rwkv_v1, 2).astype(BF16),
        "w2": _pad_lanes(rwkv_w2, 1).astype(BF16), "a2": _pad_lanes(rwkv_a2, 1).astype(BF16),
        "v2": _pad_lanes(rwkv_v2, 1).astype(BF16),
        "rwkv_prm": rwkv_prm, "rwkv_w_o": rwkv_w_o.astype(BF16),
        "hgrn_w_in": hgrn_w_in.astype(BF16), "hgrn_lb_logits": hgrn_lb_logits,
        "hgrn_norm_w": hgrn_norm_w, "hgrn_w_o": hgrn_w_o.astype(BF16),
    }

    bp, tp, _ = x_prompt.shape
    zero_shift = jnp.zeros((n_rwkv, bp, d), F32)
    zero_wkv = jnp.zeros((n_rwkv, bp) + state_rwkv_wkv.shape[2:], F32)
    zero_hgrn = jnp.zeros((state_hgrn.shape[0], bp) + state_hgrn.shape[2:], F32)
    y_p, p_shift, p_wkv, p_hgrn = _trunk_prompt(
        x_prompt.reshape(bp * tp, d), zero_shift, zero_wkv, zero_hgrn, p, n_seq=bp, n_tok=tp)

    bs, ts, _ = x_sample.shape
    y_s, s_shift, s_wkv, s_hgrn = _trunk_sample(
        x_sample.transpose(1, 0, 2).reshape(ts * bs, d), state_rwkv_shift,
        state_rwkv_wkv.transpose(0, 2, 3, 4, 1), state_hgrn, p, n_seq=bs, n_tok=ts)

    return (y_p.reshape(bp, tp, d), y_s.reshape(ts, bs, d).transpose(1, 0, 2),
            p_shift, p_wkv, p_hgrn, s_shift, s_wkv.transpose(0, 4, 1, 2, 3), s_hgrn)
```

```python
import functools
import math

import jax
import jax.numpy as jnp
from jax import lax
from jax.experimental import pallas as pl
from jax.experimental.pallas import tpu as pltpu

F32 = jnp.float32
BF16 = jnp.bfloat16

D_MODEL = 2048
DEPTH = 4
RWKV_HEAD = 64
HGRN_DK = 128
RWKV_GN_EPS = 1e-5 * RWKV_HEAD
NORM_EPS = 1e-6
GATE_FLOOR = 1e-30

LANES = 128
GROUP = 256
HEADS_PER_GROUP = GROUP // RWKV_HEAD
LORA_PAD = 128
PROMPT_CHUNK = 64
SAMPLE_TPAD = 16
HGRN_SUB = 8
HGRN_HEADS_PER_STEP = 16
HGRN_SAMPLE_SEQS = 2
ROW_TILE = 512
VMEM_LIMIT = 48 * 1024 * 1024


def _mm(a, b):
    return jnp.dot(a, b, preferred_element_type=F32)


def _mm_nt(a, b):
    return lax.dot_general(a, b, (((1,), (1,)), ((), ())), preferred_element_type=F32)


def _mm_tn(a, b):
    return lax.dot_general(a, b, (((0,), (0,)), ((), ())), preferred_element_type=F32)


def _iota(shape, dim):
    return lax.broadcasted_iota(jnp.int32, shape, dim)


def _split2(x):
    hi = x.astype(BF16)
    lo = (x - hi.astype(F32)).astype(BF16)
    return hi, lo


def _segsum(x, ones_blk):
    hi, lo = _split2(x)
    return _mm(hi, ones_blk) + _mm(lo, ones_blk)


def _segsum_many(xs, ones_blk):
    pieces = []
    for x in xs:
        pieces.extend(_split2(x))
    out = _mm(jnp.concatenate(pieces, axis=0), ones_blk)
    n = xs[0].shape[0]
    return [out[2 * i * n:(2 * i + 1) * n] + out[(2 * i + 1) * n:(2 * i + 2) * n] for i in range(len(xs))]


def _cumsum_rows(x):
    n = x.shape[0]
    tri = jnp.where(_iota((n, n), 0) >= _iota((n, n), 1), 1.0, 0.0).astype(BF16)
    x1 = x.astype(BF16)
    r1 = x - x1.astype(F32)
    x2 = r1.astype(BF16)
    x3 = (r1 - x2.astype(F32)).astype(BF16)
    return _mm(tri, x1) + _mm(tri, x2) + _mm(tri, x3)


def _block_diag(x, head_width, n_heads):
    shift = int(math.log2(head_width))
    lane_head = lax.shift_right_logical(_iota(x.shape, 1), shift)
    parts = [jnp.where(lane_head == h, x, 0.0).astype(BF16) for h in range(n_heads)]
    return jnp.concatenate(parts, axis=0)


def _sigmoid(x):
    return jax.nn.sigmoid(x)


def _log_decay(wl):
    return _sigmoid(wl) * (-math.exp(-0.5))


def _store_state(so_ref, fill, idx, value):
    if fill is None:
        so_ref[idx] = value
        return
    layer, n_layers = fill
    for other in range(n_layers):
        so_ref[(other,) + idx] = value if other == layer else jnp.zeros_like(value)


def _round_robin(gens):
    results = [None] * len(gens)
    active = list(range(len(gens)))
    while active:
        for idx in list(active):
            try:
                next(gens[idx])
            except StopIteration as stop:
                results[idx] = stop.value
                active.remove(idx)
    return results


def _rmsnorm_kernel(x_ref, w_ref, o_ref):
    x = x_ref[...]
    ms = jnp.mean(x * x, axis=-1, keepdims=True)
    o_ref[...] = x * lax.rsqrt(ms + NORM_EPS) * w_ref[...]


def _rmsnorm(x, w):
    m, d = x.shape
    return pl.pallas_call(
        _rmsnorm_kernel,
        grid=(m // ROW_TILE,),
        in_specs=[pl.BlockSpec((ROW_TILE, d), lambda i: (i, 0)),
                  pl.BlockSpec((1, d), lambda i: (0, 0))],
        out_specs=pl.BlockSpec((ROW_TILE, d), lambda i: (i, 0)),
        out_shape=jax.ShapeDtypeStruct((m, d), F32),
        name="rmsnorm",
    )(x, w.reshape(1, d))


def _prev_rows(h, first_ref, carry_ref, tile, tiles_per_seq):
    rolled = pltpu.roll(h, 1, axis=0)
    seq_start = lax.rem(tile, tiles_per_seq) == 0
    edge = jnp.where(seq_start, first_ref[...], carry_ref[7:8, :])
    carry_ref[...] = h[h.shape[0] - 8:, :]
    return jnp.where(_iota(h.shape, 0) == 0, edge, rolled)


def _proj_mix_kernel(tiles_per_seq, h_ref, hp_ref, mu_ref, w_ref, o_ref, *carry):
    h = h_ref[...]
    if tiles_per_seq:
        hprev = _prev_rows(h, hp_ref, carry[0], pl.program_id(1), tiles_per_seq)
    else:
        hprev = hp_ref[...]
    xs = h + (hprev - h) * mu_ref[...]
    o_ref[...] = _mm(xs.astype(BF16), w_ref[...])


def _proj_plain_kernel(h_ref, w_ref, o_ref):
    o_ref[...] = _mm(h_ref[...].astype(BF16), w_ref[...])


def _prev_spec(hprev, tiles_per_seq, d, grid_rank):
    if tiles_per_seq:
        if grid_rank == 2:
            return pl.BlockSpec((None, 1, d), lambda n, i: (i // tiles_per_seq, 0, 0))
        return pl.BlockSpec((None, 1, d), lambda i: (i // tiles_per_seq, 0, 0))
    if grid_rank == 2:
        return pl.BlockSpec((ROW_TILE, d), lambda n, i: (i, 0))
    return pl.BlockSpec((ROW_TILE, d), lambda i: (i, 0))


def _proj_rwkv(h, hprev, mu4, w4, tiles_per_seq):
    m, d = h.shape
    n_proj = w4.shape[0]
    return pl.pallas_call(
        functools.partial(_proj_mix_kernel, tiles_per_seq),
        grid=(n_proj, m // ROW_TILE),
        in_specs=[pl.BlockSpec((ROW_TILE, d), lambda n, i: (i, 0)),
                  _prev_spec(hprev, tiles_per_seq, d, 2),
                  pl.BlockSpec((None, 1, d), lambda n, i: (n, 0, 0)),
                  pl.BlockSpec((None, d, d), lambda n, i: (n, 0, 0))],
        out_specs=pl.BlockSpec((ROW_TILE, d), lambda n, i: (i, n)),
        out_shape=jax.ShapeDtypeStruct((m, n_proj * d), F32),
        scratch_shapes=[pltpu.VMEM((8, d), F32)] if tiles_per_seq else [],
        compiler_params=pltpu.CompilerParams(
            dimension_semantics=("arbitrary", "arbitrary"), vmem_limit_bytes=VMEM_LIMIT),
        name="proj_rwkv",
    )(h, hprev, mu4, w4)


def _proj_hgrn(h, w):
    m, d = h.shape
    n_proj = w.shape[1] // d
    return pl.pallas_call(
        _proj_plain_kernel,
        grid=(n_proj, m // ROW_TILE),
        in_specs=[pl.BlockSpec((ROW_TILE, d), lambda n, i: (i, 0)),
                  pl.BlockSpec((d, d), lambda n, i: (0, n))],
        out_specs=pl.BlockSpec((ROW_TILE, d), lambda n, i: (i, n)),
        out_shape=jax.ShapeDtypeStruct((m, n_proj * d), F32),
        compiler_params=pltpu.CompilerParams(
            dimension_semantics=("arbitrary", "arbitrary"), vmem_limit_bytes=VMEM_LIMIT),
        name="proj_hgrn",
    )(h, w)


def _lora_kernel(has_v, tiles_per_seq, h_ref, hp_ref, mu_ref, w1_ref, a1_ref, v1_ref,
                 wm_ref, am_ref, vm_ref, *carry):
    h = h_ref[...]
    if tiles_per_seq:
        hprev = _prev_rows(h, hp_ref, carry[0], pl.program_id(0), tiles_per_seq)
    else:
        hprev = hp_ref[...]
    delta = hprev - h
    xw = h + delta * mu_ref[4:5, :]
    xa = h + delta * mu_ref[5:6, :]
    wm_ref[...] = jnp.tanh(_mm(xw.astype(BF16), w1_ref[...]))
    am_ref[...] = _mm(xa.astype(BF16), a1_ref[...])
    if has_v:
        xv = h + delta * mu_ref[2:3, :]
        vm_ref[...] = _mm(xv.astype(BF16), v1_ref[...])
    else:
        vm_ref[...] = jnp.zeros(vm_ref.shape, F32)


def _lora(h, hprev, mu6, w1p, a1p, v1p, has_v, tiles_per_seq):
    m, d = h.shape
    row = pl.BlockSpec((ROW_TILE, d), lambda i: (i, 0))
    wspec = pl.BlockSpec((d, LORA_PAD), lambda i: (0, 0))
    ospec = pl.BlockSpec((ROW_TILE, LORA_PAD), lambda i: (i, 0))
    oshape = jax.ShapeDtypeStruct((m, LORA_PAD), F32)
    return pl.pallas_call(
        functools.partial(_lora_kernel, has_v, tiles_per_seq),
        grid=(m // ROW_TILE,),
        in_specs=[row, _prev_spec(hprev, tiles_per_seq, d, 1),
                  pl.BlockSpec(mu6.shape, lambda i: (0, 0)), wspec, wspec, wspec],
        out_specs=[ospec, ospec, ospec],
        out_shape=[oshape, oshape, oshape],
        scratch_shapes=[pltpu.VMEM((8, d), F32)] if tiles_per_seq else [],
        compiler_params=pltpu.CompilerParams(dimension_semantics=("arbitrary",)),
        name="lora",
    )(h, hprev, mu6, w1p, a1p, v1p)


def _outproj_kernel(g_ref, w_ref, x_ref, npost_ref, npre_ref, xo_ref, *ho_ref):
    half = g_ref.shape[0] // 2
    w = w_ref[...]
    outs = [_mm(g_ref[rows, :], w) for rows in (slice(0, half), slice(half, 2 * half))]
    for rows, out in zip((slice(0, half), slice(half, 2 * half)), outs):
        ms = jnp.mean(out * out, axis=-1, keepdims=True)
        xn = x_ref[rows, :] + out * lax.rsqrt(ms + NORM_EPS) * npost_ref[...]
        xo_ref[rows, :] = xn
        if ho_ref:
            ms2 = jnp.mean(xn * xn, axis=-1, keepdims=True)
            ho_ref[0][rows, :] = (xn * lax.rsqrt(ms2 + NORM_EPS) * npre_ref[...]).astype(ho_ref[0].dtype)


def _outproj(g, w, x, npost, npre_next, h_dtype):
    m, d = x.shape
    n_out = 1 if npre_next is None else 2
    if npre_next is None:
        npre_next = npost
    row = pl.BlockSpec((ROW_TILE, d), lambda i: (i, 0))
    vec = pl.BlockSpec((1, d), lambda i: (0, 0))
    out = pl.pallas_call(
        _outproj_kernel,
        grid=(m // ROW_TILE,),
        in_specs=[row, pl.BlockSpec((d, d), lambda i: (0, 0)), row, vec, vec],
        out_specs=[row, row][:n_out],
        out_shape=[jax.ShapeDtypeStruct((m, d), F32), jax.ShapeDtypeStruct((m, d), h_dtype)][:n_out],
        compiler_params=pltpu.CompilerParams(
            dimension_semantics=("arbitrary",), vmem_limit_bytes=VMEM_LIMIT),
        name="outproj",
    )(g, w, x, npost.reshape(1, d), npre_next.reshape(1, d))
    return (out[0], None) if n_out == 1 else out


def _rwkv_group(chunk, tvalid, has_vres, n_double, sl, grp, refs, wm, am, vm, ones_blk, diag):
    (r_ref, k_ref, v_ref, z_ref, vf_ref, w2_ref, a2_ref, v2_ref, prm_ref, g_ref, sbd_ref) = refs
    hw = RWKV_HEAD
    nh = HEADS_PER_GROUP
    prm = prm_ref[:, sl]
    w0, a0, v0 = prm[0:1], prm[1:2], prm[2:3]
    kk_w, ka_w, rk_w, ln_w, ln_b = prm[3:4], prm[4:5], prm[5:6], prm[6:7], prm[7:8]
    r = r_ref[:, sl]
    k = k_ref[:, sl]
    v = v_ref[:, sl]

    wl = w0 + _mm(wm, w2_ref[:, sl])
    al = a0 + _mm(am, a2_ref[:, sl])
    if has_vres:
        vl = v0 + _mm(vm, v2_ref[:, sl])
    yield
    logw = _log_decay(wl)
    if tvalid < chunk:
        logw = jnp.where(_iota(logw.shape, 0) < tvalid, logw, 0.0)
    alpha = _sigmoid(al)
    if has_vres:
        v = v + (vf_ref[:, sl] - v) * _sigmoid(vl)
    kk = k * kk_w
    k2 = k * (1.0 + (alpha - 1.0) * ka_w)
    kk_n2, bonus_s = _segsum_many([kk * kk, r * k2 * rk_w], ones_blk)
    cum = _cumsum_rows(logw)
    yield
    kk = kk / jnp.maximum(jnp.sqrt(kk_n2), 1e-12)
    b = kk * alpha
    a = -kk
    e_neg = jnp.exp(-cum)
    a_hat = a * jnp.exp(cum - logw)
    b_hat = b * e_neg
    k_hat = k2 * e_neg
    r_hat = r * jnp.exp(cum)
    cum_last = cum[chunk - 1:chunk, :]
    e_tail = jnp.exp(cum_last - cum)
    b_tail = b * e_tail
    k_tail = k2 * e_tail

    lhs = jnp.concatenate([a_hat, r_hat], axis=0).astype(BF16)
    gram_b = _mm_nt(lhs, _block_diag(b_hat, hw, nh))
    gram_k = _mm_nt(lhs, _block_diag(k_hat, hw, nh))
    s_bd = sbd_ref[grp]
    xy_state = _mm_nt(lhs, s_bd.astype(BF16))
    yield
    shape_cc = (chunk, nh * chunk)
    t_idx = _iota(shape_cc, 0)
    i_idx = jnp.bitwise_and(_iota(shape_cc, 1), chunk - 1)
    strict = i_idx < t_idx
    incl = i_idx <= t_idx
    n_ab = jnp.where(strict, gram_b[:chunk], 0.0)
    a_ak = jnp.where(strict, gram_k[:chunk], 0.0)
    a_rb = jnp.where(incl, gram_b[chunk:], 0.0)
    a_rk = jnp.where(incl, gram_k[chunk:], 0.0)
    v_bd = _block_diag(v, hw, nh)

    xy = xy_state + _mm(jnp.concatenate([a_ak, a_rk], axis=0).astype(BF16), v_bd)
    x = xy[:chunk]
    y_v = xy[chunk:]
    p = n_ab
    if n_double > 1:
        p_next = _mm(p.astype(BF16), _block_diag(p, chunk, nh))
    yield
    for j in range(n_double):
        x = x + _mm(p.astype(BF16), _block_diag(x, hw, nh))
        if j + 1 < n_double:
            p = p_next
            if j + 2 < n_double:
                p_next = _mm(p.astype(BF16), _block_diag(p, chunk, nh))
        yield
    u = x

    y = y_v + _mm(a_rb.astype(BF16), _block_diag(u, hw, nh))
    uv = jnp.concatenate([u, v], axis=0).astype(BF16)
    bk = jnp.concatenate([b_tail, k_tail], axis=0).astype(BF16)
    sbd_ref[grp] = s_bd * jnp.exp(cum_last) + jnp.where(diag, _mm_tn(uv, bk), 0.0)
    yield

    inv_n = 1.0 / RWKV_HEAD
    sum_y, sum_yy = _segsum_many([y, y * y], ones_blk)
    yield
    mean = sum_y * inv_n
    var = sum_yy * inv_n - mean * mean
    yn = (y - mean) * lax.rsqrt(var + RWKV_GN_EPS) * ln_w + ln_b
    z = z_ref[:, sl]
    g_ref[:, sl] = ((yn + bonus_s * v) * (z * _sigmoid(z))).astype(BF16)


def _rwkv_chunk_kernel(chunk, tvalid, has_vres, n_double, n_groups, fill,
                       r_ref, k_ref, v_ref, z_ref, wm_ref, am_ref, vm_ref, vf_ref,
                       w2_ref, a2_ref, v2_ref, prm_ref, s0_ref, ones_ref, *rest):
    g_ref, so_ref, sbd_ref = rest[-3:]
    c = pl.program_id(1)
    n_chunks = pl.num_programs(1)
    hw = RWKV_HEAD
    diag = (lax.shift_right_logical(_iota((GROUP, GROUP), 0), 6)
            == lax.shift_right_logical(_iota((GROUP, GROUP), 1), 6))

    @pl.when(c == 0)
    def _init():
        for g in range(n_groups):
            s4 = s0_ref[g * HEADS_PER_GROUP:(g + 1) * HEADS_PER_GROUP].reshape(GROUP, hw)
            tiled = jnp.concatenate([s4] * HEADS_PER_GROUP, axis=1)
            sbd_ref[g] = jnp.where(diag, tiled, 0.0)

    ones_blk = ones_ref[...]
    wm = wm_ref[...].astype(BF16)
    am = am_ref[...].astype(BF16)
    vm = vm_ref[...].astype(BF16)
    refs = (r_ref, k_ref, v_ref, z_ref, vf_ref, w2_ref, a2_ref, v2_ref, prm_ref, g_ref, sbd_ref)
    _round_robin([
        _rwkv_group(chunk, tvalid, has_vres, n_double, slice(g * GROUP, (g + 1) * GROUP), g, refs,
                    wm, am, vm, ones_blk, diag)
        for g in range(n_groups)])

    @pl.when(c == n_chunks - 1)
    def _fin():
        for g in range(n_groups):
            s_new = sbd_ref[g]
            dense = (s_new[:, 0:hw] + s_new[:, hw:2 * hw]
                     + s_new[:, 2 * hw:3 * hw] + s_new[:, 3 * hw:4 * hw])
            _store_state(so_ref, fill, (slice(g * HEADS_PER_GROUP, (g + 1) * HEADS_PER_GROUP),),
                         dense.reshape(HEADS_PER_GROUP, hw, hw))


def _rwkv_chunk(proj, vf_src, wm, am, vm, w2p, a2p, v2p, prm, s0, s_out_prev, *, layer_j, n_seq,
                n_chunks, chunk, tvalid, has_vres):
    m = n_seq * n_chunks * chunk
    d = proj.shape[1] // 4
    n_groups = d // GROUP
    n_double = max(1, math.ceil(math.log2(tvalid)))

    def col(off):
        return pl.BlockSpec((chunk, d), lambda s, c: (s * n_chunks + c, off))

    lora = pl.BlockSpec((chunk, LORA_PAD), lambda s, c: (s * n_chunks + c, 0))
    up = pl.BlockSpec((LORA_PAD, d), lambda s, c: (0, 0))
    state = pl.BlockSpec((None, None) + s0.shape[2:], lambda s, c: (layer_j, s, 0, 0, 0))
    prev = [] if s_out_prev is None else [s_out_prev]
    fill = None if prev else (layer_j, s0.shape[0])
    state_out = state if prev else pl.BlockSpec(
        (s0.shape[0], None) + s0.shape[2:], lambda s, c: (0, s, 0, 0, 0))
    ones_blk = jnp.where(
        lax.shift_right_logical(_iota((GROUP, GROUP), 0), 6)
        == lax.shift_right_logical(_iota((GROUP, GROUP), 1), 6), 1.0, 0.0).astype(BF16)
    kernel = functools.partial(_rwkv_chunk_kernel, chunk, tvalid, has_vres, n_double, n_groups, fill)
    return pl.pallas_call(
        kernel,
        grid=(n_seq, n_chunks),
        in_specs=[col(0), col(1), col(2), col(3), lora, lora, lora, col(2), up, up, up,
                  pl.BlockSpec((8, d), lambda s, c: (0, 0)), state,
                  pl.BlockSpec((GROUP, GROUP), lambda s, c: (0, 0))]
        + [pl.BlockSpec(memory_space=pl.ANY)] * len(prev),
        out_specs=[pl.BlockSpec((chunk, d), lambda s, c: (s * n_chunks + c, 0)), state_out],
        out_shape=[jax.ShapeDtypeStruct((m, d), BF16),
                   jax.ShapeDtypeStruct(s0.shape, F32)],
        scratch_shapes=[pltpu.VMEM((n_groups, GROUP, GROUP), F32)],
        input_output_aliases={14: 1} if prev else {},
        compiler_params=pltpu.CompilerParams(
            dimension_semantics=("arbitrary", "arbitrary"), vmem_limit_bytes=VMEM_LIMIT),
        name="rwkv_chunk",
    )(proj, proj, proj, proj, wm, am, vm, vf_src, w2p, a2p, v2p, prm, s0, ones_blk, *prev)


def _rwkv_step_kernel(n_tok, has_vres, fill,
                      r_ref, k_ref, v_ref, z_ref, wm_ref, am_ref, vm_ref, vf_ref,
                      w2_ref, a2_ref, v2_ref, prm_ref, s0_ref, ones_ref, *rest):
    g_ref, so_ref, tr_ref, y_ref = rest[-4:]
    hw = RWKV_HEAD
    n_seq = s0_ref.shape[-1]
    ones_blk = ones_ref[...]
    prm = prm_ref[...]
    w0, a0, v0 = prm[0:1], prm[1:2], prm[2:3]
    kk_w, ka_w, rk_w, ln_w, ln_b = prm[3:4], prm[4:5], prm[5:6], prm[6:7], prm[7:8]
    r = r_ref[...]
    k = k_ref[...]
    v = v_ref[...]

    wl = w0 + _mm(wm_ref[...].astype(BF16), w2_ref[...])
    decay = jnp.exp(_log_decay(wl))
    alpha = _sigmoid(a0 + _mm(am_ref[...].astype(BF16), a2_ref[...]))
    if has_vres:
        v = v + (vf_ref[...] - v) * _sigmoid(v0 + _mm(vm_ref[...].astype(BF16), v2_ref[...]))
    kk = k * kk_w
    k2 = k * (1.0 + (alpha - 1.0) * ka_w)
    kk_n2, bonus_s = _segsum_many([kk * kk, r * k2 * rk_w], ones_blk)
    kk = kk / jnp.maximum(jnp.sqrt(kk_n2), 1e-12)
    b = kk * alpha

    for idx, arr in enumerate((decay, k2, v, kk, b, r)):
        for t in range(n_tok):
            tr_ref[idx, t] = arr[t * n_seq:(t + 1) * n_seq, :].T

    def make_body(hh):
        lo = hh * hw

        def body(vi, carry):
            s = s0_ref[hh, vi]
            for t in range(n_tok):
                v_row = tr_ref[2, t, pl.ds(lo + vi, 1), :]
                sa = -jnp.sum(s * tr_ref[3, t, lo:lo + hw, :], axis=0, keepdims=True)
                s = (s * tr_ref[0, t, lo:lo + hw, :] + sa * tr_ref[4, t, lo:lo + hw, :]
                     + v_row * tr_ref[1, t, lo:lo + hw, :])
                y_ref[t, pl.ds(lo + vi, 1), :] = jnp.sum(
                    s * tr_ref[5, t, lo:lo + hw, :], axis=0, keepdims=True)
            _store_state(so_ref, fill, (hh, vi), s)
            return carry

        return body

    for hh in range(s0_ref.shape[0]):
        lax.fori_loop(0, hw, make_body(hh), 0, unroll=4)

    inv_n = 1.0 / hw
    for t in range(n_tok):
        rows = slice(t * n_seq, (t + 1) * n_seq)
        y = y_ref[t].T
        sum_y, sum_yy = _segsum_many([y, y * y], ones_blk)
        mean = sum_y * inv_n
        var = sum_yy * inv_n - mean * mean
        yn = (y - mean) * lax.rsqrt(var + RWKV_GN_EPS) * ln_w + ln_b
        z = z_ref[rows, :]
        g_ref[rows, :] = ((yn + bonus_s[rows] * v[rows]) * (z * _sigmoid(z))).astype(BF16)


def _rwkv_step(proj, vf_src, wm, am, vm, w2p, a2p, v2p, prm, s0, s_out_prev, *, layer_j, n_tok,
               has_vres):
    m = proj.shape[0]
    d = proj.shape[1] // 4
    pair = 2 * RWKV_HEAD
    n_pairs = d // pair
    n_seq = s0.shape[-1]

    def col(off):
        return pl.BlockSpec((m, pair), lambda hp: (0, off * n_pairs + hp))

    lora = pl.BlockSpec((m, LORA_PAD), lambda hp: (0, 0))
    up = pl.BlockSpec((LORA_PAD, pair), lambda hp: (0, hp))
    blk = (2, RWKV_HEAD, RWKV_HEAD, n_seq)
    state = pl.BlockSpec((None,) + blk, lambda hp: (layer_j, hp, 0, 0, 0))
    prev = [] if s_out_prev is None else [s_out_prev]
    fill = None if prev else (layer_j, s0.shape[0])
    state_out = state if prev else pl.BlockSpec((s0.shape[0],) + blk, lambda hp: (0, hp, 0, 0, 0))
    ones_blk = jnp.where(
        lax.shift_right_logical(_iota((pair, pair), 0), 6)
        == lax.shift_right_logical(_iota((pair, pair), 1), 6), 1.0, 0.0).astype(BF16)
    return pl.pallas_call(
        functools.partial(_rwkv_step_kernel, n_tok, has_vres, fill),
        grid=(n_pairs,),
        in_specs=[col(0), col(1), col(2), col(3), lora, lora, lora, col(2), up, up, up,
                  pl.BlockSpec((8, pair), lambda hp: (0, hp)), state,
                  pl.BlockSpec((pair, pair), lambda hp: (0, 0))]
        + [pl.BlockSpec(memory_space=pl.ANY)] * len(prev),
        out_specs=[pl.BlockSpec((m, pair), lambda hp: (0, hp)), state_out],
        out_shape=[jax.ShapeDtypeStruct((m, d), BF16), jax.ShapeDtypeStruct(s0.shape, F32)],
        scratch_shapes=[pltpu.VMEM((6, n_tok, pair, n_seq), F32),
                        pltpu.VMEM((n_tok, pair, n_seq), F32)],
        input_output_aliases={14: 1} if prev else {},
        compiler_params=pltpu.CompilerParams(
            dimension_semantics=("arbitrary",), vmem_limit_bytes=VMEM_LIMIT),
        name="rwkv_step",
    )(proj, proj, proj, proj, wm, am, vm, vf_src, w2p, a2p, v2p, prm, s0, ones_blk, *prev)


def _hgrn_head(chunk, tvalid, rows_sl, sl, head, refs, lb, nw, ones_blk):
    (q_ref, f_ref, i_ref, z_ref, g_ref, st_ref) = refs
    sub = min(HGRN_SUB, chunk)
    shift = int(math.log2(sub))
    fl = f_ref[rows_sl, sl]
    qraw = q_ref[rows_sl, sl]
    val = i_ref[rows_sl, sl]
    f = lb + (1.0 - lb) * _sigmoid(fl)
    logf = jnp.log(jnp.maximum(f, GATE_FLOOR))
    if tvalid < chunk:
        logf = jnp.where(_iota(logf.shape, 0) < tvalid, logf, 0.0)
    kg = (1.0 - lb) * _sigmoid(-fl)
    qs = qraw * _sigmoid(qraw) * (HGRN_DK ** -0.5)
    gcum = _cumsum_rows(logf)
    yield

    st = st_ref[head]
    o_inter = _mm_nt((qs * jnp.exp(gcum)).astype(BF16), st.astype(BF16))
    g_last = gcum[chunk - 1:chunk, :]
    k_tail = kg * jnp.exp(g_last - gcum)
    st_ref[head] = st * jnp.exp(g_last) + _mm_tn(val.astype(BF16), k_tail.astype(BF16))

    def bcast_rows(x):
        return jnp.concatenate(
            [jnp.broadcast_to(x[t:t + 1, :], (sub, HGRN_DK)) for t in range(tvalid)], axis=0)

    def tile_blocks(x):
        return jnp.concatenate(
            [x[(t >> shift) * sub:((t >> shift) + 1) * sub] for t in range(tvalid)], axis=0)

    rows = tvalid * sub
    prod = bcast_rows(qs) * tile_blocks(kg) * jnp.exp(
        jnp.minimum(bcast_rows(gcum) - tile_blocks(gcum), 0.0))
    att_diag = _mm(prod.astype(BF16), ones_blk)
    n_blk = -(-tvalid // sub)
    att_off = []
    for bi in range(1, n_blk):
        lo = bi * sub
        g_edge = gcum[lo - 1:lo]
        q_n = (qs[lo:lo + sub] * jnp.exp(gcum[lo:lo + sub] - g_edge)).astype(BF16)
        k_n = (kg[0:lo] * jnp.exp(g_edge - gcum[0:lo])).astype(BF16)
        att_off.append(_mm_nt(q_n, k_n))
    yield

    ridx = _iota((rows, HGRN_DK), 0)
    causal = (jnp.bitwise_and(ridx, sub - 1)
              <= jnp.bitwise_and(lax.shift_right_logical(ridx, shift), sub - 1))
    weighted = jnp.where(causal, att_diag * tile_blocks(val), 0.0)
    sel = jnp.where(lax.shift_right_logical(_iota((chunk, rows), 1), shift) == _iota((chunk, rows), 0),
                    1.0, 0.0).astype(BF16)
    o = o_inter + _mm(sel, weighted.astype(BF16))
    if n_blk > 1:
        parts = [jnp.zeros((sub, HGRN_DK), F32)]
        for bi in range(1, n_blk):
            parts.append(_mm(att_off[bi - 1].astype(BF16), val[0:bi * sub].astype(BF16)))
        if n_blk * sub < chunk:
            parts.append(jnp.zeros((chunk - n_blk * sub, HGRN_DK), F32))
        o = o + jnp.concatenate(parts, axis=0)
    yield

    o = o * lax.rsqrt(jnp.mean(o * o, axis=-1, keepdims=True) + NORM_EPS) * nw
    z = z_ref[rows_sl, sl]
    g_ref[rows_sl, sl] = (o * (z * _sigmoid(z))).astype(BF16)


def _hgrn_chunk_kernel(chunk, tvalid, layer_j, seqs, heads, fill,
                       q_ref, f_ref, i_ref, z_ref, lbl_ref, nw_ref, s0_ref, *rest):
    g_ref, so_ref, st_ref = rest[-3:]
    c = pl.program_id(2)
    n_chunks = pl.num_programs(2)

    @pl.when(c == 0)
    def _init():
        for q in range(seqs):
            for hh in range(heads):
                st_ref[q * heads + hh] = s0_ref[q, hh].T

    logits = lbl_ref[...]
    ex = jnp.exp(logits - jnp.max(logits, axis=0, keepdims=True))
    soft = ex / jnp.sum(ex, axis=0, keepdims=True)
    lb_all = jnp.sum(soft[0:layer_j + 1], axis=0, keepdims=True) - soft[0:1]

    ones_blk = jnp.ones((HGRN_DK, HGRN_DK), BF16)
    nw = nw_ref[...]
    refs = (q_ref, f_ref, i_ref, z_ref, g_ref, st_ref)
    _round_robin([
        _hgrn_head(chunk, tvalid, slice(q * chunk, (q + 1) * chunk),
                   slice(hh * HGRN_DK, (hh + 1) * HGRN_DK), q * heads + hh, refs,
                   lb_all[:, hh * HGRN_DK:(hh + 1) * HGRN_DK], nw, ones_blk)
        for q in range(seqs) for hh in range(heads)])

    @pl.when(c == n_chunks - 1)
    def _fin():
        for q in range(seqs):
            for hh in range(heads):
                _store_state(so_ref, fill, (q, hh), st_ref[q * heads + hh].T)


def _hgrn_chunk(proj, lb_logits, norm_w, s0, s_out_prev, *, layer_j, n_seq, n_chunks, chunk, tvalid,
                seqs=1):
    assert seqs == 1 or n_chunks == 1
    m = n_seq * n_chunks * chunk
    d = proj.shape[1] // 4
    n_heads = d // HGRN_DK
    heads = HGRN_HEADS_PER_STEP
    n_hb = n_heads // heads
    width = heads * HGRN_DK

    def col(off):
        return pl.BlockSpec((seqs * chunk, width), lambda s, h, c: (s * n_chunks + c, off * n_hb + h))

    state = pl.BlockSpec((None, seqs, heads, HGRN_DK, HGRN_DK), lambda s, h, c: (layer_j, s, h, 0, 0))
    prev = [] if s_out_prev is None else [s_out_prev]
    fill = None if prev else (layer_j, s0.shape[0])
    state_out = state if prev else pl.BlockSpec(
        (s0.shape[0], seqs, heads, HGRN_DK, HGRN_DK), lambda s, h, c: (0, s, h, 0, 0))
    kernel = functools.partial(_hgrn_chunk_kernel, chunk, tvalid, layer_j, seqs, heads, fill)
    return pl.pallas_call(
        kernel,
        grid=(n_seq // seqs, n_hb, n_chunks),
        in_specs=[col(0), col(1), col(2), col(3),
                  pl.BlockSpec((lb_logits.shape[0], width), lambda s, h, c: (0, h)),
                  pl.BlockSpec((1, HGRN_DK), lambda s, h, c: (0, 0)), state]
        + [pl.BlockSpec(memory_space=pl.ANY)] * len(prev),
        out_specs=[pl.BlockSpec((seqs * chunk, width), lambda s, h, c: (s * n_chunks + c, h)),
                   state_out],
        out_shape=[jax.ShapeDtypeStruct((m, d), BF16),
                   jax.ShapeDtypeStruct(s0.shape, F32)],
        scratch_shapes=[pltpu.VMEM((seqs * heads, HGRN_DK, HGRN_DK), F32)],
        input_output_aliases={7: 1} if prev else {},
        compiler_params=pltpu.CompilerParams(
            dimension_semantics=("arbitrary", "arbitrary", "arbitrary"),
            vmem_limit_bytes=VMEM_LIMIT),
        name="hgrn_chunk",
    )(proj, proj, proj, proj, lb_logits, norm_w.reshape(1, HGRN_DK), s0, *prev)


def _pad_lanes(w, axis):
    pad = [(0, 0)] * w.ndim
    pad[axis] = (0, LORA_PAD - w.shape[axis])
    return jnp.pad(w, pad)


def _rwkv_layer_inputs(h, hprev, p, j, tiles_per_seq):
    has_vres = j > 0
    proj = _proj_rwkv(h, hprev, p["mu4"][j], p["w_in"][j], tiles_per_seq)
    wm, am, vm = _lora(h, hprev, p["mu6"][j], p["w1"][j], p["a1"][j], p["v1"][max(j - 1, 0)],
                       has_vres, tiles_per_seq)
    return proj, wm, am, vm, has_vres


def _trunk_prompt(x, shift0, wkv0, hgrn0, p, *, n_seq, n_tok):
    d = x.shape[1]
    n_chunks = n_tok // PROMPT_CHUNK
    shifts = []
    wkv_out, hgrn_out, vf_src = None, None, None
    h = _rmsnorm(x, p["norm_pre"][0])
    for layer in range(DEPTH):
        j = layer // 2
        npre_next = p["norm_pre"][layer + 1] if layer + 1 < DEPTH else None
        if layer % 2 == 0:
            shifts.append(h[n_tok - 1::n_tok])
            proj, wm, am, vm, has_vres = _rwkv_layer_inputs(
                h, shift0[j][:, None, :], p, j, n_tok // ROW_TILE)
            vf_src = proj if vf_src is None else vf_src
            g, wkv_out = _rwkv_chunk(
                proj, vf_src, wm, am, vm, p["w2"][j], p["a2"][j], p["v2"][max(j - 1, 0)],
                p["rwkv_prm"][j], wkv0, wkv_out, layer_j=j, n_seq=n_seq, n_chunks=n_chunks,
                chunk=PROMPT_CHUNK, tvalid=PROMPT_CHUNK, has_vres=has_vres)
            w_o = p["rwkv_w_o"][j]
        else:
            proj = _proj_hgrn(h, p["hgrn_w_in"][j])
            g, hgrn_out = _hgrn_chunk(
                proj, p["hgrn_lb_logits"], p["hgrn_norm_w"][j], hgrn0, hgrn_out, layer_j=j,
                n_seq=n_seq, n_chunks=n_chunks, chunk=PROMPT_CHUNK, tvalid=PROMPT_CHUNK)
            w_o = p["hgrn_w_o"][j]
        x, h = _outproj(g, w_o, x, p["norm_post"][layer], npre_next, BF16 if layer % 2 == 0 else F32)
    return x, jnp.stack(shifts), wkv_out, hgrn_out


def _trunk_sample(x, shift0, wkv0, hgrn0, p, *, n_seq, n_tok):
    d = x.shape[1]
    tpad = SAMPLE_TPAD

    def to_padded(a):
        a3 = a.reshape(n_tok, n_seq, a.shape[1]).transpose(1, 0, 2)
        return jnp.pad(a3, ((0, 0), (0, tpad - n_tok), (0, 0))).reshape(n_seq * tpad, a.shape[1])

    def from_padded(a):
        a3 = a.reshape(n_seq, tpad, a.shape[1])[:, :n_tok]
        return a3.transpose(1, 0, 2).reshape(n_tok * n_seq, a.shape[1])

    shifts = []
    wkv_out, hgrn_out, vf_src = None, None, None
    h = _rmsnorm(x, p["norm_pre"][0])
    for layer in range(DEPTH):
        j = layer // 2
        npre_next = p["norm_pre"][layer + 1] if layer + 1 < DEPTH else None
        if layer % 2 == 0:
            hprev = jnp.concatenate([shift0[j], h[:(n_tok - 1) * n_seq]], axis=0)
            shifts.append(h[(n_tok - 1) * n_seq:])
            proj, wm, am, vm, has_vres = _rwkv_layer_inputs(h, hprev, p, j, 0)
            vf_src = proj if vf_src is None else vf_src
            g, wkv_out = _rwkv_step(
                proj, vf_src, wm, am, vm, p["w2"][j], p["a2"][j], p["v2"][max(j - 1, 0)],
                p["rwkv_prm"][j], wkv0, wkv_out, layer_j=j, n_tok=n_tok, has_vres=has_vres)
            w_o = p["rwkv_w_o"][j]
        else:
            proj = to_padded(_proj_hgrn(h, p["hgrn_w_in"][j]))
            g, hgrn_out = _hgrn_chunk(
                proj, p["hgrn_lb_logits"], p["hgrn_norm_w"][j], hgrn0, hgrn_out, layer_j=j,
                n_seq=n_seq, n_chunks=1, chunk=tpad, tvalid=n_tok, seqs=HGRN_SAMPLE_SEQS)
            g = from_padded(g)
            w_o = p["hgrn_w_o"][j]
        x, h = _outproj(g, w_o, x, p["norm_post"][layer], npre_next, BF16 if layer % 2 == 0 else F32)
    return x, jnp.stack(shifts), wkv_out, hgrn_out


def kernel(x_prompt, x_sample, state_rwkv_shift, state_rwkv_wkv, state_hgrn, norm_pre, norm_post,
           rwkv_mu, rwkv_w_in, rwkv_w0, rwkv_w1, rwkv_w2, rwkv_a0, rwkv_a1, rwkv_a2, rwkv_v0,
           rwkv_v1, rwkv_v2, rwkv_k_k, rwkv_k_a, rwkv_r_k, rwkv_ln_w, rwkv_ln_b, rwkv_w_o,
           hgrn_w_in, hgrn_lb_logits, hgrn_norm_w, hgrn_w_o):
    n_rwkv = rwkv_mu.shape[0]
    d = x_prompt.shape[-1]
    v0_full = jnp.concatenate([jnp.zeros((1, d), F32), rwkv_v0], axis=0)
    rwkv_prm = jnp.stack([rwkv_w0, rwkv_a0, v0_full, rwkv_k_k, rwkv_k_a,
                          rwkv_r_k.reshape(n_rwkv, d), rwkv_ln_w, rwkv_ln_b], axis=1)
    p = {
        "norm_pre": norm_pre, "norm_post": norm_post,
        "mu4": rwkv_mu[:, :4, None, :], "mu6": rwkv_mu,
        "w_in": rwkv_w_in.astype(BF16),
        "w1": _pad_lanes(rwkv_w1, 2).astype(BF16), "a1": _pad_lanes(rwkv_a1, 2).astype(BF16),
        "v1": _pad_lanes(rwkv_v1, 2).astype(BF16),
        "w2": _pad_lanes(rwkv_w2, 1).astype(BF16), "a2": _pad_lanes(rwkv_a2, 1).astype(BF16),
        "v2": _pad_lanes(rwkv_v2, 1).astype(BF16),
        "rwkv_prm": rwkv_prm, "rwkv_w_o": rwkv_w_o.astype(BF16),
        "hgrn_w_in": hgrn_w_in.astype(BF16), "hgrn_lb_logits": hgrn_lb_logits,
        "hgrn_norm_w": hgrn_norm_w, "hgrn_w_o": hgrn_w_o.astype(BF16),
    }

    bp, tp, _ = x_prompt.shape
    zero_shift = jnp.zeros((n_rwkv, bp, d), F32)
    zero_wkv = jnp.zeros((n_rwkv, bp) + state_rwkv_wkv.shape[2:], F32)
    zero_hgrn = jnp.zeros((state_hgrn.shape[0], bp) + state_hgrn.shape[2:], F32)
    y_p, p_shift, p_wkv, p_hgrn = _trunk_prompt(
        x_prompt.reshape(bp * tp, d), zero_shift, zero_wkv, zero_hgrn, p, n_seq=bp, n_tok=tp)

    bs, ts, _ = x_sample.shape
    y_s, s_shift, s_wkv, s_hgrn = _trunk_sample(
        x_sample.transpose(1, 0, 2).reshape(ts * bs, d), state_rwkv_shift,
        state_rwkv_wkv.transpose(0, 2, 3, 4, 1), state_hgrn, p, n_seq=bs, n_tok=ts)

    return (y_p.reshape(bp, tp, d), y_s.reshape(ts, bs, d).transpose(1, 0, 2),
            p_shift, p_wkv, p_hgrn, s_shift, s_wkv.transpose(0, 4, 1, 2, 3), s_hgrn)
```

```python
import functools
import math

import jax
import jax.numpy as jnp
from jax import lax
from jax.experimental import pallas as pl
from jax.experimental.pallas import tpu as pltpu

F32 = jnp.float32
BF16 = jnp.bfloat16

D_MODEL = 2048
DEPTH = 4
RWKV_HEAD = 64
HGRN_DK = 128
RWKV_GN_EPS = 1e-5 * RWKV_HEAD
NORM_EPS = 1e-6
GATE_FLOOR = 1e-30
LOG2_E = 1.0 / math.log(2.0)

LANES = 128
GROUP = 256
HEADS_PER_GROUP = GROUP // RWKV_HEAD
LORA_PAD = 128
PROMPT_CHUNK = 64
SAMPLE_TPAD = 16
HGRN_SUB = 8
HGRN_HEADS_PER_STEP = 16
HGRN_SAMPLE_SEQS = 2
ROW_TILE = 512
VMEM_LIMIT = 48 * 1024 * 1024


def _mm(a, b):
    return jnp.dot(a, b, preferred_element_type=F32)


def _mm_nt(a, b):
    return lax.dot_general(a, b, (((1,), (1,)), ((), ())), preferred_element_type=F32)


def _mm_tn(a, b):
    return lax.dot_general(a, b, (((0,), (0,)), ((), ())), preferred_element_type=F32)


def _iota(shape, dim):
    return lax.broadcasted_iota(jnp.int32, shape, dim)


def _split2(x):
    hi = x.astype(BF16)
    lo = (x - hi.astype(F32)).astype(BF16)
    return hi, lo


def _segsum(x, ones_blk):
    hi, lo = _split2(x)
    return _mm(hi, ones_blk) + _mm(lo, ones_blk)


def _segsum_many(xs, ones_blk):
    pieces = []
    for x in xs:
        pieces.extend(_split2(x))
    out = _mm(jnp.concatenate(pieces, axis=0), ones_blk)
    n = xs[0].shape[0]
    return [out[2 * i * n:(2 * i + 1) * n] + out[(2 * i + 1) * n:(2 * i + 2) * n] for i in range(len(xs))]


def _cumsum_rows(x):
    n = x.shape[0]
    tri = jnp.where(_iota((n, n), 0) >= _iota((n, n), 1), 1.0, 0.0).astype(BF16)
    x1 = x.astype(BF16)
    r1 = x - x1.astype(F32)
    x2 = r1.astype(BF16)
    x3 = (r1 - x2.astype(F32)).astype(BF16)
    return _mm(tri, x1) + _mm(tri, x2) + _mm(tri, x3)


def _block_diag(x, head_width, n_heads):
    shift = int(math.log2(head_width))
    lane_head = lax.shift_right_logical(_iota(x.shape, 1), shift)
    parts = [jnp.where(lane_head == h, x, 0.0).astype(BF16) for h in range(n_heads)]
    return jnp.concatenate(parts, axis=0)


def _sigmoid(x):
    return jax.nn.sigmoid(x)


def _log_decay(wl):
    return _sigmoid(wl) * (-math.exp(-0.5))


def _store_state(so_ref, fill, idx, value):
    if fill is None:
        so_ref[idx] = value
        return
    layer, n_layers = fill
    for other in range(n_layers):
        so_ref[(other,) + idx] = value if other == layer else jnp.zeros_like(value)


def _round_robin(gens):
    results = [None] * len(gens)
    active = list(range(len(gens)))
    while active:
        for idx in list(active):
            try:
                next(gens[idx])
            except StopIteration as stop:
                results[idx] = stop.value
                active.remove(idx)
    return results


def _rmsnorm_kernel(x_ref, w_ref, o_ref):
    x = x_ref[...]
    ms = jnp.mean(x * x, axis=-1, keepdims=True)
    o_ref[...] = x * lax.rsqrt(ms + NORM_EPS) * w_ref[...]


def _rmsnorm(x, w):
    m, d = x.shape
    return pl.pallas_call(
        _rmsnorm_kernel,
        grid=(m // ROW_TILE,),
        in_specs=[pl.BlockSpec((ROW_TILE, d), lambda i: (i, 0)),
                  pl.BlockSpec((1, d), lambda i: (0, 0))],
        out_specs=pl.BlockSpec((ROW_TILE, d), lambda i: (i, 0)),
        out_shape=jax.ShapeDtypeStruct((m, d), F32),
        name="rmsnorm",
    )(x, w.reshape(1, d))


def _prev_rows(h, first_ref, carry_ref, tile, tiles_per_seq):
    rolled = pltpu.roll(h, 1, axis=0)
    seq_start = lax.rem(tile, tiles_per_seq) == 0
    edge = jnp.where(seq_start, first_ref[...], carry_ref[7:8, :])
    carry_ref[...] = h[h.shape[0] - 8:, :]
    return jnp.where(_iota(h.shape, 0) == 0, edge, rolled)


def _proj_mix_kernel(tiles_per_seq, h_ref, hp_ref, mu_ref, w_ref, o_ref, *carry):
    h = h_ref[...]
    if tiles_per_seq:
        hprev = _prev_rows(h, hp_ref, carry[0], pl.program_id(1), tiles_per_seq)
    else:
        hprev = hp_ref[...]
    xs = h + (hprev - h) * mu_ref[...]
    o_ref[...] = _mm(xs.astype(BF16), w_ref[...])


def _proj_plain_kernel(h_ref, w_ref, o_ref):
    o_ref[...] = _mm(h_ref[...].astype(BF16), w_ref[...])


def _prev_spec(hprev, tiles_per_seq, d, grid_rank):
    if tiles_per_seq:
        if grid_rank == 2:
            return pl.BlockSpec((None, 1, d), lambda n, i: (i // tiles_per_seq, 0, 0))
        return pl.BlockSpec((None, 1, d), lambda i: (i // tiles_per_seq, 0, 0))
    if grid_rank == 2:
        return pl.BlockSpec((ROW_TILE, d), lambda n, i: (i, 0))
    return pl.BlockSpec((ROW_TILE, d), lambda i: (i, 0))


def _proj_rwkv(h, hprev, mu4, w4, layer_j, tiles_per_seq):
    m, d = h.shape
    n_proj = w4.shape[1]
    return pl.pallas_call(
        functools.partial(_proj_mix_kernel, tiles_per_seq),
        grid=(n_proj, m // ROW_TILE),
        in_specs=[pl.BlockSpec((ROW_TILE, d), lambda n, i: (i, 0)),
                  _prev_spec(hprev, tiles_per_seq, d, 2),
                  pl.BlockSpec((None, 1, d), lambda n, i: (n, 0, 0)),
                  pl.BlockSpec((None, None, d, d), lambda n, i: (layer_j, n, 0, 0))],
        out_specs=pl.BlockSpec((ROW_TILE, d), lambda n, i: (i, n)),
        out_shape=jax.ShapeDtypeStruct((m, n_proj * d), F32),
        scratch_shapes=[pltpu.VMEM((8, d), F32)] if tiles_per_seq else [],
        compiler_params=pltpu.CompilerParams(
            dimension_semantics=("arbitrary", "arbitrary"), vmem_limit_bytes=VMEM_LIMIT),
        name="proj_rwkv",
    )(h, hprev, mu4, w4)


def _proj_hgrn(h, w, layer_j):
    m, d = h.shape
    n_proj = w.shape[2] // d
    return pl.pallas_call(
        _proj_plain_kernel,
        grid=(n_proj, m // ROW_TILE),
        in_specs=[pl.BlockSpec((ROW_TILE, d), lambda n, i: (i, 0)),
                  pl.BlockSpec((None, d, d), lambda n, i: (layer_j, 0, n))],
        out_specs=pl.BlockSpec((ROW_TILE, d), lambda n, i: (i, n)),
        out_shape=jax.ShapeDtypeStruct((m, n_proj * d), F32),
        compiler_params=pltpu.CompilerParams(
            dimension_semantics=("arbitrary", "arbitrary"), vmem_limit_bytes=VMEM_LIMIT),
        name="proj_hgrn",
    )(h, w)


def _lora_kernel(has_v, tiles_per_seq, h_ref, hp_ref, mu_ref, w1_ref, a1_ref, v1_ref,
                 wm_ref, am_ref, vm_ref, *carry):
    h = h_ref[...]
    if tiles_per_seq:
        hprev = _prev_rows(h, hp_ref, carry[0], pl.program_id(0), tiles_per_seq)
    else:
        hprev = hp_ref[...]
    delta = hprev - h
    xw = h + delta * mu_ref[4:5, :]
    xa = h + delta * mu_ref[5:6, :]
    wm_ref[...] = jnp.tanh(_mm(xw.astype(BF16), w1_ref[...]))
    am_ref[...] = _mm(xa.astype(BF16), a1_ref[...])
    if has_v:
        xv = h + delta * mu_ref[2:3, :]
        vm_ref[...] = _mm(xv.astype(BF16), v1_ref[...])
    else:
        vm_ref[...] = jnp.zeros(vm_ref.shape, F32)


def _lora(h, hprev, mu6, w1p, a1p, v1p, has_v, tiles_per_seq):
    m, d = h.shape
    row = pl.BlockSpec((ROW_TILE, d), lambda i: (i, 0))
    wspec = pl.BlockSpec((d, LORA_PAD), lambda i: (0, 0))
    ospec = pl.BlockSpec((ROW_TILE, LORA_PAD), lambda i: (i, 0))
    oshape = jax.ShapeDtypeStruct((m, LORA_PAD), F32)
    return pl.pallas_call(
        functools.partial(_lora_kernel, has_v, tiles_per_seq),
        grid=(m // ROW_TILE,),
        in_specs=[row, _prev_spec(hprev, tiles_per_seq, d, 1),
                  pl.BlockSpec(mu6.shape, lambda i: (0, 0)), wspec, wspec, wspec],
        out_specs=[ospec, ospec, ospec],
        out_shape=[oshape, oshape, oshape],
        scratch_shapes=[pltpu.VMEM((8, d), F32)] if tiles_per_seq else [],
        compiler_params=pltpu.CompilerParams(dimension_semantics=("arbitrary",)),
        name="lora",
    )(h, hprev, mu6, w1p, a1p, v1p)


def _outproj_kernel(g_ref, w_ref, x_ref, npost_ref, npre_ref, xo_ref, *ho_ref):
    half = g_ref.shape[0] // 2
    w = w_ref[...]
    outs = [_mm(g_ref[rows, :], w) for rows in (slice(0, half), slice(half, 2 * half))]
    for rows, out in zip((slice(0, half), slice(half, 2 * half)), outs):
        ms = jnp.mean(out * out, axis=-1, keepdims=True)
        xn = x_ref[rows, :] + out * lax.rsqrt(ms + NORM_EPS) * npost_ref[...]
        xo_ref[rows, :] = xn
        if ho_ref:
            ms2 = jnp.mean(xn * xn, axis=-1, keepdims=True)
            ho_ref[0][rows, :] = (xn * lax.rsqrt(ms2 + NORM_EPS) * npre_ref[...]).astype(ho_ref[0].dtype)


def _outproj(g, w, layer_j, x, npost, npre_next, h_dtype):
    m, d = x.shape
    n_out = 1 if npre_next is None else 2
    if npre_next is None:
        npre_next = npost
    row = pl.BlockSpec((ROW_TILE, d), lambda i: (i, 0))
    vec = pl.BlockSpec((1, d), lambda i: (0, 0))
    out = pl.pallas_call(
        _outproj_kernel,
        grid=(m // ROW_TILE,),
        in_specs=[row, pl.BlockSpec((None, d, d), lambda i: (layer_j, 0, 0)), row, vec, vec],
        out_specs=[row, row][:n_out],
        out_shape=[jax.ShapeDtypeStruct((m, d), F32), jax.ShapeDtypeStruct((m, d), h_dtype)][:n_out],
        compiler_params=pltpu.CompilerParams(
            dimension_semantics=("arbitrary",), vmem_limit_bytes=VMEM_LIMIT),
        name="outproj",
    )(g, w, x, npost.reshape(1, d), npre_next.reshape(1, d))
    return (out[0], None) if n_out == 1 else out


def _rwkv_group(chunk, tvalid, has_vres, n_double, sl, grp, refs, wm, am, vm, ones_blk, diag):
    (r_ref, k_ref, v_ref, z_ref, vf_ref, w2_ref, a2_ref, v2_ref, prm_ref, g_ref, sbd_ref) = refs
    hw = RWKV_HEAD
    nh = HEADS_PER_GROUP
    prm = prm_ref[:, sl]
    w0, a0, v0 = prm[0:1], prm[1:2], prm[2:3]
    kk_w, ka_w, rk_w, ln_w, ln_b = prm[3:4], prm[4:5], prm[5:6], prm[6:7], prm[7:8]
    r = r_ref[:, sl]
    k = k_ref[:, sl]
    v = v_ref[:, sl]

    wl = w0 + _mm(wm, w2_ref[:, sl])
    al = a0 + _mm(am, a2_ref[:, sl])
    if has_vres:
        vl = v0 + _mm(vm, v2_ref[:, sl])
    yield
    logw = _log_decay(wl)
    if tvalid < chunk:
        logw = jnp.where(_iota(logw.shape, 0) < tvalid, logw, 0.0)
    alpha = _sigmoid(al)
    if has_vres:
        v = v + (vf_ref[:, sl] - v) * _sigmoid(vl)
    kk = k * kk_w
    k2 = k * (1.0 + (alpha - 1.0) * ka_w)
    kk_n2, bonus_s = _segsum_many([kk * kk, r * k2 * rk_w], ones_blk)
    cum = _cumsum_rows(logw)
    yield
    kk = kk / jnp.maximum(jnp.sqrt(kk_n2), 1e-12)
    b = kk * alpha
    a = -kk
    e_neg = jnp.exp(-cum)
    a_hat = a * jnp.exp(cum - logw)
    b_hat = b * e_neg
    k_hat = k2 * e_neg
    r_hat = r * jnp.exp(cum)
    cum_last = cum[chunk - 1:chunk, :]
    e_tail = jnp.exp(cum_last - cum)
    b_tail = b * e_tail
    k_tail = k2 * e_tail

    lhs = jnp.concatenate([a_hat, r_hat], axis=0).astype(BF16)
    gram_b = _mm_nt(lhs, _block_diag(b_hat, hw, nh))
    gram_k = _mm_nt(lhs, _block_diag(k_hat, hw, nh))
    s_bd = sbd_ref[grp]
    xy_state = _mm_nt(lhs, s_bd.astype(BF16))
    yield
    shape_cc = (chunk, nh * chunk)
    t_idx = _iota(shape_cc, 0)
    i_idx = jnp.bitwise_and(_iota(shape_cc, 1), chunk - 1)
    strict = i_idx < t_idx
    incl = i_idx <= t_idx
    n_ab = jnp.where(strict, gram_b[:chunk], 0.0)
    a_ak = jnp.where(strict, gram_k[:chunk], 0.0)
    a_rb = jnp.where(incl, gram_b[chunk:], 0.0)
    a_rk = jnp.where(incl, gram_k[chunk:], 0.0)
    v_bd = _block_diag(v, hw, nh)

    xy = xy_state + _mm(jnp.concatenate([a_ak, a_rk], axis=0).astype(BF16), v_bd)
    x = xy[:chunk]
    y_v = xy[chunk:]
    p = n_ab
    if n_double > 1:
        p_next = _mm(p.astype(BF16), _block_diag(p, chunk, nh))
    yield
    for j in range(n_double):
        x = x + _mm(p.astype(BF16), _block_diag(x, hw, nh))
        if j + 1 < n_double:
            p = p_next
            if j + 2 < n_double:
                p_next = _mm(p.astype(BF16), _block_diag(p, chunk, nh))
        yield
    u = x

    y = y_v + _mm(a_rb.astype(BF16), _block_diag(u, hw, nh))
    uv = jnp.concatenate([u, v], axis=0).astype(BF16)
    bk = jnp.concatenate([b_tail, k_tail], axis=0).astype(BF16)
    sbd_ref[grp] = s_bd * jnp.exp(cum_last) + jnp.where(diag, _mm_tn(uv, bk), 0.0)
    yield

    inv_n = 1.0 / RWKV_HEAD
    sum_y, sum_yy = _segsum_many([y, y * y], ones_blk)
    yield
    mean = sum_y * inv_n
    var = sum_yy * inv_n - mean * mean
    yn = (y - mean) * lax.rsqrt(var + RWKV_GN_EPS) * ln_w + ln_b
    z = z_ref[:, sl]
    g_ref[:, sl] = ((yn + bonus_s * v) * (z * _sigmoid(z))).astype(BF16)


def _rwkv_chunk_kernel(chunk, tvalid, has_vres, n_double, n_groups, fill,
                       r_ref, k_ref, v_ref, z_ref, wm_ref, am_ref, vm_ref, vf_ref,
                       w2_ref, a2_ref, v2_ref, prm_ref, s0_ref, ones_ref, *rest):
    g_ref, so_ref, sbd_ref = rest[-3:]
    c = pl.program_id(1)
    n_chunks = pl.num_programs(1)
    hw = RWKV_HEAD
    diag = (lax.shift_right_logical(_iota((GROUP, GROUP), 0), 6)
            == lax.shift_right_logical(_iota((GROUP, GROUP), 1), 6))

    @pl.when(c == 0)
    def _init():
        for g in range(n_groups):
            s4 = s0_ref[g * HEADS_PER_GROUP:(g + 1) * HEADS_PER_GROUP].reshape(GROUP, hw)
            tiled = jnp.concatenate([s4] * HEADS_PER_GROUP, axis=1)
            sbd_ref[g] = jnp.where(diag, tiled, 0.0)

    ones_blk = ones_ref[...]
    wm = wm_ref[...].astype(BF16)
    am = am_ref[...].astype(BF16)
    vm = vm_ref[...].astype(BF16)
    refs = (r_ref, k_ref, v_ref, z_ref, vf_ref, w2_ref, a2_ref, v2_ref, prm_ref, g_ref, sbd_ref)
    _round_robin([
        _rwkv_group(chunk, tvalid, has_vres, n_double, slice(g * GROUP, (g + 1) * GROUP), g, refs,
                    wm, am, vm, ones_blk, diag)
        for g in range(n_groups)])

    @pl.when(c == n_chunks - 1)
    def _fin():
        for g in range(n_groups):
            s_new = sbd_ref[g]
            dense = (s_new[:, 0:hw] + s_new[:, hw:2 * hw]
                     + s_new[:, 2 * hw:3 * hw] + s_new[:, 3 * hw:4 * hw])
            _store_state(so_ref, fill, (slice(g * HEADS_PER_GROUP, (g + 1) * HEADS_PER_GROUP),),
                         dense.reshape(HEADS_PER_GROUP, hw, hw))


def _rwkv_chunk(proj, vf_src, wm, am, vm, w2p, a2p, v2p, prm, s0, s_out_prev, *, layer_j, n_seq,
                n_chunks, chunk, tvalid, has_vres):
    m = n_seq * n_chunks * chunk
    d = proj.shape[1] // 4
    n_groups = d // GROUP
    n_double = max(1, math.ceil(math.log2(tvalid)))

    def col(off):
        return pl.BlockSpec((chunk, d), lambda s, c: (s * n_chunks + c, off))

    lora = pl.BlockSpec((chunk, LORA_PAD), lambda s, c: (s * n_chunks + c, 0))
    up = pl.BlockSpec((LORA_PAD, d), lambda s, c: (0, 0))
    state = pl.BlockSpec((None, None) + s0.shape[2:], lambda s, c: (layer_j, s, 0, 0, 0))
    prev = [] if s_out_prev is None else [s_out_prev]
    fill = None if prev else (layer_j, s0.shape[0])
    state_out = state if prev else pl.BlockSpec(
        (s0.shape[0], None) + s0.shape[2:], lambda s, c: (0, s, 0, 0, 0))
    ones_blk = jnp.where(
        lax.shift_right_logical(_iota((GROUP, GROUP), 0), 6)
        == lax.shift_right_logical(_iota((GROUP, GROUP), 1), 6), 1.0, 0.0).astype(BF16)
    kernel = functools.partial(_rwkv_chunk_kernel, chunk, tvalid, has_vres, n_double, n_groups, fill)
    return pl.pallas_call(
        kernel,
        grid=(n_seq, n_chunks),
        in_specs=[col(0), col(1), col(2), col(3), lora, lora, lora, col(2), up, up, up,
                  pl.BlockSpec((8, d), lambda s, c: (0, 0)), state,
                  pl.BlockSpec((GROUP, GROUP), lambda s, c: (0, 0))]
        + [pl.BlockSpec(memory_space=pl.ANY)] * len(prev),
        out_specs=[pl.BlockSpec((chunk, d), lambda s, c: (s * n_chunks + c, 0)), state_out],
        out_shape=[jax.ShapeDtypeStruct((m, d), BF16),
                   jax.ShapeDtypeStruct(s0.shape, F32)],
        scratch_shapes=[pltpu.VMEM((n_groups, GROUP, GROUP), F32)],
        input_output_aliases={14: 1} if prev else {},
        compiler_params=pltpu.CompilerParams(
            dimension_semantics=("arbitrary", "arbitrary"), vmem_limit_bytes=VMEM_LIMIT),
        name="rwkv_chunk",
    )(proj, proj, proj, proj, wm, am, vm, vf_src, w2p, a2p, v2p, prm, s0, ones_blk, *prev)


def _rwkv_step_kernel(n_tok, has_vres, fill,
                      r_ref, k_ref, v_ref, z_ref, wm_ref, am_ref, vm_ref, vf_ref,
                      w2_ref, a2_ref, v2_ref, prm_ref, s0_ref, ones_ref, *rest):
    g_ref, so_ref, tr_ref, y_ref = rest[-4:]
    hw = RWKV_HEAD
    n_seq = s0_ref.shape[-1]
    ones_blk = ones_ref[...]
    prm = prm_ref[...]
    w0, a0, v0 = prm[0:1], prm[1:2], prm[2:3]
    kk_w, ka_w, rk_w, ln_w, ln_b = prm[3:4], prm[4:5], prm[5:6], prm[6:7], prm[7:8]
    r = r_ref[...]
    k = k_ref[...]
    v = v_ref[...]

    wl = w0 + _mm(wm_ref[...].astype(BF16), w2_ref[...])
    decay = jnp.exp(_log_decay(wl))
    alpha = _sigmoid(a0 + _mm(am_ref[...].astype(BF16), a2_ref[...]))
    if has_vres:
        v = v + (vf_ref[...] - v) * _sigmoid(v0 + _mm(vm_ref[...].astype(BF16), v2_ref[...]))
    kk = k * kk_w
    k2 = k * (1.0 + (alpha - 1.0) * ka_w)
    kk_n2, bonus_s = _segsum_many([kk * kk, r * k2 * rk_w], ones_blk)
    kk = kk / jnp.maximum(jnp.sqrt(kk_n2), 1e-12)
    b = kk * alpha

    for idx, arr in enumerate((decay, k2, v, kk, b, r)):
        for t in range(n_tok):
            tr_ref[idx, t] = arr[t * n_seq:(t + 1) * n_seq, :].T

    def make_body(hh):
        lo = hh * hw

        def body(vi, carry):
            s = s0_ref[hh, vi]
            for t in range(n_tok):
                v_row = tr_ref[2, t, pl.ds(lo + vi, 1), :]
                sa = -jnp.sum(s * tr_ref[3, t, lo:lo + hw, :], axis=0, keepdims=True)
                s = (s * tr_ref[0, t, lo:lo + hw, :] + sa * tr_ref[4, t, lo:lo + hw, :]
                     + v_row * tr_ref[1, t, lo:lo + hw, :])
                y_ref[t, pl.ds(lo + vi, 1), :] = jnp.sum(
                    s * tr_ref[5, t, lo:lo + hw, :], axis=0, keepdims=True)
            _store_state(so_ref, fill, (hh, vi), s)
            return carry

        return body

    for hh in range(s0_ref.shape[0]):
        lax.fori_loop(0, hw, make_body(hh), 0, unroll=4)

    inv_n = 1.0 / hw
    for t in range(n_tok):
        rows = slice(t * n_seq, (t + 1) * n_seq)
        y = y_ref[t].T
        sum_y, sum_yy = _segsum_many([y, y * y], ones_blk)
        mean = sum_y * inv_n
        var = sum_yy * inv_n - mean * mean
        yn = (y - mean) * lax.rsqrt(var + RWKV_GN_EPS) * ln_w + ln_b
        z = z_ref[rows, :]
        g_ref[rows, :] = ((yn + bonus_s[rows] * v[rows]) * (z * _sigmoid(z))).astype(BF16)


def _rwkv_step(proj, vf_src, wm, am, vm, w2p, a2p, v2p, prm, s0, s_out_prev, *, layer_j, n_tok,
               has_vres):
    m = proj.shape[0]
    d = proj.shape[1] // 4
    pair = 2 * RWKV_HEAD
    n_pairs = d // pair
    n_seq = s0.shape[-1]

    def col(off):
        return pl.BlockSpec((m, pair), lambda hp: (0, off * n_pairs + hp))

    lora = pl.BlockSpec((m, LORA_PAD), lambda hp: (0, 0))
    up = pl.BlockSpec((LORA_PAD, pair), lambda hp: (0, hp))
    blk = (2, RWKV_HEAD, RWKV_HEAD, n_seq)
    state = pl.BlockSpec((None,) + blk, lambda hp: (layer_j, hp, 0, 0, 0))
    prev = [] if s_out_prev is None else [s_out_prev]
    fill = None if prev else (layer_j, s0.shape[0])
    state_out = state if prev else pl.BlockSpec((s0.shape[0],) + blk, lambda hp: (0, hp, 0, 0, 0))
    ones_blk = jnp.where(
        lax.shift_right_logical(_iota((pair, pair), 0), 6)
        == lax.shift_right_logical(_iota((pair, pair), 1), 6), 1.0, 0.0).astype(BF16)
    return pl.pallas_call(
        functools.partial(_rwkv_step_kernel, n_tok, has_vres, fill),
        grid=(n_pairs,),
        in_specs=[col(0), col(1), col(2), col(3), lora, lora, lora, col(2), up, up, up,
                  pl.BlockSpec((8, pair), lambda hp: (0, hp)), state,
                  pl.BlockSpec((pair, pair), lambda hp: (0, 0))]
        + [pl.BlockSpec(memory_space=pl.ANY)] * len(prev),
        out_specs=[pl.BlockSpec((m, pair), lambda hp: (0, hp)), state_out],
        out_shape=[jax.ShapeDtypeStruct((m, d), BF16), jax.ShapeDtypeStruct(s0.shape, F32)],
        scratch_shapes=[pltpu.VMEM((6, n_tok, pair, n_seq), F32),
                        pltpu.VMEM((n_tok, pair, n_seq), F32)],
        input_output_aliases={14: 1} if prev else {},
        compiler_params=pltpu.CompilerParams(
            dimension_semantics=("arbitrary",), vmem_limit_bytes=VMEM_LIMIT),
        name="rwkv_step",
    )(proj, proj, proj, proj, wm, am, vm, vf_src, w2p, a2p, v2p, prm, s0, ones_blk, *prev)


def _hgrn_head(chunk, tvalid, rows_sl, sl, head, refs, lb, nw, ones_blk):
    (q_ref, f_ref, i_ref, z_ref, g_ref, st_ref) = refs
    sub = min(HGRN_SUB, chunk)
    shift = int(math.log2(sub))
    fl = f_ref[rows_sl, sl]
    qraw = q_ref[rows_sl, sl]
    val = i_ref[rows_sl, sl]
    f = lb + (1.0 - lb) * _sigmoid(fl)
    logf = jnp.log(jnp.maximum(f, GATE_FLOOR)) * LOG2_E
    if tvalid < chunk:
        logf = jnp.where(_iota(logf.shape, 0) < tvalid, logf, 0.0)
    kg = (1.0 - lb) * _sigmoid(-fl)
    qs = qraw * _sigmoid(qraw) * (HGRN_DK ** -0.5)
    gcum = _cumsum_rows(logf)
    yield

    st = st_ref[head]
    o_inter = _mm_nt((qs * jnp.exp2(gcum)).astype(BF16), st.astype(BF16))
    g_last = gcum[chunk - 1:chunk, :]
    k_tail = kg * jnp.exp2(g_last - gcum)
    st_ref[head] = st * jnp.exp2(g_last) + _mm_tn(val.astype(BF16), k_tail.astype(BF16))

    def bcast_rows(x):
        return jnp.concatenate(
            [jnp.broadcast_to(x[t:t + 1, :], (sub, HGRN_DK)) for t in range(tvalid)], axis=0)

    def tile_blocks(x):
        return jnp.concatenate(
            [x[(t >> shift) * sub:((t >> shift) + 1) * sub] for t in range(tvalid)], axis=0)

    rows = tvalid * sub
    prod = bcast_rows(qs) * tile_blocks(kg) * jnp.exp2(
        jnp.minimum(bcast_rows(gcum) - tile_blocks(gcum), 0.0))
    att_diag = _mm(prod.astype(BF16), ones_blk)
    n_blk = -(-tvalid // sub)
    att_off = []
    for bi in range(1, n_blk):
        lo = bi * sub
        g_edge = gcum[lo - 1:lo]
        q_n = (qs[lo:lo + sub] * jnp.exp2(gcum[lo:lo + sub] - g_edge)).astype(BF16)
        k_n = (kg[0:lo] * jnp.exp2(g_edge - gcum[0:lo])).astype(BF16)
        att_off.append(_mm_nt(q_n, k_n))
    yield

    ridx = _iota((rows, HGRN_DK), 0)
    causal = (jnp.bitwise_and(ridx, sub - 1)
              <= jnp.bitwise_and(lax.shift_right_logical(ridx, shift), sub - 1))
    weighted = jnp.where(causal, att_diag * tile_blocks(val), 0.0)
    sel = jnp.where(lax.shift_right_logical(_iota((chunk, rows), 1), shift) == _iota((chunk, rows), 0),
                    1.0, 0.0).astype(BF16)
    o = o_inter + _mm(sel, weighted.astype(BF16))
    if n_blk > 1:
        parts = [jnp.zeros((sub, HGRN_DK), F32)]
        for bi in range(1, n_blk):
            parts.append(_mm(att_off[bi - 1].astype(BF16), val[0:bi * sub].astype(BF16)))
        if n_blk * sub < chunk:
            parts.append(jnp.zeros((chunk - n_blk * sub, HGRN_DK), F32))
        o = o + jnp.concatenate(parts, axis=0)
    yield

    o = o * lax.rsqrt(jnp.mean(o * o, axis=-1, keepdims=True) + NORM_EPS) * nw
    z = z_ref[rows_sl, sl]
    g_ref[rows_sl, sl] = (o * (z * _sigmoid(z))).astype(BF16)


def _hgrn_chunk_kernel(chunk, tvalid, layer_j, seqs, heads, fill,
                       q_ref, f_ref, i_ref, z_ref, lbl_ref, nw_ref, s0_ref, *rest):
    g_ref, so_ref, st_ref = rest[-3:]
    c = pl.program_id(2)
    n_chunks = pl.num_programs(2)

    @pl.when(c == 0)
    def _init():
        for q in range(seqs):
            for hh in range(heads):
                st_ref[q * heads + hh] = s0_ref[q, hh].T

    logits = lbl_ref[...]
    ex = jnp.exp(logits - jnp.max(logits, axis=0, keepdims=True))
    soft = ex / jnp.sum(ex, axis=0, keepdims=True)
    lb_all = jnp.sum(soft[0:layer_j + 1], axis=0, keepdims=True) - soft[0:1]

    ones_blk = jnp.ones((HGRN_DK, HGRN_DK), BF16)
    nw = nw_ref[...]
    refs = (q_ref, f_ref, i_ref, z_ref, g_ref, st_ref)
    _round_robin([
        _hgrn_head(chunk, tvalid, slice(q * chunk, (q + 1) * chunk),
                   slice(hh * HGRN_DK, (hh + 1) * HGRN_DK), q * heads + hh, refs,
                   lb_all[:, hh * HGRN_DK:(hh + 1) * HGRN_DK], nw, ones_blk)
        for q in range(seqs) for hh in range(heads)])

    @pl.when(c == n_chunks - 1)
    def _fin():
        for q in range(seqs):
            for hh in range(heads):
                _store_state(so_ref, fill, (q, hh), st_ref[q * heads + hh].T)


def _hgrn_chunk(proj, lb_logits, norm_w, s0, s_out_prev, *, layer_j, n_seq, n_chunks, chunk, tvalid,
                seqs=1):
    assert seqs == 1 or n_chunks == 1
    m = n_seq * n_chunks * chunk
    d = proj.shape[1] // 4
    n_heads = d // HGRN_DK
    heads = HGRN_HEADS_PER_STEP
    n_hb = n_heads // heads
    width = heads * HGRN_DK

    def col(off):
        return pl.BlockSpec((seqs * chunk, width), lambda s, h, c: (s * n_chunks + c, off * n_hb + h))

    state = pl.BlockSpec((None, seqs, heads, HGRN_DK, HGRN_DK), lambda s, h, c: (layer_j, s, h, 0, 0))
    prev = [] if s_out_prev is None else [s_out_prev]
    fill = None if prev else (layer_j, s0.shape[0])
    state_out = state if prev else pl.BlockSpec(
        (s0.shape[0], seqs, heads, HGRN_DK, HGRN_DK), lambda s, h, c: (0, s, h, 0, 0))
    kernel = functools.partial(_hgrn_chunk_kernel, chunk, tvalid, layer_j, seqs, heads, fill)
    return pl.pallas_call(
        kernel,
        grid=(n_seq // seqs, n_hb, n_chunks),
        in_specs=[col(0), col(1), col(2), col(3),
                  pl.BlockSpec((lb_logits.shape[0], width), lambda s, h, c: (0, h)),
                  pl.BlockSpec((1, HGRN_DK), lambda s, h, c: (0, 0)), state]
        + [pl.BlockSpec(memory_space=pl.ANY)] * len(prev),
        out_specs=[pl.BlockSpec((seqs * chunk, width), lambda s, h, c: (s * n_chunks + c, h)),
                   state_out],
        out_shape=[jax.ShapeDtypeStruct((m, d), BF16),
                   jax.ShapeDtypeStruct(s0.shape, F32)],
        scratch_shapes=[pltpu.VMEM((seqs * heads, HGRN_DK, HGRN_DK), F32)],
        input_output_aliases={7: 1} if prev else {},
        compiler_params=pltpu.CompilerParams(
            dimension_semantics=("arbitrary", "arbitrary", "arbitrary"),
            vmem_limit_bytes=VMEM_LIMIT),
        name="hgrn_chunk",
    )(proj, proj, proj, proj, lb_logits, norm_w.reshape(1, HGRN_DK), s0, *prev)


def _pad_lanes(w, axis):
    pad = [(0, 0)] * w.ndim
    pad[axis] = (0, LORA_PAD - w.shape[axis])
    return jnp.pad(w, pad)


def _rwkv_layer_inputs(h, hprev, p, j, tiles_per_seq):
    has_vres = j > 0
    proj = _proj_rwkv(h, hprev, p["mu4"][j], p["w_in"], j, tiles_per_seq)
    wm, am, vm = _lora(h, hprev, p["mu6"][j], p["w1"][j], p["a1"][j], p["v1"][max(j - 1, 0)],
                       has_vres, tiles_per_seq)
    return proj, wm, am, vm, has_vres


def _trunk_prompt(x, shift0, wkv0, hgrn0, p, *, n_seq, n_tok):
    d = x.shape[1]
    n_chunks = n_tok // PROMPT_CHUNK
    shifts = []
    wkv_out, hgrn_out, vf_src = None, None, None
    h = _rmsnorm(x, p["norm_pre"][0])
    for layer in range(DEPTH):
        j = layer // 2
        npre_next = p["norm_pre"][layer + 1] if layer + 1 < DEPTH else None
        if layer % 2 == 0:
            shifts.append(h[n_tok - 1::n_tok])
            proj, wm, am, vm, has_vres = _rwkv_layer_inputs(
                h, shift0[j][:, None, :], p, j, n_tok // ROW_TILE)
            vf_src = proj if vf_src is None else vf_src
            g, wkv_out = _rwkv_chunk(
                proj, vf_src, wm, am, vm, p["w2"][j], p["a2"][j], p["v2"][max(j - 1, 0)],
                p["rwkv_prm"][j], wkv0, wkv_out, layer_j=j, n_seq=n_seq, n_chunks=n_chunks,
                chunk=PROMPT_CHUNK, tvalid=PROMPT_CHUNK, has_vres=has_vres)
            w_o = p["rwkv_w_o"]
        else:
            proj = _proj_hgrn(h, p["hgrn_w_in"], j)
            g, hgrn_out = _hgrn_chunk(
                proj, p["hgrn_lb_logits"], p["hgrn_norm_w"][j], hgrn0, hgrn_out, layer_j=j,
                n_seq=n_seq, n_chunks=n_chunks, chunk=PROMPT_CHUNK, tvalid=PROMPT_CHUNK)
            w_o = p["hgrn_w_o"]
        x, h = _outproj(g, w_o, j, x, p["norm_post"][layer], npre_next,
                        BF16 if layer % 2 == 0 else F32)
    return x, jnp.stack(shifts), wkv_out, hgrn_out


def _trunk_sample(x, shift0, wkv0, hgrn0, p, *, n_seq, n_tok):
    d = x.shape[1]
    tpad = SAMPLE_TPAD

    def to_padded(a):
        a3 = a.reshape(n_tok, n_seq, a.shape[1]).transpose(1, 0, 2)
        return jnp.pad(a3, ((0, 0), (0, tpad - n_tok), (0, 0))).reshape(n_seq * tpad, a.shape[1])

    def from_padded(a):
        a3 = a.reshape(n_seq, tpad, a.shape[1])[:, :n_tok]
        return a3.transpose(1, 0, 2).reshape(n_tok * n_seq, a.shape[1])

    shifts = []
    wkv_out, hgrn_out, vf_src = None, None, None
    h = _rmsnorm(x, p["norm_pre"][0])
    for layer in range(DEPTH):
        j = layer // 2
        npre_next = p["norm_pre"][layer + 1] if layer + 1 < DEPTH else None
        if layer % 2 == 0:
            hprev = jnp.concatenate([shift0[j], h[:(n_tok - 1) * n_seq]], axis=0)
            shifts.append(h[(n_tok - 1) * n_seq:])
            proj, wm, am, vm, has_vres = _rwkv_layer_inputs(h, hprev, p, j, 0)
            vf_src = proj if vf_src is None else vf_src
            g, wkv_out = _rwkv_step(
                proj, vf_src, wm, am, vm, p["w2"][j], p["a2"][j], p["v2"][max(j - 1, 0)],
                p["rwkv_prm"][j], wkv0, wkv_out, layer_j=j, n_tok=n_tok, has_vres=has_vres)
            w_o = p["rwkv_w_o"]
        else:
            proj = to_padded(_proj_hgrn(h, p["hgrn_w_in"], j))
            g, hgrn_out = _hgrn_chunk(
                proj, p["hgrn_lb_logits"], p["hgrn_norm_w"][j], hgrn0, hgrn_out, layer_j=j,
                n_seq=n_seq, n_chunks=1, chunk=tpad, tvalid=n_tok, seqs=HGRN_SAMPLE_SEQS)
            g = from_padded(g)
            w_o = p["hgrn_w_o"]
        x, h = _outproj(g, w_o, j, x, p["norm_post"][layer], npre_next,
                        BF16 if layer % 2 == 0 else F32)
    return x, jnp.stack(shifts), wkv_out, hgrn_out


def kernel(x_prompt, x_sample, state_rwkv_shift, state_rwkv_wkv, state_hgrn, norm_pre, norm_post,
           rwkv_mu, rwkv_w_in, rwkv_w0, rwkv_w1, rwkv_w2, rwkv_a0, rwkv_a1, rwkv_a2, rwkv_v0,
           rwkv_v1, rwkv_v2, rwkv_k_k, rwkv_k_a, rwkv_r_k, rwkv_ln_w, rwkv_ln_b, rwkv_w_o,
           hgrn_w_in, hgrn_lb_logits, hgrn_norm_w, hgrn_w_o):
    n_rwkv = rwkv_mu.shape[0]
    d = x_prompt.shape[-1]
    v0_full = jnp.concatenate([jnp.zeros((1, d), F32), rwkv_v0], axis=0)
    rwkv_prm = jnp.stack([rwkv_w0, rwkv_a0, v0_full, rwkv_k_k, rwkv_k_a,
                          rwkv_r_k.reshape(n_rwkv, d), rwkv_ln_w, rwkv_ln_b], axis=1)
    p = {
        "norm_pre": norm_pre, "norm_post": norm_post,
        "mu4": rwkv_mu[:, :4, None, :], "mu6": rwkv_mu,
        "w_in": rwkv_w_in.astype(BF16),
        "w1": _pad_lanes(rwkv_w1, 2).astype(BF16), "a1": _pad_lanes(rwkv_a1, 2).astype(BF16),
        "v1": _pad_lanes(rwkv_v1, 2).astype(BF16),
        "w2": _pad_lanes(rwkv_w2, 1).astype(BF16), "a2": _pad_lanes(rwkv_a2, 1).astype(BF16),
        "v2": _pad_lanes(rwkv_v2, 1).astype(BF16),
        "rwkv_prm": rwkv_prm, "rwkv_w_o": rwkv_w_o.astype(BF16),
        "hgrn_w_in": hgrn_w_in.astype(BF16), "hgrn_lb_logits": hgrn_lb_logits,
        "hgrn_norm_w": hgrn_norm_w, "hgrn_w_o": hgrn_w_o.astype(BF16),
    }

    bp, tp, _ = x_prompt.shape
    zero_shift = jnp.zeros((n_rwkv, bp, d), F32)
    zero_wkv = jnp.zeros((n_rwkv, bp) + state_rwkv_wkv.shape[2:], F32)
    zero_hgrn = jnp.zeros((state_hgrn.shape[0], bp) + state_hgrn.shape[2:], F32)
    y_p, p_shift, p_wkv, p_hgrn = _trunk_prompt(
        x_prompt.reshape(bp * tp, d), zero_shift, zero_wkv, zero_hgrn, p, n_seq=bp, n_tok=tp)

    bs, ts, _ = x_sample.shape
    y_s, s_shift, s_wkv, s_hgrn = _trunk_sample(
        x_sample.transpose(1, 0, 2).reshape(ts * bs, d), state_rwkv_shift,
        state_rwkv_wkv.transpose(0, 2, 3, 4, 1), state_hgrn, p, n_seq=bs, n_tok=ts)

    return (y_p.reshape(bp, tp, d), y_s.reshape(ts, bs, d).transpose(1, 0, 2),
            p_shift, p_wkv, p_hgrn, s_shift, s_wkv.transpose(0, 4, 1, 2, 3), s_hgrn)
```

```python
import functools
import math

import jax
import jax.numpy as jnp
from jax import lax
from jax.experimental import pallas as pl
from jax.experimental.pallas import tpu as pltpu

F32 = jnp.float32
BF16 = jnp.bfloat16

D_MODEL = 2048
DEPTH = 4
RWKV_HEAD = 64
HGRN_DK = 128
RWKV_GN_EPS = 1e-5 * RWKV_HEAD
NORM_EPS = 1e-6
GATE_FLOOR = 1e-30
LOG2_E = 1.0 / math.log(2.0)

LANES = 128
GROUP = 256
HEADS_PER_GROUP = GROUP // RWKV_HEAD
LORA_PAD = 128
PROMPT_CHUNK = 64
SAMPLE_TPAD = 8
HGRN_SUB = 8
HGRN_HEADS_PER_STEP = 16
HGRN_SAMPLE_SEQS = 4
ROW_TILE = 512
VMEM_LIMIT = 48 * 1024 * 1024


def _mm(a, b):
    return jnp.dot(a, b, preferred_element_type=F32)


def _mm_nt(a, b):
    return lax.dot_general(a, b, (((1,), (1,)), ((), ())), preferred_element_type=F32)


def _mm_tn(a, b):
    return lax.dot_general(a, b, (((0,), (0,)), ((), ())), preferred_element_type=F32)


def _iota(shape, dim):
    return lax.broadcasted_iota(jnp.int32, shape, dim)


def _split2(x):
    hi = x.astype(BF16)
    lo = (x - hi.astype(F32)).astype(BF16)
    return hi, lo


def _segsum(x, ones_blk):
    hi, lo = _split2(x)
    return _mm(hi, ones_blk) + _mm(lo, ones_blk)


def _segsum_many(xs, ones_blk):
    pieces = []
    for x in xs:
        pieces.extend(_split2(x))
    out = _mm(jnp.concatenate(pieces, axis=0), ones_blk)
    n = xs[0].shape[0]
    return [out[2 * i * n:(2 * i + 1) * n] + out[(2 * i + 1) * n:(2 * i + 2) * n] for i in range(len(xs))]


def _cumsum_rows(x):
    n = x.shape[0]
    tri = jnp.where(_iota((n, n), 0) >= _iota((n, n), 1), 1.0, 0.0).astype(BF16)
    x1 = x.astype(BF16)
    r1 = x - x1.astype(F32)
    x2 = r1.astype(BF16)
    x3 = (r1 - x2.astype(F32)).astype(BF16)
    return _mm(tri, x1) + _mm(tri, x2) + _mm(tri, x3)


def _block_diag(x, head_width, n_heads):
    shift = int(math.log2(head_width))
    lane_head = lax.shift_right_logical(_iota(x.shape, 1), shift)
    parts = [jnp.where(lane_head == h, x, 0.0).astype(BF16) for h in range(n_heads)]
    return jnp.concatenate(parts, axis=0)


def _sigmoid(x):
    return jax.nn.sigmoid(x)


def _log_decay(wl):
    return _sigmoid(wl) * (-math.exp(-0.5))


def _store_state(so_ref, fill, idx, value):
    if fill is None:
        so_ref[idx] = value
        return
    layer, n_layers = fill
    for other in range(n_layers):
        so_ref[(other,) + idx] = value if other == layer else jnp.zeros_like(value)


def _round_robin(gens):
    results = [None] * len(gens)
    active = list(range(len(gens)))
    while active:
        for idx in list(active):
            try:
                next(gens[idx])
            except StopIteration as stop:
                results[idx] = stop.value
                active.remove(idx)
    return results


def _rmsnorm_kernel(x_ref, w_ref, o_ref):
    x = x_ref[...]
    ms = jnp.mean(x * x, axis=-1, keepdims=True)
    o_ref[...] = x * lax.rsqrt(ms + NORM_EPS) * w_ref[...]


def _rmsnorm(x, w):
    m, d = x.shape
    return pl.pallas_call(
        _rmsnorm_kernel,
        grid=(m // ROW_TILE,),
        in_specs=[pl.BlockSpec((ROW_TILE, d), lambda i: (i, 0)),
                  pl.BlockSpec((1, d), lambda i: (0, 0))],
        out_specs=pl.BlockSpec((ROW_TILE, d), lambda i: (i, 0)),
        out_shape=jax.ShapeDtypeStruct((m, d), F32),
        name="rmsnorm",
    )(x, w.reshape(1, d))


def _prev_rows(h, first_ref, carry_ref, tile, tiles_per_seq):
    rolled = pltpu.roll(h, 1, axis=0)
    seq_start = lax.rem(tile, tiles_per_seq) == 0
    edge = jnp.where(seq_start, first_ref[...], carry_ref[7:8, :])
    carry_ref[...] = h[h.shape[0] - 8:, :]
    return jnp.where(_iota(h.shape, 0) == 0, edge, rolled)


def _proj_mix_kernel(tiles_per_seq, h_ref, hp_ref, mu_ref, w_ref, o_ref, *carry):
    h = h_ref[...]
    if tiles_per_seq:
        hprev = _prev_rows(h, hp_ref, carry[0], pl.program_id(1), tiles_per_seq)
    else:
        hprev = hp_ref[...]
    xs = h + (hprev - h) * mu_ref[...]
    o_ref[...] = _mm(xs.astype(BF16), w_ref[...])


def _proj_plain_kernel(h_ref, w_ref, o_ref):
    o_ref[...] = _mm(h_ref[...].astype(BF16), w_ref[...])


def _prev_spec(hprev, tiles_per_seq, d, grid_rank):
    if tiles_per_seq:
        if grid_rank == 2:
            return pl.BlockSpec((None, 1, d), lambda n, i: (i // tiles_per_seq, 0, 0))
        return pl.BlockSpec((None, 1, d), lambda i: (i // tiles_per_seq, 0, 0))
    if grid_rank == 2:
        return pl.BlockSpec((ROW_TILE, d), lambda n, i: (i, 0))
    return pl.BlockSpec((ROW_TILE, d), lambda i: (i, 0))


def _proj_rwkv(h, hprev, mu4, w4, layer_j, tiles_per_seq):
    m, d = h.shape
    n_proj = w4.shape[1]
    return pl.pallas_call(
        functools.partial(_proj_mix_kernel, tiles_per_seq),
        grid=(n_proj, m // ROW_TILE),
        in_specs=[pl.BlockSpec((ROW_TILE, d), lambda n, i: (i, 0)),
                  _prev_spec(hprev, tiles_per_seq, d, 2),
                  pl.BlockSpec((None, 1, d), lambda n, i: (n, 0, 0)),
                  pl.BlockSpec((None, None, d, d), lambda n, i: (layer_j, n, 0, 0))],
        out_specs=pl.BlockSpec((ROW_TILE, d), lambda n, i: (i, n)),
        out_shape=jax.ShapeDtypeStruct((m, n_proj * d), F32),
        scratch_shapes=[pltpu.VMEM((8, d), F32)] if tiles_per_seq else [],
        compiler_params=pltpu.CompilerParams(
            dimension_semantics=("arbitrary", "arbitrary"), vmem_limit_bytes=VMEM_LIMIT),
        name="proj_rwkv",
    )(h, hprev, mu4, w4)


def _proj_hgrn(h, w, layer_j):
    m, d = h.shape
    n_proj = w.shape[2] // d
    return pl.pallas_call(
        _proj_plain_kernel,
        grid=(n_proj, m // ROW_TILE),
        in_specs=[pl.BlockSpec((ROW_TILE, d), lambda n, i: (i, 0)),
                  pl.BlockSpec((None, d, d), lambda n, i: (layer_j, 0, n))],
        out_specs=pl.BlockSpec((ROW_TILE, d), lambda n, i: (i, n)),
        out_shape=jax.ShapeDtypeStruct((m, n_proj * d), F32),
        compiler_params=pltpu.CompilerParams(
            dimension_semantics=("arbitrary", "arbitrary"), vmem_limit_bytes=VMEM_LIMIT),
        name="proj_hgrn",
    )(h, w)


def _lora_kernel(has_v, tiles_per_seq, h_ref, hp_ref, mu_ref, w1_ref, a1_ref, v1_ref,
                 wm_ref, am_ref, vm_ref, *carry):
    h = h_ref[...]
    if tiles_per_seq:
        hprev = _prev_rows(h, hp_ref, carry[0], pl.program_id(0), tiles_per_seq)
    else:
        hprev = hp_ref[...]
    delta = hprev - h
    xw = h + delta * mu_ref[4:5, :]
    xa = h + delta * mu_ref[5:6, :]
    wm_ref[...] = jnp.tanh(_mm(xw.astype(BF16), w1_ref[...]))
    am_ref[...] = _mm(xa.astype(BF16), a1_ref[...])
    if has_v:
        xv = h + delta * mu_ref[2:3, :]
        vm_ref[...] = _mm(xv.astype(BF16), v1_ref[...])
    else:
        vm_ref[...] = jnp.zeros(vm_ref.shape, F32)


def _lora(h, hprev, mu6, w1p, a1p, v1p, has_v, tiles_per_seq):
    m, d = h.shape
    row = pl.BlockSpec((ROW_TILE, d), lambda i: (i, 0))
    wspec = pl.BlockSpec((d, LORA_PAD), lambda i: (0, 0))
    ospec = pl.BlockSpec((ROW_TILE, LORA_PAD), lambda i: (i, 0))
    oshape = jax.ShapeDtypeStruct((m, LORA_PAD), F32)
    return pl.pallas_call(
        functools.partial(_lora_kernel, has_v, tiles_per_seq),
        grid=(m // ROW_TILE,),
        in_specs=[row, _prev_spec(hprev, tiles_per_seq, d, 1),
                  pl.BlockSpec(mu6.shape, lambda i: (0, 0)), wspec, wspec, wspec],
        out_specs=[ospec, ospec, ospec],
        out_shape=[oshape, oshape, oshape],
        scratch_shapes=[pltpu.VMEM((8, d), F32)] if tiles_per_seq else [],
        compiler_params=pltpu.CompilerParams(dimension_semantics=("arbitrary",)),
        name="lora",
    )(h, hprev, mu6, w1p, a1p, v1p)


def _outproj_kernel(g_ref, w_ref, x_ref, npost_ref, npre_ref, xo_ref, *ho_ref):
    half = g_ref.shape[0] // 2
    w = w_ref[...]
    outs = [_mm(g_ref[rows, :], w) for rows in (slice(0, half), slice(half, 2 * half))]
    for rows, out in zip((slice(0, half), slice(half, 2 * half)), outs):
        ms = jnp.mean(out * out, axis=-1, keepdims=True)
        xn = x_ref[rows, :] + out * lax.rsqrt(ms + NORM_EPS) * npost_ref[...]
        xo_ref[rows, :] = xn
        if ho_ref:
            ms2 = jnp.mean(xn * xn, axis=-1, keepdims=True)
            ho_ref[0][rows, :] = (xn * lax.rsqrt(ms2 + NORM_EPS) * npre_ref[...]).astype(ho_ref[0].dtype)


def _outproj(g, w, layer_j, x, npost, npre_next, h_dtype):
    m, d = x.shape
    n_out = 1 if npre_next is None else 2
    if npre_next is None:
        npre_next = npost
    row = pl.BlockSpec((ROW_TILE, d), lambda i: (i, 0))
    vec = pl.BlockSpec((1, d), lambda i: (0, 0))
    out = pl.pallas_call(
        _outproj_kernel,
        grid=(m // ROW_TILE,),
        in_specs=[row, pl.BlockSpec((None, d, d), lambda i: (layer_j, 0, 0)), row, vec, vec],
        out_specs=[row, row][:n_out],
        out_shape=[jax.ShapeDtypeStruct((m, d), F32), jax.ShapeDtypeStruct((m, d), h_dtype)][:n_out],
        compiler_params=pltpu.CompilerParams(
            dimension_semantics=("arbitrary",), vmem_limit_bytes=VMEM_LIMIT),
        name="outproj",
    )(g, w, x, npost.reshape(1, d), npre_next.reshape(1, d))
    return (out[0], None) if n_out == 1 else out


def _rwkv_group(chunk, tvalid, has_vres, n_double, sl, grp, refs, wm, am, vm, ones_blk, diag):
    (r_ref, k_ref, v_ref, z_ref, vf_ref, w2_ref, a2_ref, v2_ref, prm_ref, g_ref, sbd_ref) = refs
    hw = RWKV_HEAD
    nh = HEADS_PER_GROUP
    prm = prm_ref[:, sl]
    w0, a0, v0 = prm[0:1], prm[1:2], prm[2:3]
    kk_w, ka_w, rk_w, ln_w, ln_b = prm[3:4], prm[4:5], prm[5:6], prm[6:7], prm[7:8]
    r = r_ref[:, sl]
    k = k_ref[:, sl]
    v = v_ref[:, sl]

    wl = w0 + _mm(wm, w2_ref[:, sl])
    al = a0 + _mm(am, a2_ref[:, sl])
    if has_vres:
        vl = v0 + _mm(vm, v2_ref[:, sl])
    yield
    logw = _log_decay(wl)
    if tvalid < chunk:
        logw = jnp.where(_iota(logw.shape, 0) < tvalid, logw, 0.0)
    alpha = _sigmoid(al)
    if has_vres:
        v = v + (vf_ref[:, sl] - v) * _sigmoid(vl)
    kk = k * kk_w
    k2 = k * (1.0 + (alpha - 1.0) * ka_w)
    kk_n2, bonus_s = _segsum_many([kk * kk, r * k2 * rk_w], ones_blk)
    cum = _cumsum_rows(logw)
    yield
    kk = kk / jnp.maximum(jnp.sqrt(kk_n2), 1e-12)
    b = kk * alpha
    a = -kk
    e_neg = jnp.exp(-cum)
    a_hat = a * jnp.exp(cum - logw)
    b_hat = b * e_neg
    k_hat = k2 * e_neg
    r_hat = r * jnp.exp(cum)
    cum_last = cum[chunk - 1:chunk, :]
    e_tail = jnp.exp(cum_last - cum)
    b_tail = b * e_tail
    k_tail = k2 * e_tail

    lhs = jnp.concatenate([a_hat, r_hat], axis=0).astype(BF16)
    gram_b = _mm_nt(lhs, _block_diag(b_hat, hw, nh))
    gram_k = _mm_nt(lhs, _block_diag(k_hat, hw, nh))
    s_bd = sbd_ref[grp]
    xy_state = _mm_nt(lhs, s_bd.astype(BF16))
    yield
    shape_cc = (chunk, nh * chunk)
    t_idx = _iota(shape_cc, 0)
    i_idx = jnp.bitwise_and(_iota(shape_cc, 1), chunk - 1)
    strict = i_idx < t_idx
    incl = i_idx <= t_idx
    n_ab = jnp.where(strict, gram_b[:chunk], 0.0)
    a_ak = jnp.where(strict, gram_k[:chunk], 0.0)
    a_rb = jnp.where(incl, gram_b[chunk:], 0.0)
    a_rk = jnp.where(incl, gram_k[chunk:], 0.0)
    v_bd = _block_diag(v, hw, nh)

    xy = xy_state + _mm(jnp.concatenate([a_ak, a_rk], axis=0).astype(BF16), v_bd)
    x = xy[:chunk]
    y_v = xy[chunk:]
    p = n_ab
    if n_double > 1:
        p_next = _mm(p.astype(BF16), _block_diag(p, chunk, nh))
    yield
    for j in range(n_double):
        x = x + _mm(p.astype(BF16), _block_diag(x, hw, nh))
        if j + 1 < n_double:
            p = p_next
            if j + 2 < n_double:
                p_next = _mm(p.astype(BF16), _block_diag(p, chunk, nh))
        yield
    u = x

    y = y_v + _mm(a_rb.astype(BF16), _block_diag(u, hw, nh))
    uv = jnp.concatenate([u, v], axis=0).astype(BF16)
    bk = jnp.concatenate([b_tail, k_tail], axis=0).astype(BF16)
    sbd_ref[grp] = s_bd * jnp.exp(cum_last) + jnp.where(diag, _mm_tn(uv, bk), 0.0)
    yield

    inv_n = 1.0 / RWKV_HEAD
    sum_y, sum_yy = _segsum_many([y, y * y], ones_blk)
    yield
    mean = sum_y * inv_n
    var = sum_yy * inv_n - mean * mean
    yn = (y - mean) * lax.rsqrt(var + RWKV_GN_EPS) * ln_w + ln_b
    z = z_ref[:, sl]
    g_ref[:, sl] = ((yn + bonus_s * v) * (z * _sigmoid(z))).astype(BF16)


def _rwkv_chunk_kernel(chunk, tvalid, has_vres, n_double, n_groups, fill,
                       r_ref, k_ref, v_ref, z_ref, wm_ref, am_ref, vm_ref, vf_ref,
                       w2_ref, a2_ref, v2_ref, prm_ref, s0_ref, ones_ref, *rest):
    g_ref, so_ref, sbd_ref = rest[-3:]
    c = pl.program_id(1)
    n_chunks = pl.num_programs(1)
    hw = RWKV_HEAD
    diag = (lax.shift_right_logical(_iota((GROUP, GROUP), 0), 6)
            == lax.shift_right_logical(_iota((GROUP, GROUP), 1), 6))

    @pl.when(c == 0)
    def _init():
        for g in range(n_groups):
            s4 = s0_ref[g * HEADS_PER_GROUP:(g + 1) * HEADS_PER_GROUP].reshape(GROUP, hw)
            tiled = jnp.concatenate([s4] * HEADS_PER_GROUP, axis=1)
            sbd_ref[g] = jnp.where(diag, tiled, 0.0)

    ones_blk = ones_ref[...]
    wm = wm_ref[...].astype(BF16)
    am = am_ref[...].astype(BF16)
    vm = vm_ref[...].astype(BF16)
    refs = (r_ref, k_ref, v_ref, z_ref, vf_ref, w2_ref, a2_ref, v2_ref, prm_ref, g_ref, sbd_ref)
    _round_robin([
        _rwkv_group(chunk, tvalid, has_vres, n_double, slice(g * GROUP, (g + 1) * GROUP), g, refs,
                    wm, am, vm, ones_blk, diag)
        for g in range(n_groups)])

    @pl.when(c == n_chunks - 1)
    def _fin():
        for g in range(n_groups):
            s_new = sbd_ref[g]
            dense = (s_new[:, 0:hw] + s_new[:, hw:2 * hw]
                     + s_new[:, 2 * hw:3 * hw] + s_new[:, 3 * hw:4 * hw])
            _store_state(so_ref, fill, (slice(g * HEADS_PER_GROUP, (g + 1) * HEADS_PER_GROUP),),
                         dense.reshape(HEADS_PER_GROUP, hw, hw))


def _rwkv_chunk(proj, vf_src, wm, am, vm, w2p, a2p, v2p, prm, s0, s_out_prev, *, layer_j, n_seq,
                n_chunks, chunk, tvalid, has_vres):
    m = n_seq * n_chunks * chunk
    d = proj.shape[1] // 4
    n_groups = d // GROUP
    n_double = max(1, math.ceil(math.log2(tvalid)))

    def col(off):
        return pl.BlockSpec((chunk, d), lambda s, c: (s * n_chunks + c, off))

    lora = pl.BlockSpec((chunk, LORA_PAD), lambda s, c: (s * n_chunks + c, 0))
    up = pl.BlockSpec((LORA_PAD, d), lambda s, c: (0, 0))
    state = pl.BlockSpec((None, None) + s0.shape[2:], lambda s, c: (layer_j, s, 0, 0, 0))
    prev = [] if s_out_prev is None else [s_out_prev]
    fill = None if prev else (layer_j, s0.shape[0])
    state_out = state if prev else pl.BlockSpec(
        (s0.shape[0], None) + s0.shape[2:], lambda s, c: (0, s, 0, 0, 0))
    ones_blk = jnp.where(
        lax.shift_right_logical(_iota((GROUP, GROUP), 0), 6)
        == lax.shift_right_logical(_iota((GROUP, GROUP), 1), 6), 1.0, 0.0).astype(BF16)
    kernel = functools.partial(_rwkv_chunk_kernel, chunk, tvalid, has_vres, n_double, n_groups, fill)
    return pl.pallas_call(
        kernel,
        grid=(n_seq, n_chunks),
        in_specs=[col(0), col(1), col(2), col(3), lora, lora, lora, col(2), up, up, up,
                  pl.BlockSpec((8, d), lambda s, c: (0, 0)), state,
                  pl.BlockSpec((GROUP, GROUP), lambda s, c: (0, 0))]
        + [pl.BlockSpec(memory_space=pl.ANY)] * len(prev),
        out_specs=[pl.BlockSpec((chunk, d), lambda s, c: (s * n_chunks + c, 0)), state_out],
        out_shape=[jax.ShapeDtypeStruct((m, d), BF16),
                   jax.ShapeDtypeStruct(s0.shape, F32)],
        scratch_shapes=[pltpu.VMEM((n_groups, GROUP, GROUP), F32)],
        input_output_aliases={14: 1} if prev else {},
        compiler_params=pltpu.CompilerParams(
            dimension_semantics=("arbitrary", "arbitrary"), vmem_limit_bytes=VMEM_LIMIT),
        name="rwkv_chunk",
    )(proj, proj, proj, proj, wm, am, vm, vf_src, w2p, a2p, v2p, prm, s0, ones_blk, *prev)


def _rwkv_step_kernel(n_tok, has_vres, fill,
                      r_ref, k_ref, v_ref, z_ref, wm_ref, am_ref, vm_ref, vf_ref,
                      w2_ref, a2_ref, v2_ref, prm_ref, s0_ref, ones_ref, *rest):
    g_ref, so_ref, tr_ref, y_ref = rest[-4:]
    hw = RWKV_HEAD
    n_seq = s0_ref.shape[-1]
    ones_blk = ones_ref[...]
    prm = prm_ref[...]
    w0, a0, v0 = prm[0:1], prm[1:2], prm[2:3]
    kk_w, ka_w, rk_w, ln_w, ln_b = prm[3:4], prm[4:5], prm[5:6], prm[6:7], prm[7:8]
    r = r_ref[...]
    k = k_ref[...]
    v = v_ref[...]

    wl = w0 + _mm(wm_ref[...].astype(BF16), w2_ref[...])
    decay = jnp.exp(_log_decay(wl))
    alpha = _sigmoid(a0 + _mm(am_ref[...].astype(BF16), a2_ref[...]))
    if has_vres:
        v = v + (vf_ref[...] - v) * _sigmoid(v0 + _mm(vm_ref[...].astype(BF16), v2_ref[...]))
    kk = k * kk_w
    k2 = k * (1.0 + (alpha - 1.0) * ka_w)
    kk_n2, bonus_s = _segsum_many([kk * kk, r * k2 * rk_w], ones_blk)
    kk = kk / jnp.maximum(jnp.sqrt(kk_n2), 1e-12)
    b = kk * alpha

    for idx, arr in enumerate((decay, k2, v, kk, b, r)):
        for t in range(n_tok):
            tr_ref[idx, t] = arr[t * n_seq:(t + 1) * n_seq, :].T

    def make_body(hh):
        lo = hh * hw

        def body(vi, carry):
            s = s0_ref[hh, vi]
            for t in range(n_tok):
                v_row = tr_ref[2, t, pl.ds(lo + vi, 1), :]
                sa = -jnp.sum(s * tr_ref[3, t, lo:lo + hw, :], axis=0, keepdims=True)
                s = (s * tr_ref[0, t, lo:lo + hw, :] + sa * tr_ref[4, t, lo:lo + hw, :]
                     + v_row * tr_ref[1, t, lo:lo + hw, :])
                y_ref[t, pl.ds(lo + vi, 1), :] = jnp.sum(
                    s * tr_ref[5, t, lo:lo + hw, :], axis=0, keepdims=True)
            _store_state(so_ref, fill, (hh, vi), s)
            return carry

        return body

    for hh in range(s0_ref.shape[0]):
        lax.fori_loop(0, hw, make_body(hh), 0, unroll=4)

    inv_n = 1.0 / hw
    for t in range(n_tok):
        rows = slice(t * n_seq, (t + 1) * n_seq)
        y = y_ref[t].T
        sum_y, sum_yy = _segsum_many([y, y * y], ones_blk)
        mean = sum_y * inv_n
        var = sum_yy * inv_n - mean * mean
        yn = (y - mean) * lax.rsqrt(var + RWKV_GN_EPS) * ln_w + ln_b
        z = z_ref[rows, :]
        g_ref[rows, :] = ((yn + bonus_s[rows] * v[rows]) * (z * _sigmoid(z))).astype(BF16)


def _rwkv_step(proj, vf_src, wm, am, vm, w2p, a2p, v2p, prm, s0, s_out_prev, *, layer_j, n_tok,
               has_vres):
    m = proj.shape[0]
    d = proj.shape[1] // 4
    pair = 2 * RWKV_HEAD
    n_pairs = d // pair
    n_seq = s0.shape[-1]

    def col(off):
        return pl.BlockSpec((m, pair), lambda hp: (0, off * n_pairs + hp))

    lora = pl.BlockSpec((m, LORA_PAD), lambda hp: (0, 0))
    up = pl.BlockSpec((LORA_PAD, pair), lambda hp: (0, hp))
    blk = (2, RWKV_HEAD, RWKV_HEAD, n_seq)
    state = pl.BlockSpec((None,) + blk, lambda hp: (layer_j, hp, 0, 0, 0))
    prev = [] if s_out_prev is None else [s_out_prev]
    fill = None if prev else (layer_j, s0.shape[0])
    state_out = state if prev else pl.BlockSpec((s0.shape[0],) + blk, lambda hp: (0, hp, 0, 0, 0))
    ones_blk = jnp.where(
        lax.shift_right_logical(_iota((pair, pair), 0), 6)
        == lax.shift_right_logical(_iota((pair, pair), 1), 6), 1.0, 0.0).astype(BF16)
    return pl.pallas_call(
        functools.partial(_rwkv_step_kernel, n_tok, has_vres, fill),
        grid=(n_pairs,),
        in_specs=[col(0), col(1), col(2), col(3), lora, lora, lora, col(2), up, up, up,
                  pl.BlockSpec((8, pair), lambda hp: (0, hp)), state,
                  pl.BlockSpec((pair, pair), lambda hp: (0, 0))]
        + [pl.BlockSpec(memory_space=pl.ANY)] * len(prev),
        out_specs=[pl.BlockSpec((m, pair), lambda hp: (0, hp)), state_out],
        out_shape=[jax.ShapeDtypeStruct((m, d), BF16), jax.ShapeDtypeStruct(s0.shape, F32)],
        scratch_shapes=[pltpu.VMEM((6, n_tok, pair, n_seq), F32),
                        pltpu.VMEM((n_tok, pair, n_seq), F32)],
        input_output_aliases={14: 1} if prev else {},
        compiler_params=pltpu.CompilerParams(
            dimension_semantics=("arbitrary",), vmem_limit_bytes=VMEM_LIMIT),
        name="rwkv_step",
    )(proj, proj, proj, proj, wm, am, vm, vf_src, w2p, a2p, v2p, prm, s0, ones_blk, *prev)


def _hgrn_head(chunk, tvalid, rows_sl, sl, head, refs, lb, nw, ones_blk):
    (q_ref, f_ref, i_ref, z_ref, st_ref) = refs
    sub = min(HGRN_SUB, chunk)
    shift = int(math.log2(sub))
    fl = f_ref[rows_sl, sl]
    qraw = q_ref[rows_sl, sl]
    val = i_ref[rows_sl, sl]
    f = lb + (1.0 - lb) * _sigmoid(fl)
    logf = jnp.log(jnp.maximum(f, GATE_FLOOR)) * LOG2_E
    if tvalid < chunk:
        logf = jnp.where(_iota(logf.shape, 0) < tvalid, logf, 0.0)
    kg = (1.0 - lb) * _sigmoid(-fl)
    qs = qraw * _sigmoid(qraw) * (HGRN_DK ** -0.5)
    gcum = _cumsum_rows(logf)
    yield

    st = st_ref[head]
    o_inter = _mm_nt((qs * jnp.exp2(gcum)).astype(BF16), st.astype(BF16))
    g_last = gcum[chunk - 1:chunk, :]
    k_tail = kg * jnp.exp2(g_last - gcum)
    st_ref[head] = st * jnp.exp2(g_last) + _mm_tn(val.astype(BF16), k_tail.astype(BF16))

    def bcast_rows(x):
        return jnp.concatenate(
            [jnp.broadcast_to(x[t:t + 1, :], (sub, HGRN_DK)) for t in range(tvalid)], axis=0)

    def tile_blocks(x):
        return jnp.concatenate(
            [x[(t >> shift) * sub:((t >> shift) + 1) * sub] for t in range(tvalid)], axis=0)

    rows = tvalid * sub
    prod = bcast_rows(qs) * tile_blocks(kg) * jnp.exp2(
        jnp.minimum(bcast_rows(gcum) - tile_blocks(gcum), 0.0))
    att_diag = _mm(prod.astype(BF16), ones_blk)
    n_blk = -(-tvalid // sub)
    att_off = []
    for bi in range(1, n_blk):
        lo = bi * sub
        g_edge = gcum[lo - 1:lo]
        q_n = (qs[lo:lo + sub] * jnp.exp2(gcum[lo:lo + sub] - g_edge)).astype(BF16)
        k_n = (kg[0:lo] * jnp.exp2(g_edge - gcum[0:lo])).astype(BF16)
        att_off.append(_mm_nt(q_n, k_n))
    yield

    ridx = _iota((rows, HGRN_DK), 0)
    causal = (jnp.bitwise_and(ridx, sub - 1)
              <= jnp.bitwise_and(lax.shift_right_logical(ridx, shift), sub - 1))
    weighted = jnp.where(causal, att_diag * tile_blocks(val), 0.0)
    sel = jnp.where(lax.shift_right_logical(_iota((chunk, rows), 1), shift) == _iota((chunk, rows), 0),
                    1.0, 0.0).astype(BF16)
    o = o_inter + _mm(sel, weighted.astype(BF16))
    if n_blk > 1:
        parts = [jnp.zeros((sub, HGRN_DK), F32)]
        for bi in range(1, n_blk):
            parts.append(_mm(att_off[bi - 1].astype(BF16), val[0:bi * sub].astype(BF16)))
        if n_blk * sub < chunk:
            parts.append(jnp.zeros((chunk - n_blk * sub, HGRN_DK), F32))
        o = o + jnp.concatenate(parts, axis=0)
    yield

    o = o * lax.rsqrt(jnp.mean(o * o, axis=-1, keepdims=True) + NORM_EPS) * nw
    z = z_ref[rows_sl, sl]
    return o * (z * _sigmoid(z))


def _hgrn_chunk_kernel(chunk, tvalid, layer_j, seqs, heads, fill,
                       q_ref, f_ref, i_ref, z_ref, lbl_ref, nw_ref, s0_ref, *rest):
    g_ref, so_ref, st_ref = rest[-3:]
    c = pl.program_id(2)
    n_chunks = pl.num_programs(2)

    @pl.when(c == 0)
    def _init():
        for q in range(seqs):
            for hh in range(heads):
                st_ref[q * heads + hh] = s0_ref[q, hh].T

    logits = lbl_ref[...]
    ex = jnp.exp(logits - jnp.max(logits, axis=0, keepdims=True))
    soft = ex / jnp.sum(ex, axis=0, keepdims=True)
    lb_all = jnp.sum(soft[0:layer_j + 1], axis=0, keepdims=True) - soft[0:1]

    ones_blk = jnp.ones((HGRN_DK, HGRN_DK), BF16)
    nw = nw_ref[...]
    refs = (q_ref, f_ref, i_ref, z_ref, st_ref)
    outs = _round_robin([
        _hgrn_head(chunk, tvalid, slice(q * chunk, (q + 1) * chunk),
                   slice(hh * HGRN_DK, (hh + 1) * HGRN_DK), q * heads + hh, refs,
                   lb_all[:, hh * HGRN_DK:(hh + 1) * HGRN_DK], nw, ones_blk)
        for q in range(seqs) for hh in range(heads)])
    for hh in range(heads):
        col = [outs[q * heads + hh] for q in range(seqs)]
        g_ref[:, hh * HGRN_DK:(hh + 1) * HGRN_DK] = (
            col[0] if seqs == 1 else jnp.concatenate(col, axis=0)).astype(BF16)

    @pl.when(c == n_chunks - 1)
    def _fin():
        for q in range(seqs):
            for hh in range(heads):
                _store_state(so_ref, fill, (q, hh), st_ref[q * heads + hh].T)


def _hgrn_chunk(proj, lb_logits, norm_w, s0, s_out_prev, *, layer_j, n_seq, n_chunks, chunk, tvalid,
                seqs=1):
    assert seqs == 1 or n_chunks == 1
    m = n_seq * n_chunks * chunk
    d = proj.shape[1] // 4
    n_heads = d // HGRN_DK
    heads = HGRN_HEADS_PER_STEP
    n_hb = n_heads // heads
    width = heads * HGRN_DK

    def col(off):
        return pl.BlockSpec((seqs * chunk, width), lambda s, h, c: (s * n_chunks + c, off * n_hb + h))

    state = pl.BlockSpec((None, seqs, heads, HGRN_DK, HGRN_DK), lambda s, h, c: (layer_j, s, h, 0, 0))
    prev = [] if s_out_prev is None else [s_out_prev]
    fill = None if prev else (layer_j, s0.shape[0])
    state_out = state if prev else pl.BlockSpec(
        (s0.shape[0], seqs, heads, HGRN_DK, HGRN_DK), lambda s, h, c: (0, s, h, 0, 0))
    kernel = functools.partial(_hgrn_chunk_kernel, chunk, tvalid, layer_j, seqs, heads, fill)
    return pl.pallas_call(
        kernel,
        grid=(n_seq // seqs, n_hb, n_chunks),
        in_specs=[col(0), col(1), col(2), col(3),
                  pl.BlockSpec((lb_logits.shape[0], width), lambda s, h, c: (0, h)),
                  pl.BlockSpec((1, HGRN_DK), lambda s, h, c: (0, 0)), state]
        + [pl.BlockSpec(memory_space=pl.ANY)] * len(prev),
        out_specs=[pl.BlockSpec((seqs * chunk, width), lambda s, h, c: (s * n_chunks + c, h)),
                   state_out],
        out_shape=[jax.ShapeDtypeStruct((m, d), BF16),
                   jax.ShapeDtypeStruct(s0.shape, F32)],
        scratch_shapes=[pltpu.VMEM((seqs * heads, HGRN_DK, HGRN_DK), F32)],
        input_output_aliases={7: 1} if prev else {},
        compiler_params=pltpu.CompilerParams(
            dimension_semantics=("arbitrary", "arbitrary", "arbitrary"),
            vmem_limit_bytes=VMEM_LIMIT),
        name="hgrn_chunk",
    )(proj, proj, proj, proj, lb_logits, norm_w.reshape(1, HGRN_DK), s0, *prev)


def _pad_lanes(w, axis):
    pad = [(0, 0)] * w.ndim
    pad[axis] = (0, LORA_PAD - w.shape[axis])
    return jnp.pad(w, pad)


def _rwkv_layer_inputs(h, hprev, p, j, tiles_per_seq):
    has_vres = j > 0
    proj = _proj_rwkv(h, hprev, p["mu4"][j], p["w_in"], j, tiles_per_seq)
    wm, am, vm = _lora(h, hprev, p["mu6"][j], p["w1"][j], p["a1"][j], p["v1"][max(j - 1, 0)],
                       has_vres, tiles_per_seq)
    return proj, wm, am, vm, has_vres


def _trunk_prompt(x, shift0, wkv0, hgrn0, p, *, n_seq, n_tok):
    d = x.shape[1]
    n_chunks = n_tok // PROMPT_CHUNK
    shifts = []
    wkv_out, hgrn_out, vf_src = None, None, None
    h = _rmsnorm(x, p["norm_pre"][0])
    for layer in range(DEPTH):
        j = layer // 2
        npre_next = p["norm_pre"][layer + 1] if layer + 1 < DEPTH else None
        if layer % 2 == 0:
            shifts.append(h[n_tok - 1::n_tok])
            proj, wm, am, vm, has_vres = _rwkv_layer_inputs(
                h, shift0[j][:, None, :], p, j, n_tok // ROW_TILE)
            vf_src = proj if vf_src is None else vf_src
            g, wkv_out = _rwkv_chunk(
                proj, vf_src, wm, am, vm, p["w2"][j], p["a2"][j], p["v2"][max(j - 1, 0)],
                p["rwkv_prm"][j], wkv0, wkv_out, layer_j=j, n_seq=n_seq, n_chunks=n_chunks,
                chunk=PROMPT_CHUNK, tvalid=PROMPT_CHUNK, has_vres=has_vres)
            w_o = p["rwkv_w_o"]
        else:
            proj = _proj_hgrn(h, p["hgrn_w_in"], j)
            g, hgrn_out = _hgrn_chunk(
                proj, p["hgrn_lb_logits"], p["hgrn_norm_w"][j], hgrn0, hgrn_out, layer_j=j,
                n_seq=n_seq, n_chunks=n_chunks, chunk=PROMPT_CHUNK, tvalid=PROMPT_CHUNK)
            w_o = p["hgrn_w_o"]
        x, h = _outproj(g, w_o, j, x, p["norm_post"][layer], npre_next,
                        BF16 if layer % 2 == 0 else F32)
    return x, jnp.stack(shifts), wkv_out, hgrn_out


def _trunk_sample(x, shift0, wkv0, hgrn0, p, *, n_seq, n_tok):
    d = x.shape[1]
    tpad = SAMPLE_TPAD

    def to_padded(a):
        a3 = a.reshape(n_tok, n_seq, a.shape[1]).transpose(1, 0, 2)
        return jnp.pad(a3, ((0, 0), (0, tpad - n_tok), (0, 0))).reshape(n_seq * tpad, a.shape[1])

    def from_padded(a):
        a3 = a.reshape(n_seq, tpad, a.shape[1])[:, :n_tok]
        return a3.transpose(1, 0, 2).reshape(n_tok * n_seq, a.shape[1])

    shifts = []
    wkv_out, hgrn_out, vf_src = None, None, None
    h = _rmsnorm(x, p["norm_pre"][0])
    for layer in range(DEPTH):
        j = layer // 2
        npre_next = p["norm_pre"][layer + 1] if layer + 1 < DEPTH else None
        if layer % 2 == 0:
            hprev = jnp.concatenate([shift0[j], h[:(n_tok - 1) * n_seq]], axis=0)
            shifts.append(h[(n_tok - 1) * n_seq:])
            proj, wm, am, vm, has_vres = _rwkv_layer_inputs(h, hprev, p, j, 0)
            vf_src = proj if vf_src is None else vf_src
            g, wkv_out = _rwkv_step(
                proj, vf_src, wm, am, vm, p["w2"][j], p["a2"][j], p["v2"][max(j - 1, 0)],
                p["rwkv_prm"][j], wkv0, wkv_out, layer_j=j, n_tok=n_tok, has_vres=has_vres)
            w_o = p["rwkv_w_o"]
        else:
            proj = to_padded(_proj_hgrn(h, p["hgrn_w_in"], j))
            g, hgrn_out = _hgrn_chunk(
                proj, p["hgrn_lb_logits"], p["hgrn_norm_w"][j], hgrn0, hgrn_out, layer_j=j,
                n_seq=n_seq, n_chunks=1, chunk=tpad, tvalid=n_tok, seqs=HGRN_SAMPLE_SEQS)
            g = from_padded(g)
            w_o = p["hgrn_w_o"]
        x, h = _outproj(g, w_o, j, x, p["norm_post"][layer], npre_next,
                        BF16 if layer % 2 == 0 else F32)
    return x, jnp.stack(shifts), wkv_out, hgrn_out


def kernel(x_prompt, x_sample, state_rwkv_shift, state_rwkv_wkv, state_hgrn, norm_pre, norm_post,
           rwkv_mu, rwkv_w_in, rwkv_w0, rwkv_w1, rwkv_w2, rwkv_a0, rwkv_a1, rwkv_a2, rwkv_v0,
           rwkv_v1, rwkv_v2, rwkv_k_k, rwkv_k_a, rwkv_r_k, rwkv_ln_w, rwkv_ln_b, rwkv_w_o,
           hgrn_w_in, hgrn_lb_logits, hgrn_norm_w, hgrn_w_o):
    n_rwkv = rwkv_mu.shape[0]
    d = x_prompt.shape[-1]
    v0_full = jnp.concatenate([jnp.zeros((1, d), F32), rwkv_v0], axis=0)
    rwkv_prm = jnp.stack([rwkv_w0, rwkv_a0, v0_full, rwkv_k_k, rwkv_k_a,
                          rwkv_r_k.reshape(n_rwkv, d), rwkv_ln_w, rwkv_ln_b], axis=1)
    p = {
        "norm_pre": norm_pre, "norm_post": norm_post,
        "mu4": rwkv_mu[:, :4, None, :], "mu6": rwkv_mu,
        "w_in": rwkv_w_in.astype(BF16),
        "w1": _pad_lanes(rwkv_w1, 2).astype(BF16), "a1": _pad_lanes(rwkv_a1, 2).astype(BF16),
        "v1": _pad_lanes(rwkv_v1, 2).astype(BF16),
        "w2": _pad_lanes(rwkv_w2, 1).astype(BF16), "a2": _pad_lanes(rwkv_a2, 1).astype(BF16),
        "v2": _pad_lanes(rwkv_v2, 1).astype(BF16),
        "rwkv_prm": rwkv_prm, "rwkv_w_o": rwkv_w_o.astype(BF16),
        "hgrn_w_in": hgrn_w_in.astype(BF16), "hgrn_lb_logits": hgrn_lb_logits,
        "hgrn_norm_w": hgrn_norm_w, "hgrn_w_o": hgrn_w_o.astype(BF16),
    }

    bp, tp, _ = x_prompt.shape
    zero_shift = jnp.zeros((n_rwkv, bp, d), F32)
    zero_wkv = jnp.zeros((n_rwkv, bp) + state_rwkv_wkv.shape[2:], F32)
    zero_hgrn = jnp.zeros((state_hgrn.shape[0], bp) + state_hgrn.shape[2:], F32)
    y_p, p_shift, p_wkv, p_hgrn = _trunk_prompt(
        x_prompt.reshape(bp * tp, d), zero_shift, zero_wkv, zero_hgrn, p, n_seq=bp, n_tok=tp)

    bs, ts, _ = x_sample.shape
    y_s, s_shift, s_wkv, s_hgrn = _trunk_sample(
        x_sample.transpose(1, 0, 2).reshape(ts * bs, d), state_rwkv_shift,
        state_rwkv_wkv.transpose(0, 2, 3, 4, 1), state_hgrn, p, n_seq=bs, n_tok=ts)

    return (y_p.reshape(bp, tp, d), y_s.reshape(ts, bs, d).transpose(1, 0, 2),
            p_shift, p_wkv, p_hgrn, s_shift, s_wkv.transpose(0, 4, 1, 2, 3), s_hgrn)
```

```python
import functools
import math

import jax
import jax.numpy as jnp
from jax import lax
from jax.experimental import pallas as pl
from jax.experimental.pallas import tpu as pltpu

F32 = jnp.float32
BF16 = jnp.bfloat16

D_MODEL = 2048
DEPTH = 4
RWKV_HEAD = 64
HGRN_DK = 128
RWKV_GN_EPS = 1e-5 * RWKV_HEAD
NORM_EPS = 1e-6
GATE_FLOOR = 1e-30
LOG2_E = 1.0 / math.log(2.0)

LANES = 128
GROUP = 256
HEADS_PER_GROUP = GROUP // RWKV_HEAD
LORA_PAD = 128
PROMPT_CHUNK = 64
SAMPLE_TPAD = 8
HGRN_SUB = 8
HGRN_HEADS_PER_STEP = 16
HGRN_SAMPLE_SEQS = 4
ROW_TILE = 512
VMEM_LIMIT = 48 * 1024 * 1024


def _mm(a, b):
    return jnp.dot(a, b, preferred_element_type=F32)


def _mm_nt(a, b):
    return lax.dot_general(a, b, (((1,), (1,)), ((), ())), preferred_element_type=F32)


def _mm_tn(a, b):
    return lax.dot_general(a, b, (((0,), (0,)), ((), ())), preferred_element_type=F32)


def _iota(shape, dim):
    return lax.broadcasted_iota(jnp.int32, shape, dim)


def _split2(x):
    hi = x.astype(BF16)
    lo = (x - hi.astype(F32)).astype(BF16)
    return hi, lo


def _segsum(x, ones_blk):
    hi, lo = _split2(x)
    return _mm(hi, ones_blk) + _mm(lo, ones_blk)


def _split3(x):
    x1 = x.astype(BF16)
    r1 = x - x1.astype(F32)
    x2 = r1.astype(BF16)
    return x1, x2, (r1 - x2.astype(F32)).astype(BF16)


def _segsum_many(xs, ones_blk, n_pieces=2):
    split = _split2 if n_pieces == 2 else _split3
    pieces = []
    for x in xs:
        pieces.extend(split(x))
    out = _mm(jnp.concatenate(pieces, axis=0), ones_blk)
    n = xs[0].shape[0]
    sums = []
    for i in range(len(xs)):
        parts = [out[(n_pieces * i + j) * n:(n_pieces * i + j + 1) * n] for j in range(n_pieces)]
        sums.append(functools.reduce(lambda a, b: a + b, parts))
    return sums


def _cumsum_rows(x):
    n = x.shape[0]
    tri = jnp.where(_iota((n, n), 0) >= _iota((n, n), 1), 1.0, 0.0).astype(BF16)
    x1, x2, x3 = _split3(x)
    return _mm(tri, x1) + _mm(tri, x2) + _mm(tri, x3)


def _block_diag(x, head_width, n_heads):
    shift = int(math.log2(head_width))
    lane_head = lax.shift_right_logical(_iota(x.shape, 1), shift)
    parts = [jnp.where(lane_head == h, x, 0.0).astype(BF16) for h in range(n_heads)]
    return jnp.concatenate(parts, axis=0)


def _sigmoid(x):
    return jax.nn.sigmoid(x)


def _log_decay(wl):
    return _sigmoid(wl) * (-math.exp(-0.5))


def _store_state(so_ref, fill, idx, value):
    if fill is None:
        so_ref[idx] = value
        return
    layer, n_layers = fill
    for other in range(n_layers):
        so_ref[(other,) + idx] = value if other == layer else jnp.zeros_like(value)


def _round_robin(gens):
    results = [None] * len(gens)
    active = list(range(len(gens)))
    while active:
        for idx in list(active):
            try:
                next(gens[idx])
            except StopIteration as stop:
                results[idx] = stop.value
                active.remove(idx)
    return results


def _rmsnorm_kernel(x_ref, w_ref, o_ref):
    x = x_ref[...]
    ms = jnp.mean(x * x, axis=-1, keepdims=True)
    o_ref[...] = x * lax.rsqrt(ms + NORM_EPS) * w_ref[...]


def _rmsnorm(x, w):
    m, d = x.shape
    return pl.pallas_call(
        _rmsnorm_kernel,
        grid=(m // ROW_TILE,),
        in_specs=[pl.BlockSpec((ROW_TILE, d), lambda i: (i, 0)),
                  pl.BlockSpec((1, d), lambda i: (0, 0))],
        out_specs=pl.BlockSpec((ROW_TILE, d), lambda i: (i, 0)),
        out_shape=jax.ShapeDtypeStruct((m, d), F32),
        name="rmsnorm",
    )(x, w.reshape(1, d))


def _prev_rows(h, first_ref, carry_ref, tile, tiles_per_seq):
    rolled = pltpu.roll(h, 1, axis=0)
    seq_start = lax.rem(tile, tiles_per_seq) == 0
    edge = jnp.where(seq_start, first_ref[...], carry_ref[7:8, :])
    carry_ref[...] = h[h.shape[0] - 8:, :]
    return jnp.where(_iota(h.shape, 0) == 0, edge, rolled)


def _proj_mix_kernel(tiles_per_seq, h_ref, hp_ref, mu_ref, w_ref, o_ref, *carry):
    h = h_ref[...]
    if tiles_per_seq:
        hprev = _prev_rows(h, hp_ref, carry[0], pl.program_id(1), tiles_per_seq)
    else:
        hprev = hp_ref[...]
    xs = h + (hprev - h) * mu_ref[...]
    o_ref[...] = _mm(xs.astype(BF16), w_ref[...])


def _proj_plain_kernel(h_ref, w_ref, o_ref):
    o_ref[...] = _mm(h_ref[...].astype(BF16), w_ref[...])


def _prev_spec(hprev, tiles_per_seq, d, grid_rank):
    if tiles_per_seq:
        if grid_rank == 2:
            return pl.BlockSpec((None, 1, d), lambda n, i: (i // tiles_per_seq, 0, 0))
        return pl.BlockSpec((None, 1, d), lambda i: (i // tiles_per_seq, 0, 0))
    if grid_rank == 2:
        return pl.BlockSpec((ROW_TILE, d), lambda n, i: (i, 0))
    return pl.BlockSpec((ROW_TILE, d), lambda i: (i, 0))


def _proj_rwkv(h, hprev, mu4, w4, layer_j, tiles_per_seq):
    m, d = h.shape
    n_proj = w4.shape[1]
    return pl.pallas_call(
        functools.partial(_proj_mix_kernel, tiles_per_seq),
        grid=(n_proj, m // ROW_TILE),
        in_specs=[pl.BlockSpec((ROW_TILE, d), lambda n, i: (i, 0)),
                  _prev_spec(hprev, tiles_per_seq, d, 2),
                  pl.BlockSpec((None, 1, d), lambda n, i: (n, 0, 0)),
                  pl.BlockSpec((None, None, d, d), lambda n, i: (layer_j, n, 0, 0))],
        out_specs=pl.BlockSpec((ROW_TILE, d), lambda n, i: (i, n)),
        out_shape=jax.ShapeDtypeStruct((m, n_proj * d), F32),
        scratch_shapes=[pltpu.VMEM((8, d), F32)] if tiles_per_seq else [],
        compiler_params=pltpu.CompilerParams(
            dimension_semantics=("arbitrary", "arbitrary"), vmem_limit_bytes=VMEM_LIMIT),
        name="proj_rwkv",
    )(h, hprev, mu4, w4)


def _proj_hgrn(h, w, layer_j):
    m, d = h.shape
    n_proj = w.shape[2] // d
    return pl.pallas_call(
        _proj_plain_kernel,
        grid=(n_proj, m // ROW_TILE),
        in_specs=[pl.BlockSpec((ROW_TILE, d), lambda n, i: (i, 0)),
                  pl.BlockSpec((None, d, d), lambda n, i: (layer_j, 0, n))],
        out_specs=pl.BlockSpec((ROW_TILE, d), lambda n, i: (i, n)),
        out_shape=jax.ShapeDtypeStruct((m, n_proj * d), F32),
        compiler_params=pltpu.CompilerParams(
            dimension_semantics=("arbitrary", "arbitrary"), vmem_limit_bytes=VMEM_LIMIT),
        name="proj_hgrn",
    )(h, w)


def _lora_kernel(has_v, tiles_per_seq, h_ref, hp_ref, mu_ref, w1_ref, a1_ref, v1_ref,
                 wm_ref, am_ref, vm_ref, *carry):
    h = h_ref[...]
    if tiles_per_seq:
        hprev = _prev_rows(h, hp_ref, carry[0], pl.program_id(0), tiles_per_seq)
    else:
        hprev = hp_ref[...]
    delta = hprev - h
    xw = h + delta * mu_ref[4:5, :]
    xa = h + delta * mu_ref[5:6, :]
    wm_ref[...] = jnp.tanh(_mm(xw.astype(BF16), w1_ref[...]))
    am_ref[...] = _mm(xa.astype(BF16), a1_ref[...])
    if has_v:
        xv = h + delta * mu_ref[2:3, :]
        vm_ref[...] = _mm(xv.astype(BF16), v1_ref[...])
    else:
        vm_ref[...] = jnp.zeros(vm_ref.shape, F32)


def _lora(h, hprev, mu6, w1p, a1p, v1p, has_v, tiles_per_seq):
    m, d = h.shape
    row = pl.BlockSpec((ROW_TILE, d), lambda i: (i, 0))
    wspec = pl.BlockSpec((d, LORA_PAD), lambda i: (0, 0))
    ospec = pl.BlockSpec((ROW_TILE, LORA_PAD), lambda i: (i, 0))
    oshape = jax.ShapeDtypeStruct((m, LORA_PAD), F32)
    return pl.pallas_call(
        functools.partial(_lora_kernel, has_v, tiles_per_seq),
        grid=(m // ROW_TILE,),
        in_specs=[row, _prev_spec(hprev, tiles_per_seq, d, 1),
                  pl.BlockSpec(mu6.shape, lambda i: (0, 0)), wspec, wspec, wspec],
        out_specs=[ospec, ospec, ospec],
        out_shape=[oshape, oshape, oshape],
        scratch_shapes=[pltpu.VMEM((8, d), F32)] if tiles_per_seq else [],
        compiler_params=pltpu.CompilerParams(dimension_semantics=("arbitrary",)),
        name="lora",
    )(h, hprev, mu6, w1p, a1p, v1p)


def _outproj_kernel(g_ref, w_ref, x_ref, npost_ref, npre_ref, xo_ref, *ho_ref):
    half = g_ref.shape[0] // 2
    w = w_ref[...]
    outs = [_mm(g_ref[rows, :], w) for rows in (slice(0, half), slice(half, 2 * half))]
    for rows, out in zip((slice(0, half), slice(half, 2 * half)), outs):
        ms = jnp.mean(out * out, axis=-1, keepdims=True)
        xn = x_ref[rows, :] + out * lax.rsqrt(ms + NORM_EPS) * npost_ref[...]
        xo_ref[rows, :] = xn
        if ho_ref:
            ms2 = jnp.mean(xn * xn, axis=-1, keepdims=True)
            ho_ref[0][rows, :] = (xn * lax.rsqrt(ms2 + NORM_EPS) * npre_ref[...]).astype(ho_ref[0].dtype)


def _outproj(g, w, layer_j, x, npost, npre_next, h_dtype):
    m, d = x.shape
    n_out = 1 if npre_next is None else 2
    if npre_next is None:
        npre_next = npost
    row = pl.BlockSpec((ROW_TILE, d), lambda i: (i, 0))
    vec = pl.BlockSpec((1, d), lambda i: (0, 0))
    out = pl.pallas_call(
        _outproj_kernel,
        grid=(m // ROW_TILE,),
        in_specs=[row, pl.BlockSpec((None, d, d), lambda i: (layer_j, 0, 0)), row, vec, vec],
        out_specs=[row, row][:n_out],
        out_shape=[jax.ShapeDtypeStruct((m, d), F32), jax.ShapeDtypeStruct((m, d), h_dtype)][:n_out],
        compiler_params=pltpu.CompilerParams(
            dimension_semantics=("arbitrary",), vmem_limit_bytes=VMEM_LIMIT),
        name="outproj",
    )(g, w, x, npost.reshape(1, d), npre_next.reshape(1, d))
    return (out[0], None) if n_out == 1 else out


def _rwkv_group(chunk, tvalid, has_vres, n_double, sl, grp, refs, wm, am, vm, ones_blk, diag):
    (r_ref, k_ref, v_ref, z_ref, vf_ref, w2_ref, a2_ref, v2_ref, prm_ref, g_ref, sbd_ref) = refs
    hw = RWKV_HEAD
    nh = HEADS_PER_GROUP
    prm = prm_ref[:, sl]
    w0, a0, v0 = prm[0:1], prm[1:2], prm[2:3]
    kk_w, ka_w, rk_w, ln_w, ln_b = prm[3:4], prm[4:5], prm[5:6], prm[6:7], prm[7:8]
    r = r_ref[:, sl]
    k = k_ref[:, sl]
    v = v_ref[:, sl]

    wl = w0 + _mm(wm, w2_ref[:, sl])
    al = a0 + _mm(am, a2_ref[:, sl])
    if has_vres:
        vl = v0 + _mm(vm, v2_ref[:, sl])
    yield
    logw = _log_decay(wl)
    if tvalid < chunk:
        logw = jnp.where(_iota(logw.shape, 0) < tvalid, logw, 0.0)
    alpha = _sigmoid(al)
    if has_vres:
        v = v + (vf_ref[:, sl] - v) * _sigmoid(vl)
    kk = k * kk_w
    k2 = k * (1.0 + (alpha - 1.0) * ka_w)
    kk_n2, bonus_s = _segsum_many([kk * kk, r * k2 * rk_w], ones_blk)
    cum = _cumsum_rows(logw)
    yield
    kk = kk / jnp.maximum(jnp.sqrt(kk_n2), 1e-12)
    b = kk * alpha
    a = -kk
    e_neg = jnp.exp(-cum)
    a_hat = a * jnp.exp(cum - logw)
    b_hat = b * e_neg
    k_hat = k2 * e_neg
    r_hat = r * jnp.exp(cum)
    cum_last = cum[chunk - 1:chunk, :]
    e_tail = jnp.exp(cum_last - cum)
    b_tail = b * e_tail
    k_tail = k2 * e_tail

    lhs = jnp.concatenate([a_hat, r_hat], axis=0).astype(BF16)
    gram_b = _mm_nt(lhs, _block_diag(b_hat, hw, nh))
    gram_k = _mm_nt(lhs, _block_diag(k_hat, hw, nh))
    s_bd = sbd_ref[grp]
    xy_state = _mm_nt(lhs, s_bd.astype(BF16))
    yield
    shape_cc = (chunk, nh * chunk)
    t_idx = _iota(shape_cc, 0)
    i_idx = jnp.bitwise_and(_iota(shape_cc, 1), chunk - 1)
    strict = i_idx < t_idx
    incl = i_idx <= t_idx
    n_ab = jnp.where(strict, gram_b[:chunk], 0.0)
    a_ak = jnp.where(strict, gram_k[:chunk], 0.0)
    a_rb = jnp.where(incl, gram_b[chunk:], 0.0)
    a_rk = jnp.where(incl, gram_k[chunk:], 0.0)
    v_bd = _block_diag(v, hw, nh)

    xy = xy_state + _mm(jnp.concatenate([a_ak, a_rk], axis=0).astype(BF16), v_bd)
    x = xy[:chunk]
    y_v = xy[chunk:]
    p = n_ab
    if n_double > 1:
        p_next = _mm(p.astype(BF16), _block_diag(p, chunk, nh))
    yield
    for j in range(n_double):
        x = x + _mm(p.astype(BF16), _block_diag(x, hw, nh))
        if j + 1 < n_double:
            p = p_next
            if j + 2 < n_double:
                p_next = _mm(p.astype(BF16), _block_diag(p, chunk, nh))
        yield
    u = x

    y = y_v + _mm(a_rb.astype(BF16), _block_diag(u, hw, nh))
    uv = jnp.concatenate([u, v], axis=0).astype(BF16)
    bk = jnp.concatenate([b_tail, k_tail], axis=0).astype(BF16)
    sbd_ref[grp] = s_bd * jnp.exp(cum_last) + jnp.where(diag, _mm_tn(uv, bk), 0.0)
    yield

    inv_n = 1.0 / RWKV_HEAD
    sum_y, sum_yy = _segsum_many([y, y * y], ones_blk, n_pieces=3)
    yield
    mean = sum_y * inv_n
    var = jnp.maximum(sum_yy * inv_n - mean * mean, 0.0)
    yn = (y - mean) * lax.rsqrt(var + RWKV_GN_EPS) * ln_w + ln_b
    z = z_ref[:, sl]
    g_ref[:, sl] = ((yn + bonus_s * v) * (z * _sigmoid(z))).astype(BF16)


def _rwkv_chunk_kernel(chunk, tvalid, has_vres, n_double, n_groups, fill,
                       r_ref, k_ref, v_ref, z_ref, wm_ref, am_ref, vm_ref, vf_ref,
                       w2_ref, a2_ref, v2_ref, prm_ref, s0_ref, ones_ref, *rest):
    g_ref, so_ref, sbd_ref = rest[-3:]
    c = pl.program_id(1)
    n_chunks = pl.num_programs(1)
    hw = RWKV_HEAD
    diag = (lax.shift_right_logical(_iota((GROUP, GROUP), 0), 6)
            == lax.shift_right_logical(_iota((GROUP, GROUP), 1), 6))

    @pl.when(c == 0)
    def _init():
        for g in range(n_groups):
            s4 = s0_ref[g * HEADS_PER_GROUP:(g + 1) * HEADS_PER_GROUP].reshape(GROUP, hw)
            tiled = jnp.concatenate([s4] * HEADS_PER_GROUP, axis=1)
            sbd_ref[g] = jnp.where(diag, tiled, 0.0)

    ones_blk = ones_ref[...]
    wm = wm_ref[...].astype(BF16)
    am = am_ref[...].astype(BF16)
    vm = vm_ref[...].astype(BF16)
    refs = (r_ref, k_ref, v_ref, z_ref, vf_ref, w2_ref, a2_ref, v2_ref, prm_ref, g_ref, sbd_ref)
    _round_robin([
        _rwkv_group(chunk, tvalid, has_vres, n_double, slice(g * GROUP, (g + 1) * GROUP), g, refs,
                    wm, am, vm, ones_blk, diag)
        for g in range(n_groups)])

    @pl.when(c == n_chunks - 1)
    def _fin():
        for g in range(n_groups):
            s_new = sbd_ref[g]
            dense = (s_new[:, 0:hw] + s_new[:, hw:2 * hw]
                     + s_new[:, 2 * hw:3 * hw] + s_new[:, 3 * hw:4 * hw])
            _store_state(so_ref, fill, (slice(g * HEADS_PER_GROUP, (g + 1) * HEADS_PER_GROUP),),
                         dense.reshape(HEADS_PER_GROUP, hw, hw))


def _rwkv_chunk(proj, vf_src, wm, am, vm, w2p, a2p, v2p, prm, s0, s_out_prev, *, layer_j, n_seq,
                n_chunks, chunk, tvalid, has_vres):
    m = n_seq * n_chunks * chunk
    d = proj.shape[1] // 4
    n_groups = d // GROUP
    n_double = max(1, math.ceil(math.log2(tvalid)))

    def col(off):
        return pl.BlockSpec((chunk, d), lambda s, c: (s * n_chunks + c, off))

    lora = pl.BlockSpec((chunk, LORA_PAD), lambda s, c: (s * n_chunks + c, 0))
    up = pl.BlockSpec((LORA_PAD, d), lambda s, c: (0, 0))
    state = pl.BlockSpec((None, None) + s0.shape[2:], lambda s, c: (layer_j, s, 0, 0, 0))
    prev = [] if s_out_prev is None else [s_out_prev]
    fill = None if prev else (layer_j, s0.shape[0])
    state_out = state if prev else pl.BlockSpec(
        (s0.shape[0], None) + s0.shape[2:], lambda s, c: (0, s, 0, 0, 0))
    ones_blk = jnp.where(
        lax.shift_right_logical(_iota((GROUP, GROUP), 0), 6)
        == lax.shift_right_logical(_iota((GROUP, GROUP), 1), 6), 1.0, 0.0).astype(BF16)
    kernel = functools.partial(_rwkv_chunk_kernel, chunk, tvalid, has_vres, n_double, n_groups, fill)
    return pl.pallas_call(
        kernel,
        grid=(n_seq, n_chunks),
        in_specs=[col(0), col(1), col(2), col(3), lora, lora, lora, col(2), up, up, up,
                  pl.BlockSpec((8, d), lambda s, c: (0, 0)), state,
                  pl.BlockSpec((GROUP, GROUP), lambda s, c: (0, 0))]
        + [pl.BlockSpec(memory_space=pl.ANY)] * len(prev),
        out_specs=[pl.BlockSpec((chunk, d), lambda s, c: (s * n_chunks + c, 0)), state_out],
        out_shape=[jax.ShapeDtypeStruct((m, d), BF16),
                   jax.ShapeDtypeStruct(s0.shape, F32)],
        scratch_shapes=[pltpu.VMEM((n_groups, GROUP, GROUP), F32)],
        input_output_aliases={14: 1} if prev else {},
        compiler_params=pltpu.CompilerParams(
            dimension_semantics=("arbitrary", "arbitrary"), vmem_limit_bytes=VMEM_LIMIT),
        name="rwkv_chunk",
    )(proj, proj, proj, proj, wm, am, vm, vf_src, w2p, a2p, v2p, prm, s0, ones_blk, *prev)


def _rwkv_step_kernel(n_tok, has_vres, fill,
                      r_ref, k_ref, v_ref, z_ref, wm_ref, am_ref, vm_ref, vf_ref,
                      w2_ref, a2_ref, v2_ref, prm_ref, s0_ref, ones_ref, *rest):
    g_ref, so_ref, tr_ref, y_ref = rest[-4:]
    hw = RWKV_HEAD
    n_seq = s0_ref.shape[-1]
    ones_blk = ones_ref[...]
    prm = prm_ref[...]
    w0, a0, v0 = prm[0:1], prm[1:2], prm[2:3]
    kk_w, ka_w, rk_w, ln_w, ln_b = prm[3:4], prm[4:5], prm[5:6], prm[6:7], prm[7:8]
    r = r_ref[...]
    k = k_ref[...]
    v = v_ref[...]

    wl = w0 + _mm(wm_ref[...].astype(BF16), w2_ref[...])
    decay = jnp.exp(_log_decay(wl))
    alpha = _sigmoid(a0 + _mm(am_ref[...].astype(BF16), a2_ref[...]))
    if has_vres:
        v = v + (vf_ref[...] - v) * _sigmoid(v0 + _mm(vm_ref[...].astype(BF16), v2_ref[...]))
    kk = k * kk_w
    k2 = k * (1.0 + (alpha - 1.0) * ka_w)
    kk_n2, bonus_s = _segsum_many([kk * kk, r * k2 * rk_w], ones_blk)
    kk = kk / jnp.maximum(jnp.sqrt(kk_n2), 1e-12)
    b = kk * alpha

    for idx, arr in enumerate((decay, k2, v, kk, b, r)):
        for t in range(n_tok):
            tr_ref[idx, t] = arr[t * n_seq:(t + 1) * n_seq, :].T

    def make_body(hh):
        lo = hh * hw

        def body(vi, carry):
            s = s0_ref[hh, vi]
            for t in range(n_tok):
                v_row = tr_ref[2, t, pl.ds(lo + vi, 1), :]
                sa = -jnp.sum(s * tr_ref[3, t, lo:lo + hw, :], axis=0, keepdims=True)
                s = (s * tr_ref[0, t, lo:lo + hw, :] + sa * tr_ref[4, t, lo:lo + hw, :]
                     + v_row * tr_ref[1, t, lo:lo + hw, :])
                y_ref[t, pl.ds(lo + vi, 1), :] = jnp.sum(
                    s * tr_ref[5, t, lo:lo + hw, :], axis=0, keepdims=True)
            _store_state(so_ref, fill, (hh, vi), s)
            return carry

        return body

    for hh in range(s0_ref.shape[0]):
        lax.fori_loop(0, hw, make_body(hh), 0, unroll=4)

    inv_n = 1.0 / hw
    for t in range(n_tok):
        rows = slice(t * n_seq, (t + 1) * n_seq)
        y = y_ref[t].T
        sum_y, sum_yy = _segsum_many([y, y * y], ones_blk, n_pieces=3)
        mean = sum_y * inv_n
        var = jnp.maximum(sum_yy * inv_n - mean * mean, 0.0)
        yn = (y - mean) * lax.rsqrt(var + RWKV_GN_EPS) * ln_w + ln_b
        z = z_ref[rows, :]
        g_ref[rows, :] = ((yn + bonus_s[rows] * v[rows]) * (z * _sigmoid(z))).astype(BF16)


def _rwkv_step(proj, vf_src, wm, am, vm, w2p, a2p, v2p, prm, s0, s_out_prev, *, layer_j, n_tok,
               has_vres):
    m = proj.shape[0]
    d = proj.shape[1] // 4
    pair = 2 * RWKV_HEAD
    n_pairs = d // pair
    n_seq = s0.shape[-1]

    def col(off):
        return pl.BlockSpec((m, pair), lambda hp: (0, off * n_pairs + hp))

    lora = pl.BlockSpec((m, LORA_PAD), lambda hp: (0, 0))
    up = pl.BlockSpec((LORA_PAD, pair), lambda hp: (0, hp))
    blk = (2, RWKV_HEAD, RWKV_HEAD, n_seq)
    state = pl.BlockSpec((None,) + blk, lambda hp: (layer_j, hp, 0, 0, 0))
    prev = [] if s_out_prev is None else [s_out_prev]
    fill = None if prev else (layer_j, s0.shape[0])
    state_out = state if prev else pl.BlockSpec((s0.shape[0],) + blk, lambda hp: (0, hp, 0, 0, 0))
    ones_blk = jnp.where(
        lax.shift_right_logical(_iota((pair, pair), 0), 6)
        == lax.shift_right_logical(_iota((pair, pair), 1), 6), 1.0, 0.0).astype(BF16)
    return pl.pallas_call(
        functools.partial(_rwkv_step_kernel, n_tok, has_vres, fill),
        grid=(n_pairs,),
        in_specs=[col(0), col(1), col(2), col(3), lora, lora, lora, col(2), up, up, up,
                  pl.BlockSpec((8, pair), lambda hp: (0, hp)), state,
                  pl.BlockSpec((pair, pair), lambda hp: (0, 0))]
        + [pl.BlockSpec(memory_space=pl.ANY)] * len(prev),
        out_specs=[pl.BlockSpec((m, pair), lambda hp: (0, hp)), state_out],
        out_shape=[jax.ShapeDtypeStruct((m, d), BF16), jax.ShapeDtypeStruct(s0.shape, F32)],
        scratch_shapes=[pltpu.VMEM((6, n_tok, pair, n_seq), F32),
                        pltpu.VMEM((n_tok, pair, n_seq), F32)],
        input_output_aliases={14: 1} if prev else {},
        compiler_params=pltpu.CompilerParams(
            dimension_semantics=("arbitrary",), vmem_limit_bytes=VMEM_LIMIT),
        name="rwkv_step",
    )(proj, proj, proj, proj, wm, am, vm, vf_src, w2p, a2p, v2p, prm, s0, ones_blk, *prev)


def _hgrn_head(chunk, tvalid, rows_sl, sl, head, refs, lb, nw, ones_blk):
    (q_ref, f_ref, i_ref, z_ref, st_ref) = refs
    sub = min(HGRN_SUB, chunk)
    shift = int(math.log2(sub))
    fl = f_ref[rows_sl, sl]
    qraw = q_ref[rows_sl, sl]
    val = i_ref[rows_sl, sl]
    f = lb + (1.0 - lb) * _sigmoid(fl)
    logf = jnp.log(jnp.maximum(f, GATE_FLOOR)) * LOG2_E
    if tvalid < chunk:
        logf = jnp.where(_iota(logf.shape, 0) < tvalid, logf, 0.0)
    kg = (1.0 - lb) * _sigmoid(-fl)
    qs = qraw * _sigmoid(qraw) * (HGRN_DK ** -0.5)
    gcum = _cumsum_rows(logf)
    yield

    st = st_ref[head]
    o_inter = _mm_nt((qs * jnp.exp2(gcum)).astype(BF16), st.astype(BF16))
    g_last = gcum[chunk - 1:chunk, :]
    k_tail = kg * jnp.exp2(g_last - gcum)
    st_ref[head] = st * jnp.exp2(g_last) + _mm_tn(val.astype(BF16), k_tail.astype(BF16))

    def bcast_rows(x):
        return jnp.concatenate(
            [jnp.broadcast_to(x[t:t + 1, :], (sub, HGRN_DK)) for t in range(tvalid)], axis=0)

    def tile_blocks(x):
        return jnp.concatenate(
            [x[(t >> shift) * sub:((t >> shift) + 1) * sub] for t in range(tvalid)], axis=0)

    rows = tvalid * sub
    prod = bcast_rows(qs) * tile_blocks(kg) * jnp.exp2(
        jnp.minimum(bcast_rows(gcum) - tile_blocks(gcum), 0.0))
    att_diag = _mm(prod.astype(BF16), ones_blk)
    n_blk = -(-tvalid // sub)
    att_off = []
    for bi in range(1, n_blk):
        lo = bi * sub
        g_edge = gcum[lo - 1:lo]
        q_n = (qs[lo:lo + sub] * jnp.exp2(gcum[lo:lo + sub] - g_edge)).astype(BF16)
        k_n = (kg[0:lo] * jnp.exp2(g_edge - gcum[0:lo])).astype(BF16)
        att_off.append(_mm_nt(q_n, k_n))
    yield

    ridx = _iota((rows, HGRN_DK), 0)
    causal = (jnp.bitwise_and(ridx, sub - 1)
              <= jnp.bitwise_and(lax.shift_right_logical(ridx, shift), sub - 1))
    weighted = jnp.where(causal, att_diag * tile_blocks(val), 0.0)
    sel = jnp.where(lax.shift_right_logical(_iota((chunk, rows), 1), shift) == _iota((chunk, rows), 0),
                    1.0, 0.0).astype(BF16)
    o = o_inter + _mm(sel, weighted.astype(BF16))
    if n_blk > 1:
        parts = [jnp.zeros((sub, HGRN_DK), F32)]
        for bi in range(1, n_blk):
            parts.append(_mm(att_off[bi - 1].astype(BF16), val[0:bi * sub].astype(BF16)))
        if n_blk * sub < chunk:
            parts.append(jnp.zeros((chunk - n_blk * sub, HGRN_DK), F32))
        o = o + jnp.concatenate(parts, axis=0)
    yield

    o = o * lax.rsqrt(jnp.mean(o * o, axis=-1, keepdims=True) + NORM_EPS) * nw
    z = z_ref[rows_sl, sl]
    return o * (z * _sigmoid(z))


def _hgrn_chunk_kernel(chunk, tvalid, layer_j, seqs, heads, fill,
                       q_ref, f_ref, i_ref, z_ref, lbl_ref, nw_ref, s0_ref, *rest):
    g_ref, so_ref, st_ref = rest[-3:]
    c = pl.program_id(2)
    n_chunks = pl.num_programs(2)

    @pl.when(c == 0)
    def _init():
        for q in range(seqs):
            for hh in range(heads):
                st_ref[q * heads + hh] = s0_ref[q, hh].T

    logits = lbl_ref[...]
    ex = jnp.exp(logits - jnp.max(logits, axis=0, keepdims=True))
    soft = ex / jnp.sum(ex, axis=0, keepdims=True)
    lb_all = jnp.sum(soft[0:layer_j + 1], axis=0, keepdims=True) - soft[0:1]

    ones_blk = jnp.ones((HGRN_DK, HGRN_DK), BF16)
    nw = nw_ref[...]
    refs = (q_ref, f_ref, i_ref, z_ref, st_ref)
    outs = _round_robin([
        _hgrn_head(chunk, tvalid, slice(q * chunk, (q + 1) * chunk),
                   slice(hh * HGRN_DK, (hh + 1) * HGRN_DK), q * heads + hh, refs,
                   lb_all[:, hh * HGRN_DK:(hh + 1) * HGRN_DK], nw, ones_blk)
        for q in range(seqs) for hh in range(heads)])
    for hh in range(heads):
        col = [outs[q * heads + hh] for q in range(seqs)]
        g_ref[:, hh * HGRN_DK:(hh + 1) * HGRN_DK] = (
            col[0] if seqs == 1 else jnp.concatenate(col, axis=0)).astype(BF16)

    @pl.when(c == n_chunks - 1)
    def _fin():
        for q in range(seqs):
            for hh in range(heads):
                _store_state(so_ref, fill, (q, hh), st_ref[q * heads + hh].T)


def _hgrn_chunk(proj, lb_logits, norm_w, s0, s_out_prev, *, layer_j, n_seq, n_chunks, chunk, tvalid,
                seqs=1):
    assert seqs == 1 or n_chunks == 1
    m = n_seq * n_chunks * chunk
    d = proj.shape[1] // 4
    n_heads = d // HGRN_DK
    heads = HGRN_HEADS_PER_STEP
    n_hb = n_heads // heads
    width = heads * HGRN_DK

    def col(off):
        return pl.BlockSpec((seqs * chunk, width), lambda s, h, c: (s * n_chunks + c, off * n_hb + h))

    state = pl.BlockSpec((None, seqs, heads, HGRN_DK, HGRN_DK), lambda s, h, c: (layer_j, s, h, 0, 0))
    prev = [] if s_out_prev is None else [s_out_prev]
    fill = None if prev else (layer_j, s0.shape[0])
    state_out = state if prev else pl.BlockSpec(
        (s0.shape[0], seqs, heads, HGRN_DK, HGRN_DK), lambda s, h, c: (0, s, h, 0, 0))
    kernel = functools.partial(_hgrn_chunk_kernel, chunk, tvalid, layer_j, seqs, heads, fill)
    return pl.pallas_call(
        kernel,
        grid=(n_seq // seqs, n_hb, n_chunks),
        in_specs=[col(0), col(1), col(2), col(3),
                  pl.BlockSpec((lb_logits.shape[0], width), lambda s, h, c: (0, h)),
                  pl.BlockSpec((1, HGRN_DK), lambda s, h, c: (0, 0)), state]
        + [pl.BlockSpec(memory_space=pl.ANY)] * len(prev),
        out_specs=[pl.BlockSpec((seqs * chunk, width), lambda s, h, c: (s * n_chunks + c, h)),
                   state_out],
        out_shape=[jax.ShapeDtypeStruct((m, d), BF16),
                   jax.ShapeDtypeStruct(s0.shape, F32)],
        scratch_shapes=[pltpu.VMEM((seqs * heads, HGRN_DK, HGRN_DK), F32)],
        input_output_aliases={7: 1} if prev else {},
        compiler_params=pltpu.CompilerParams(
            dimension_semantics=("arbitrary", "arbitrary", "arbitrary"),
            vmem_limit_bytes=VMEM_LIMIT),
        name="hgrn_chunk",
    )(proj, proj, proj, proj, lb_logits, norm_w.reshape(1, HGRN_DK), s0, *prev)


def _pad_lanes(w, axis):
    pad = [(0, 0)] * w.ndim
    pad[axis] = (0, LORA_PAD - w.shape[axis])
    return jnp.pad(w, pad)


def _rwkv_layer_inputs(h, hprev, p, j, tiles_per_seq):
    has_vres = j > 0
    proj = _proj_rwkv(h, hprev, p["mu4"][j], p["w_in"], j, tiles_per_seq)
    wm, am, vm = _lora(h, hprev, p["mu6"][j], p["w1"][j], p["a1"][j], p["v1"][max(j - 1, 0)],
                       has_vres, tiles_per_seq)
    return proj, wm, am, vm, has_vres


def _trunk_prompt(x, shift0, wkv0, hgrn0, p, *, n_seq, n_tok):
    d = x.shape[1]
    n_chunks = n_tok // PROMPT_CHUNK
    shifts = []
    wkv_out, hgrn_out, vf_src = None, None, None
    h = _rmsnorm(x, p["norm_pre"][0])
    for layer in range(DEPTH):
        j = layer // 2
        npre_next = p["norm_pre"][layer + 1] if layer + 1 < DEPTH else None
        if layer % 2 == 0:
            shifts.append(h[n_tok - 1::n_tok])
            proj, wm, am, vm, has_vres = _rwkv_layer_inputs(
                h, shift0[j][:, None, :], p, j, n_tok // ROW_TILE)
            vf_src = proj if vf_src is None else vf_src
            g, wkv_out = _rwkv_chunk(
                proj, vf_src, wm, am, vm, p["w2"][j], p["a2"][j], p["v2"][max(j - 1, 0)],
                p["rwkv_prm"][j], wkv0, wkv_out, layer_j=j, n_seq=n_seq, n_chunks=n_chunks,
                chunk=PROMPT_CHUNK, tvalid=PROMPT_CHUNK, has_vres=has_vres)
            w_o = p["rwkv_w_o"]
        else:
            proj = _proj_hgrn(h, p["hgrn_w_in"], j)
            g, hgrn_out = _hgrn_chunk(
                proj, p["hgrn_lb_logits"], p["hgrn_norm_w"][j], hgrn0, hgrn_out, layer_j=j,
                n_seq=n_seq, n_chunks=n_chunks, chunk=PROMPT_CHUNK, tvalid=PROMPT_CHUNK)
            w_o = p["hgrn_w_o"]
        x, h = _outproj(g, w_o, j, x, p["norm_post"][layer], npre_next,
                        BF16 if layer % 2 == 0 else F32)
    return x, jnp.stack(shifts), wkv_out, hgrn_out


def _trunk_sample(x, shift0, wkv0, hgrn0, p, *, n_seq, n_tok):
    d = x.shape[1]
    tpad = SAMPLE_TPAD

    def to_padded(a):
        a3 = a.reshape(n_tok, n_seq, a.shape[1]).transpose(1, 0, 2)
        return jnp.pad(a3, ((0, 0), (0, tpad - n_tok), (0, 0))).reshape(n_seq * tpad, a.shape[1])

    def from_padded(a):
        a3 = a.reshape(n_seq, tpad, a.shape[1])[:, :n_tok]
        return a3.transpose(1, 0, 2).reshape(n_tok * n_seq, a.shape[1])

    shifts = []
    wkv_out, hgrn_out, vf_src = None, None, None
    h = _rmsnorm(x, p["norm_pre"][0])
    for layer in range(DEPTH):
        j = layer // 2
        npre_next = p["norm_pre"][layer + 1] if layer + 1 < DEPTH else None
        if layer % 2 == 0:
            hprev = jnp.concatenate([shift0[j], h[:(n_tok - 1) * n_seq]], axis=0)
            shifts.append(h[(n_tok - 1) * n_seq:])
            proj, wm, am, vm, has_vres = _rwkv_layer_inputs(h, hprev, p, j, 0)
            vf_src = proj if vf_src is None else vf_src
            g, wkv_out = _rwkv_step(
                proj, vf_src, wm, am, vm, p["w2"][j], p["a2"][j], p["v2"][max(j - 1, 0)],
                p["rwkv_prm"][j], wkv0, wkv_out, layer_j=j, n_tok=n_tok, has_vres=has_vres)
            w_o = p["rwkv_w_o"]
        else:
            proj = to_padded(_proj_hgrn(h, p["hgrn_w_in"], j))
            g, hgrn_out = _hgrn_chunk(
                proj, p["hgrn_lb_logits"], p["hgrn_norm_w"][j], hgrn0, hgrn_out, layer_j=j,
                n_seq=n_seq, n_chunks=1, chunk=tpad, tvalid=n_tok, seqs=HGRN_SAMPLE_SEQS)
            g = from_padded(g)
            w_o = p["hgrn_w_o"]
        x, h = _outproj(g, w_o, j, x, p["norm_post"][layer], npre_next,
                        BF16 if layer % 2 == 0 else F32)
    return x, jnp.stack(shifts), wkv_out, hgrn_out


def kernel(x_prompt, x_sample, state_rwkv_shift, state_rwkv_wkv, state_hgrn, norm_pre, norm_post,
           rwkv_mu, rwkv_w_in, rwkv_w0, rwkv_w1, rwkv_w2, rwkv_a0, rwkv_a1, rwkv_a2, rwkv_v0,
           rwkv_v1, rwkv_v2, rwkv_k_k, rwkv_k_a, rwkv_r_k, rwkv_ln_w, rwkv_ln_b, rwkv_w_o,
           hgrn_w_in, hgrn_lb_logits, hgrn_norm_w, hgrn_w_o):
    n_rwkv = rwkv_mu.shape[0]
    d = x_prompt.shape[-1]
    v0_full = jnp.concatenate([jnp.zeros((1, d), F32), rwkv_v0], axis=0)
    rwkv_prm = jnp.stack([rwkv_w0, rwkv_a0, v0_full, rwkv_k_k, rwkv_k_a,
                          rwkv_r_k.reshape(n_rwkv, d), rwkv_ln_w, rwkv_ln_b], axis=1)
    p = {
        "norm_pre": norm_pre, "norm_post": norm_post,
        "mu4": rwkv_mu[:, :4, None, :], "mu6": rwkv_mu,
        "w_in": rwkv_w_in.astype(BF16),
        "w1": _pad_lanes(rwkv_w1, 2).astype(BF16), "a1": _pad_lanes(rwkv_a1, 2).astype(BF16),
        "v1": _pad_lanes(rwkv_v1, 2).astype(BF16),
        "w2": _pad_lanes(rwkv_w2, 1).astype(BF16), "a2": _pad_lanes(rwkv_a2, 1).astype(BF16),
        "v2": _pad_lanes(rwkv_v2, 1).astype(BF16),
        "rwkv_prm": rwkv_prm, "rwkv_w_o": rwkv_w_o.astype(BF16),
        "hgrn_w_in": hgrn_w_in.astype(BF16), "hgrn_lb_logits": hgrn_lb_logits,
        "hgrn_norm_w": hgrn_norm_w, "hgrn_w_o": hgrn_w_o.astype(BF16),
    }

    bp, tp, _ = x_prompt.shape
    zero_shift = jnp.zeros((n_rwkv, bp, d), F32)
    zero_wkv = jnp.zeros((n_rwkv, bp) + state_rwkv_wkv.shape[2:], F32)
    zero_hgrn = jnp.zeros((state_hgrn.shape[0], bp) + state_hgrn.shape[2:], F32)
    y_p, p_shift, p_wkv, p_hgrn = _trunk_prompt(
        x_prompt.reshape(bp * tp, d), zero_shift, zero_wkv, zero_hgrn, p, n_seq=bp, n_tok=tp)

    bs, ts, _ = x_sample.shape
    y_s, s_shift, s_wkv, s_hgrn = _trunk_sample(
        x_sample.transpose(1, 0, 2).reshape(ts * bs, d), state_rwkv_shift,
        state_rwkv_wkv.transpose(0, 2, 3, 4, 1), state_hgrn, p, n_seq=bs, n_tok=ts)

    return (y_p.reshape(bp, tp, d), y_s.reshape(ts, bs, d).transpose(1, 0, 2),
            p_shift, p_wkv, p_hgrn, s_shift, s_wkv.transpose(0, 4, 1, 2, 3), s_hgrn)
```

```python
import functools
import math

import jax
import jax.numpy as jnp
from jax import lax
from jax.experimental import pallas as pl
from jax.experimental.pallas import tpu as pltpu

F32 = jnp.float32
BF16 = jnp.bfloat16

D_MODEL = 2048
DEPTH = 4
RWKV_HEAD = 64
HGRN_DK = 128
RWKV_GN_EPS = 1e-5 * RWKV_HEAD
NORM_EPS = 1e-6
GATE_FLOOR = 1e-30
LOG2_E = 1.0 / math.log(2.0)

LANES = 128
GROUP = 256
HEADS_PER_GROUP = GROUP // RWKV_HEAD
LORA_PAD = 128
PROMPT_CHUNK = 64
SAMPLE_TPAD = 8
HGRN_SUB = 8
HGRN_HEADS_PER_STEP = 16
HGRN_SAMPLE_SEQS = 4
ROW_TILE = 512
VMEM_LIMIT = 48 * 1024 * 1024


def _mm(a, b):
    return jnp.dot(a, b, preferred_element_type=F32)


def _mm_nt(a, b):
    return lax.dot_general(a, b, (((1,), (1,)), ((), ())), preferred_element_type=F32)


def _mm_tn(a, b):
    return lax.dot_general(a, b, (((0,), (0,)), ((), ())), preferred_element_type=F32)


def _iota(shape, dim):
    return lax.broadcasted_iota(jnp.int32, shape, dim)


def _split2(x):
    hi = x.astype(BF16)
    lo = (x - hi.astype(F32)).astype(BF16)
    return hi, lo


def _segsum(x, ones_blk):
    hi, lo = _split2(x)
    return _mm(hi, ones_blk) + _mm(lo, ones_blk)


def _split3(x):
    x1 = x.astype(BF16)
    r1 = x - x1.astype(F32)
    x2 = r1.astype(BF16)
    return x1, x2, (r1 - x2.astype(F32)).astype(BF16)


def _segsum_many(xs, ones_blk, n_pieces=2):
    split = _split2 if n_pieces == 2 else _split3
    pieces = []
    for x in xs:
        pieces.extend(split(x))
    out = _mm(jnp.concatenate(pieces, axis=0), ones_blk)
    n = xs[0].shape[0]
    sums = []
    for i in range(len(xs)):
        parts = [out[(n_pieces * i + j) * n:(n_pieces * i + j + 1) * n] for j in range(n_pieces)]
        sums.append(functools.reduce(lambda a, b: a + b, parts))
    return sums


def _cumsum_rows(x):
    n = x.shape[0]
    tri = jnp.where(_iota((n, n), 0) >= _iota((n, n), 1), 1.0, 0.0).astype(BF16)
    x1, x2, x3 = _split3(x)
    return _mm(tri, x1) + _mm(tri, x2) + _mm(tri, x3)


def _block_diag(x, head_width, n_heads):
    shift = int(math.log2(head_width))
    lane_head = lax.shift_right_logical(_iota(x.shape, 1), shift)
    parts = [jnp.where(lane_head == h, x, 0.0).astype(BF16) for h in range(n_heads)]
    return jnp.concatenate(parts, axis=0)


def _sigmoid(x):
    return jax.nn.sigmoid(x)


def _log_decay(wl):
    return _sigmoid(wl) * (-math.exp(-0.5))


def _store_state(so_ref, fill, idx, value):
    if fill is None:
        so_ref[idx] = value
        return
    layer, n_layers = fill
    for other in range(n_layers):
        so_ref[(other,) + idx] = value if other == layer else jnp.zeros_like(value)


def _round_robin(gens):
    results = [None] * len(gens)
    active = list(range(len(gens)))
    while active:
        for idx in list(active):
            try:
                next(gens[idx])
            except StopIteration as stop:
                results[idx] = stop.value
                active.remove(idx)
    return results


def _rmsnorm_kernel(x_ref, w_ref, o_ref):
    x = x_ref[...]
    ms = jnp.mean(x * x, axis=-1, keepdims=True)
    o_ref[...] = x * lax.rsqrt(ms + NORM_EPS) * w_ref[...]


def _rmsnorm(x, w):
    m, d = x.shape
    return pl.pallas_call(
        _rmsnorm_kernel,
        grid=(m // ROW_TILE,),
        in_specs=[pl.BlockSpec((ROW_TILE, d), lambda i: (i, 0)),
                  pl.BlockSpec((1, d), lambda i: (0, 0))],
        out_specs=pl.BlockSpec((ROW_TILE, d), lambda i: (i, 0)),
        out_shape=jax.ShapeDtypeStruct((m, d), F32),
        name="rmsnorm",
    )(x, w.reshape(1, d))


def _prev_rows(h, first_ref, carry_ref, tile, tiles_per_seq):
    rolled = pltpu.roll(h, 1, axis=0)
    seq_start = lax.rem(tile, tiles_per_seq) == 0
    edge = jnp.where(seq_start, first_ref[...], carry_ref[7:8, :])
    carry_ref[...] = h[h.shape[0] - 8:, :]
    return jnp.where(_iota(h.shape, 0) == 0, edge, rolled)


def _proj_mix_kernel(tiles_per_seq, h_ref, hp_ref, mu_ref, w_ref, o_ref, *carry):
    h = h_ref[...]
    if tiles_per_seq:
        hprev = _prev_rows(h, hp_ref, carry[0], pl.program_id(1), tiles_per_seq)
    else:
        hprev = hp_ref[...]
    xs = h + (hprev - h) * mu_ref[...]
    o_ref[...] = _mm(xs.astype(BF16), w_ref[...])


def _proj_plain_kernel(h_ref, w_ref, o_ref):
    o_ref[...] = _mm(h_ref[...].astype(BF16), w_ref[...])


def _prev_spec(hprev, tiles_per_seq, d, grid_rank):
    if tiles_per_seq:
        if grid_rank == 2:
            return pl.BlockSpec((None, 1, d), lambda n, i: (i // tiles_per_seq, 0, 0))
        return pl.BlockSpec((None, 1, d), lambda i: (i // tiles_per_seq, 0, 0))
    if grid_rank == 2:
        return pl.BlockSpec((ROW_TILE, d), lambda n, i: (i, 0))
    return pl.BlockSpec((ROW_TILE, d), lambda i: (i, 0))


def _proj_rwkv(h, hprev, mu4, w4, layer_j, tiles_per_seq):
    m, d = h.shape
    n_proj = w4.shape[1]
    return pl.pallas_call(
        functools.partial(_proj_mix_kernel, tiles_per_seq),
        grid=(n_proj, m // ROW_TILE),
        in_specs=[pl.BlockSpec((ROW_TILE, d), lambda n, i: (i, 0)),
                  _prev_spec(hprev, tiles_per_seq, d, 2),
                  pl.BlockSpec((None, 1, d), lambda n, i: (n, 0, 0)),
                  pl.BlockSpec((None, None, d, d), lambda n, i: (layer_j, n, 0, 0))],
        out_specs=pl.BlockSpec((ROW_TILE, d), lambda n, i: (i, n)),
        out_shape=jax.ShapeDtypeStruct((m, n_proj * d), F32),
        scratch_shapes=[pltpu.VMEM((8, d), F32)] if tiles_per_seq else [],
        compiler_params=pltpu.CompilerParams(
            dimension_semantics=("arbitrary", "arbitrary"), vmem_limit_bytes=VMEM_LIMIT),
        name="proj_rwkv",
    )(h, hprev, mu4, w4)


def _proj_hgrn(h, w, layer_j):
    m, d = h.shape
    n_proj = w.shape[2] // d
    return pl.pallas_call(
        _proj_plain_kernel,
        grid=(n_proj, m // ROW_TILE),
        in_specs=[pl.BlockSpec((ROW_TILE, d), lambda n, i: (i, 0)),
                  pl.BlockSpec((None, d, d), lambda n, i: (layer_j, 0, n))],
        out_specs=pl.BlockSpec((ROW_TILE, d), lambda n, i: (i, n)),
        out_shape=jax.ShapeDtypeStruct((m, n_proj * d), F32),
        compiler_params=pltpu.CompilerParams(
            dimension_semantics=("arbitrary", "arbitrary"), vmem_limit_bytes=VMEM_LIMIT),
        name="proj_hgrn",
    )(h, w)


def _lora_kernel(has_v, tiles_per_seq, h_ref, hp_ref, mu_ref, w1_ref, a1_ref, v1_ref,
                 wm_ref, am_ref, vm_ref, *rest):
    h = h_ref[...]
    if tiles_per_seq:
        tail_ref, carry_ref = rest
        hprev = _prev_rows(h, hp_ref, carry_ref, pl.program_id(0), tiles_per_seq)
        tail_ref[...] = h[h.shape[0] - 8:, :]
    else:
        hprev = hp_ref[...]
    delta = hprev - h
    xw = h + delta * mu_ref[4:5, :]
    xa = h + delta * mu_ref[5:6, :]
    wm_ref[...] = jnp.tanh(_mm(xw.astype(BF16), w1_ref[...]))
    am_ref[...] = _mm(xa.astype(BF16), a1_ref[...])
    if has_v:
        xv = h + delta * mu_ref[2:3, :]
        vm_ref[...] = _mm(xv.astype(BF16), v1_ref[...])
    else:
        vm_ref[...] = jnp.zeros(vm_ref.shape, F32)


def _lora(h, hprev, mu6, w1p, a1p, v1p, has_v, tiles_per_seq):
    m, d = h.shape
    row = pl.BlockSpec((ROW_TILE, d), lambda i: (i, 0))
    wspec = pl.BlockSpec((d, LORA_PAD), lambda i: (0, 0))
    ospec = pl.BlockSpec((ROW_TILE, LORA_PAD), lambda i: (i, 0))
    oshape = jax.ShapeDtypeStruct((m, LORA_PAD), F32)
    return pl.pallas_call(
        functools.partial(_lora_kernel, has_v, tiles_per_seq),
        grid=(m // ROW_TILE,),
        in_specs=[row, _prev_spec(hprev, tiles_per_seq, d, 1),
                  pl.BlockSpec(mu6.shape, lambda i: (0, 0)), wspec, wspec, wspec],
        out_specs=[ospec, ospec, ospec] + (
            [pl.BlockSpec((None, 8, d), lambda i: (i // tiles_per_seq, 0, 0))] if tiles_per_seq else []),
        out_shape=[oshape, oshape, oshape] + (
            [jax.ShapeDtypeStruct((m // (tiles_per_seq * ROW_TILE), 8, d), F32)] if tiles_per_seq else []),
        scratch_shapes=[pltpu.VMEM((8, d), F32)] if tiles_per_seq else [],
        compiler_params=pltpu.CompilerParams(dimension_semantics=("arbitrary",)),
        name="lora",
    )(h, hprev, mu6, w1p, a1p, v1p)


def _outproj_kernel(g_ref, w_ref, x_ref, npost_ref, npre_ref, xo_ref, *ho_ref):
    half = g_ref.shape[0] // 2
    w = w_ref[...]
    outs = [_mm(g_ref[rows, :], w) for rows in (slice(0, half), slice(half, 2 * half))]
    for rows, out in zip((slice(0, half), slice(half, 2 * half)), outs):
        ms = jnp.mean(out * out, axis=-1, keepdims=True)
        xn = x_ref[rows, :] + out * lax.rsqrt(ms + NORM_EPS) * npost_ref[...]
        xo_ref[rows, :] = xn
        if ho_ref:
            ms2 = jnp.mean(xn * xn, axis=-1, keepdims=True)
            ho_ref[0][rows, :] = (xn * lax.rsqrt(ms2 + NORM_EPS) * npre_ref[...]).astype(ho_ref[0].dtype)


def _outproj(g, w, layer_j, x, npost, npre_next, h_dtype):
    m, d = x.shape
    n_out = 1 if npre_next is None else 2
    if npre_next is None:
        npre_next = npost
    row = pl.BlockSpec((ROW_TILE, d), lambda i: (i, 0))
    vec = pl.BlockSpec((1, d), lambda i: (0, 0))
    out = pl.pallas_call(
        _outproj_kernel,
        grid=(m // ROW_TILE,),
        in_specs=[row, pl.BlockSpec((None, d, d), lambda i: (layer_j, 0, 0)), row, vec, vec],
        out_specs=[row, row][:n_out],
        out_shape=[jax.ShapeDtypeStruct((m, d), F32), jax.ShapeDtypeStruct((m, d), h_dtype)][:n_out],
        compiler_params=pltpu.CompilerParams(
            dimension_semantics=("arbitrary",), vmem_limit_bytes=VMEM_LIMIT),
        name="outproj",
    )(g, w, x, npost.reshape(1, d), npre_next.reshape(1, d))
    return (out[0], None) if n_out == 1 else out


def _rwkv_group(chunk, tvalid, has_vres, n_double, sl, grp, refs, wm, am, vm, ones_blk, diag):
    (r_ref, k_ref, v_ref, z_ref, vf_ref, w2_ref, a2_ref, v2_ref, prm_ref, g_ref, sbd_ref) = refs
    hw = RWKV_HEAD
    nh = HEADS_PER_GROUP
    prm = prm_ref[:, sl]
    w0, a0, v0 = prm[0:1], prm[1:2], prm[2:3]
    kk_w, ka_w, rk_w, ln_w, ln_b = prm[3:4], prm[4:5], prm[5:6], prm[6:7], prm[7:8]
    r = r_ref[:, sl]
    k = k_ref[:, sl]
    v = v_ref[:, sl]

    wl = w0 + _mm(wm, w2_ref[:, sl])
    al = a0 + _mm(am, a2_ref[:, sl])
    if has_vres:
        vl = v0 + _mm(vm, v2_ref[:, sl])
    yield
    logw = _log_decay(wl)
    if tvalid < chunk:
        logw = jnp.where(_iota(logw.shape, 0) < tvalid, logw, 0.0)
    alpha = _sigmoid(al)
    if has_vres:
        v = v + (vf_ref[:, sl] - v) * _sigmoid(vl)
    kk = k * kk_w
    k2 = k * (1.0 + (alpha - 1.0) * ka_w)
    kk_n2, bonus_s = _segsum_many([kk * kk, r * k2 * rk_w], ones_blk)
    cum = _cumsum_rows(logw)
    yield
    kk = kk / jnp.maximum(jnp.sqrt(kk_n2), 1e-12)
    b = kk * alpha
    a = -kk
    e_neg = jnp.exp(-cum)
    a_hat = a * jnp.exp(cum - logw)
    b_hat = b * e_neg
    k_hat = k2 * e_neg
    r_hat = r * jnp.exp(cum)
    cum_last = cum[chunk - 1:chunk, :]
    e_tail = jnp.exp(cum_last - cum)
    b_tail = b * e_tail
    k_tail = k2 * e_tail

    lhs = jnp.concatenate([a_hat, r_hat], axis=0).astype(BF16)
    gram_b = _mm_nt(lhs, _block_diag(b_hat, hw, nh))
    gram_k = _mm_nt(lhs, _block_diag(k_hat, hw, nh))
    s_bd = sbd_ref[grp]
    xy_state = _mm_nt(lhs, s_bd.astype(BF16))
    yield
    shape_cc = (chunk, nh * chunk)
    t_idx = _iota(shape_cc, 0)
    i_idx = jnp.bitwise_and(_iota(shape_cc, 1), chunk - 1)
    strict = i_idx < t_idx
    incl = i_idx <= t_idx
    n_ab = jnp.where(strict, gram_b[:chunk], 0.0)
    a_ak = jnp.where(strict, gram_k[:chunk], 0.0)
    a_rb = jnp.where(incl, gram_b[chunk:], 0.0)
    a_rk = jnp.where(incl, gram_k[chunk:], 0.0)
    v_bd = _block_diag(v, hw, nh)

    xy = xy_state + _mm(jnp.concatenate([a_ak, a_rk], axis=0).astype(BF16), v_bd)
    x = xy[:chunk]
    y_v = xy[chunk:]
    p = n_ab
    if n_double > 1:
        p_next = _mm(p.astype(BF16), _block_diag(p, chunk, nh))
    yield
    for j in range(n_double):
        x = x + _mm(p.astype(BF16), _block_diag(x, hw, nh))
        if j + 1 < n_double:
            p = p_next
            if j + 2 < n_double:
                p_next = _mm(p.astype(BF16), _block_diag(p, chunk, nh))
        yield
    u = x

    y = y_v + _mm(a_rb.astype(BF16), _block_diag(u, hw, nh))
    uv = jnp.concatenate([u, v], axis=0).astype(BF16)
    bk = jnp.concatenate([b_tail, k_tail], axis=0).astype(BF16)
    sbd_ref[grp] = s_bd * jnp.exp(cum_last) + jnp.where(diag, _mm_tn(uv, bk), 0.0)
    yield

    inv_n = 1.0 / RWKV_HEAD
    sum_y, sum_yy = _segsum_many([y, y * y], ones_blk, n_pieces=3)
    yield
    mean = sum_y * inv_n
    var = jnp.maximum(sum_yy * inv_n - mean * mean, 0.0)
    yn = (y - mean) * lax.rsqrt(var + RWKV_GN_EPS) * ln_w + ln_b
    z = z_ref[:, sl]
    g_ref[:, sl] = ((yn + bonus_s * v) * (z * _sigmoid(z))).astype(BF16)


def _rwkv_chunk_kernel(chunk, tvalid, has_vres, n_double, n_groups, fill,
                       r_ref, k_ref, v_ref, z_ref, wm_ref, am_ref, vm_ref, vf_ref,
                       w2_ref, a2_ref, v2_ref, prm_ref, s0_ref, ones_ref, *rest):
    g_ref, so_ref, sbd_ref = rest[-3:]
    c = pl.program_id(1)
    n_chunks = pl.num_programs(1)
    hw = RWKV_HEAD
    diag = (lax.shift_right_logical(_iota((GROUP, GROUP), 0), 6)
            == lax.shift_right_logical(_iota((GROUP, GROUP), 1), 6))

    @pl.when(c == 0)
    def _init():
        for g in range(n_groups):
            s4 = s0_ref[g * HEADS_PER_GROUP:(g + 1) * HEADS_PER_GROUP].reshape(GROUP, hw)
            tiled = jnp.concatenate([s4] * HEADS_PER_GROUP, axis=1)
            sbd_ref[g] = jnp.where(diag, tiled, 0.0)

    ones_blk = ones_ref[...]
    wm = wm_ref[...].astype(BF16)
    am = am_ref[...].astype(BF16)
    vm = vm_ref[...].astype(BF16)
    refs = (r_ref, k_ref, v_ref, z_ref, vf_ref, w2_ref, a2_ref, v2_ref, prm_ref, g_ref, sbd_ref)
    _round_robin([
        _rwkv_group(chunk, tvalid, has_vres, n_double, slice(g * GROUP, (g + 1) * GROUP), g, refs,
                    wm, am, vm, ones_blk, diag)
        for g in range(n_groups)])

    @pl.when(c == n_chunks - 1)
    def _fin():
        for g in range(n_groups):
            s_new = sbd_ref[g]
            dense = (s_new[:, 0:hw] + s_new[:, hw:2 * hw]
                     + s_new[:, 2 * hw:3 * hw] + s_new[:, 3 * hw:4 * hw])
            _store_state(so_ref, fill, (slice(g * HEADS_PER_GROUP, (g + 1) * HEADS_PER_GROUP),),
                         dense.reshape(HEADS_PER_GROUP, hw, hw))


def _rwkv_chunk(proj, vf_src, wm, am, vm, w2p, a2p, v2p, prm, s0, s_out_prev, *, layer_j, n_seq,
                n_chunks, chunk, tvalid, has_vres):
    m = n_seq * n_chunks * chunk
    d = proj.shape[1] // 4
    n_groups = d // GROUP
    n_double = max(1, math.ceil(math.log2(tvalid)))

    def col(off):
        return pl.BlockSpec((chunk, d), lambda s, c: (s * n_chunks + c, off))

    lora = pl.BlockSpec((chunk, LORA_PAD), lambda s, c: (s * n_chunks + c, 0))
    up = pl.BlockSpec((LORA_PAD, d), lambda s, c: (0, 0))
    state = pl.BlockSpec((None, None) + s0.shape[2:], lambda s, c: (layer_j, s, 0, 0, 0))
    prev = [] if s_out_prev is None else [s_out_prev]
    fill = None if prev else (layer_j, s0.shape[0])
    state_out = state if prev else pl.BlockSpec(
        (s0.shape[0], None) + s0.shape[2:], lambda s, c: (0, s, 0, 0, 0))
    ones_blk = jnp.where(
        lax.shift_right_logical(_iota((GROUP, GROUP), 0), 6)
        == lax.shift_right_logical(_iota((GROUP, GROUP), 1), 6), 1.0, 0.0).astype(BF16)
    kernel = functools.partial(_rwkv_chunk_kernel, chunk, tvalid, has_vres, n_double, n_groups, fill)
    return pl.pallas_call(
        kernel,
        grid=(n_seq, n_chunks),
        in_specs=[col(0), col(1), col(2), col(3), lora, lora, lora, col(2), up, up, up,
                  pl.BlockSpec((8, d), lambda s, c: (0, 0)), state,
                  pl.BlockSpec((GROUP, GROUP), lambda s, c: (0, 0))]
        + [pl.BlockSpec(memory_space=pl.ANY)] * len(prev),
        out_specs=[pl.BlockSpec((chunk, d), lambda s, c: (s * n_chunks + c, 0)), state_out],
        out_shape=[jax.ShapeDtypeStruct((m, d), BF16),
                   jax.ShapeDtypeStruct(s0.shape, F32)],
        scratch_shapes=[pltpu.VMEM((n_groups, GROUP, GROUP), F32)],
        input_output_aliases={14: 1} if prev else {},
        compiler_params=pltpu.CompilerParams(
            dimension_semantics=("arbitrary", "arbitrary"), vmem_limit_bytes=VMEM_LIMIT),
        name="rwkv_chunk",
    )(proj, proj, proj, proj, wm, am, vm, vf_src, w2p, a2p, v2p, prm, s0, ones_blk, *prev)


def _rwkv_step_kernel(n_tok, has_vres, fill,
                      r_ref, k_ref, v_ref, z_ref, wm_ref, am_ref, vm_ref, vf_ref,
                      w2_ref, a2_ref, v2_ref, prm_ref, s0_ref, ones_ref, *rest):
    g_ref, so_ref, tr_ref, y_ref = rest[-4:]
    hw = RWKV_HEAD
    n_seq = s0_ref.shape[-1]
    ones_blk = ones_ref[...]
    prm = prm_ref[...]
    w0, a0, v0 = prm[0:1], prm[1:2], prm[2:3]
    kk_w, ka_w, rk_w, ln_w, ln_b = prm[3:4], prm[4:5], prm[5:6], prm[6:7], prm[7:8]
    r = r_ref[...]
    k = k_ref[...]
    v = v_ref[...]

    wl = w0 + _mm(wm_ref[...].astype(BF16), w2_ref[...])
    decay = jnp.exp(_log_decay(wl))
    alpha = _sigmoid(a0 + _mm(am_ref[...].astype(BF16), a2_ref[...]))
    if has_vres:
        v = v + (vf_ref[...] - v) * _sigmoid(v0 + _mm(vm_ref[...].astype(BF16), v2_ref[...]))
    kk = k * kk_w
    k2 = k * (1.0 + (alpha - 1.0) * ka_w)
    kk_n2, bonus_s = _segsum_many([kk * kk, r * k2 * rk_w], ones_blk)
    kk = kk / jnp.maximum(jnp.sqrt(kk_n2), 1e-12)
    b = kk * alpha

    for idx, arr in enumerate((decay, k2, v, kk, b, r)):
        for t in range(n_tok):
            tr_ref[idx, t] = arr[t * n_seq:(t + 1) * n_seq, :].T

    def make_body(hh):
        lo = hh * hw

        def body(vi, carry):
            s = s0_ref[hh, vi]
            for t in range(n_tok):
                v_row = tr_ref[2, t, pl.ds(lo + vi, 1), :]
                sa = -jnp.sum(s * tr_ref[3, t, lo:lo + hw, :], axis=0, keepdims=True)
                s = (s * tr_ref[0, t, lo:lo + hw, :] + sa * tr_ref[4, t, lo:lo + hw, :]
                     + v_row * tr_ref[1, t, lo:lo + hw, :])
                y_ref[t, pl.ds(lo + vi, 1), :] = jnp.sum(
                    s * tr_ref[5, t, lo:lo + hw, :], axis=0, keepdims=True)
            _store_state(so_ref, fill, (hh, vi), s)
            return carry

        return body

    for hh in range(s0_ref.shape[0]):
        lax.fori_loop(0, hw, make_body(hh), 0, unroll=4)

    inv_n = 1.0 / hw
    for t in range(n_tok):
        rows = slice(t * n_seq, (t + 1) * n_seq)
        y = y_ref[t].T
        sum_y, sum_yy = _segsum_many([y, y * y], ones_blk, n_pieces=3)
        mean = sum_y * inv_n
        var = jnp.maximum(sum_yy * inv_n - mean * mean, 0.0)
        yn = (y - mean) * lax.rsqrt(var + RWKV_GN_EPS) * ln_w + ln_b
        z = z_ref[rows, :]
        g_ref[rows, :] = ((yn + bonus_s[rows] * v[rows]) * (z * _sigmoid(z))).astype(BF16)


def _rwkv_step(proj, vf_src, wm, am, vm, w2p, a2p, v2p, prm, s0, s_out_prev, *, layer_j, n_tok,
               has_vres):
    m = proj.shape[0]
    d = proj.shape[1] // 4
    pair = 2 * RWKV_HEAD
    n_pairs = d // pair
    n_seq = s0.shape[-1]

    def col(off):
        return pl.BlockSpec((m, pair), lambda hp: (0, off * n_pairs + hp))

    lora = pl.BlockSpec((m, LORA_PAD), lambda hp: (0, 0))
    up = pl.BlockSpec((LORA_PAD, pair), lambda hp: (0, hp))
    blk = (2, RWKV_HEAD, RWKV_HEAD, n_seq)
    state = pl.BlockSpec((None,) + blk, lambda hp: (layer_j, hp, 0, 0, 0))
    prev = [] if s_out_prev is None else [s_out_prev]
    fill = None if prev else (layer_j, s0.shape[0])
    state_out = state if prev else pl.BlockSpec((s0.shape[0],) + blk, lambda hp: (0, hp, 0, 0, 0))
    ones_blk = jnp.where(
        lax.shift_right_logical(_iota((pair, pair), 0), 6)
        == lax.shift_right_logical(_iota((pair, pair), 1), 6), 1.0, 0.0).astype(BF16)
    return pl.pallas_call(
        functools.partial(_rwkv_step_kernel, n_tok, has_vres, fill),
        grid=(n_pairs,),
        in_specs=[col(0), col(1), col(2), col(3), lora, lora, lora, col(2), up, up, up,
                  pl.BlockSpec((8, pair), lambda hp: (0, hp)), state,
                  pl.BlockSpec((pair, pair), lambda hp: (0, 0))]
        + [pl.BlockSpec(memory_space=pl.ANY)] * len(prev),
        out_specs=[pl.BlockSpec((m, pair), lambda hp: (0, hp)), state_out],
        out_shape=[jax.ShapeDtypeStruct((m, d), BF16), jax.ShapeDtypeStruct(s0.shape, F32)],
        scratch_shapes=[pltpu.VMEM((6, n_tok, pair, n_seq), F32),
                        pltpu.VMEM((n_tok, pair, n_seq), F32)],
        input_output_aliases={14: 1} if prev else {},
        compiler_params=pltpu.CompilerParams(
            dimension_semantics=("arbitrary",), vmem_limit_bytes=VMEM_LIMIT),
        name="rwkv_step",
    )(proj, proj, proj, proj, wm, am, vm, vf_src, w2p, a2p, v2p, prm, s0, ones_blk, *prev)


def _hgrn_head(chunk, tvalid, rows_sl, sl, head, refs, lb, nw, ones_blk):
    (q_ref, f_ref, i_ref, z_ref, st_ref) = refs
    sub = min(HGRN_SUB, chunk)
    shift = int(math.log2(sub))
    fl = f_ref[rows_sl, sl]
    qraw = q_ref[rows_sl, sl]
    val = i_ref[rows_sl, sl]
    f = lb + (1.0 - lb) * _sigmoid(fl)
    logf = jnp.log(jnp.maximum(f, GATE_FLOOR)) * LOG2_E
    if tvalid < chunk:
        logf = jnp.where(_iota(logf.shape, 0) < tvalid, logf, 0.0)
    kg = (1.0 - lb) * _sigmoid(-fl)
    qs = qraw * _sigmoid(qraw) * (HGRN_DK ** -0.5)
    gcum = _cumsum_rows(logf)
    yield

    st = st_ref[head]
    o_inter = _mm_nt((qs * jnp.exp2(gcum)).astype(BF16), st.astype(BF16))
    g_last = gcum[chunk - 1:chunk, :]
    k_tail = kg * jnp.exp2(g_last - gcum)
    st_ref[head] = st * jnp.exp2(g_last) + _mm_tn(val.astype(BF16), k_tail.astype(BF16))

    def bcast_rows(x):
        return jnp.concatenate(
            [jnp.broadcast_to(x[t:t + 1, :], (sub, HGRN_DK)) for t in range(tvalid)], axis=0)

    def tile_blocks(x):
        return jnp.concatenate(
            [x[(t >> shift) * sub:((t >> shift) + 1) * sub] for t in range(tvalid)], axis=0)

    rows = tvalid * sub
    prod = bcast_rows(qs) * tile_blocks(kg) * jnp.exp2(
        jnp.minimum(bcast_rows(gcum) - tile_blocks(gcum), 0.0))
    att_diag = _mm(prod.astype(BF16), ones_blk)
    n_blk = -(-tvalid // sub)
    att_off = []
    for bi in range(1, n_blk):
        lo = bi * sub
        g_edge = gcum[lo - 1:lo]
        q_n = (qs[lo:lo + sub] * jnp.exp2(gcum[lo:lo + sub] - g_edge)).astype(BF16)
        k_n = (kg[0:lo] * jnp.exp2(g_edge - gcum[0:lo])).astype(BF16)
        att_off.append(_mm_nt(q_n, k_n))
    yield

    ridx = _iota((rows, HGRN_DK), 0)
    causal = (jnp.bitwise_and(ridx, sub - 1)
              <= jnp.bitwise_and(lax.shift_right_logical(ridx, shift), sub - 1))
    weighted = jnp.where(causal, att_diag * tile_blocks(val), 0.0)
    sel = jnp.where(lax.shift_right_logical(_iota((chunk, rows), 1), shift) == _iota((chunk, rows), 0),
                    1.0, 0.0).astype(BF16)
    o = o_inter + _mm(sel, weighted.astype(BF16))
    if n_blk > 1:
        parts = [jnp.zeros((sub, HGRN_DK), F32)]
        for bi in range(1, n_blk):
            parts.append(_mm(att_off[bi - 1].astype(BF16), val[0:bi * sub].astype(BF16)))
        if n_blk * sub < chunk:
            parts.append(jnp.zeros((chunk - n_blk * sub, HGRN_DK), F32))
        o = o + jnp.concatenate(parts, axis=0)
    yield

    o = o * lax.rsqrt(jnp.mean(o * o, axis=-1, keepdims=True) + NORM_EPS) * nw
    z = z_ref[rows_sl, sl]
    return o * (z * _sigmoid(z))


def _hgrn_chunk_kernel(chunk, tvalid, layer_j, seqs, heads, fill,
                       q_ref, f_ref, i_ref, z_ref, lbl_ref, nw_ref, s0_ref, *rest):
    g_ref, so_ref, st_ref = rest[-3:]
    c = pl.program_id(2)
    n_chunks = pl.num_programs(2)

    @pl.when(c == 0)
    def _init():
        for q in range(seqs):
            for hh in range(heads):
                st_ref[q * heads + hh] = s0_ref[q, hh].T

    logits = lbl_ref[...]
    ex = jnp.exp(logits - jnp.max(logits, axis=0, keepdims=True))
    soft = ex / jnp.sum(ex, axis=0, keepdims=True)
    lb_all = jnp.sum(soft[0:layer_j + 1], axis=0, keepdims=True) - soft[0:1]

    ones_blk = jnp.ones((HGRN_DK, HGRN_DK), BF16)
    nw = nw_ref[...]
    refs = (q_ref, f_ref, i_ref, z_ref, st_ref)
    outs = _round_robin([
        _hgrn_head(chunk, tvalid, slice(q * chunk, (q + 1) * chunk),
                   slice(hh * HGRN_DK, (hh + 1) * HGRN_DK), q * heads + hh, refs,
                   lb_all[:, hh * HGRN_DK:(hh + 1) * HGRN_DK], nw, ones_blk)
        for q in range(seqs) for hh in range(heads)])
    for hh in range(heads):
        col = [outs[q * heads + hh] for q in range(seqs)]
        g_ref[:, hh * HGRN_DK:(hh + 1) * HGRN_DK] = (
            col[0] if seqs == 1 else jnp.concatenate(col, axis=0)).astype(BF16)

    @pl.when(c == n_chunks - 1)
    def _fin():
        for q in range(seqs):
            for hh in range(heads):
                _store_state(so_ref, fill, (q, hh), st_ref[q * heads + hh].T)


def _hgrn_chunk(proj, lb_logits, norm_w, s0, s_out_prev, *, layer_j, n_seq, n_chunks, chunk, tvalid,
                seqs=1):
    assert seqs == 1 or n_chunks == 1
    m = n_seq * n_chunks * chunk
    d = proj.shape[1] // 4
    n_heads = d // HGRN_DK
    heads = HGRN_HEADS_PER_STEP
    n_hb = n_heads // heads
    width = heads * HGRN_DK

    def col(off):
        return pl.BlockSpec((seqs * chunk, width), lambda s, h, c: (s * n_chunks + c, off * n_hb + h))

    state = pl.BlockSpec((None, seqs, heads, HGRN_DK, HGRN_DK), lambda s, h, c: (layer_j, s, h, 0, 0))
    prev = [] if s_out_prev is None else [s_out_prev]
    fill = None if prev else (layer_j, s0.shape[0])
    state_out = state if prev else pl.BlockSpec(
        (s0.shape[0], seqs, heads, HGRN_DK, HGRN_DK), lambda s, h, c: (0, s, h, 0, 0))
    kernel = functools.partial(_hgrn_chunk_kernel, chunk, tvalid, layer_j, seqs, heads, fill)
    return pl.pallas_call(
        kernel,
        grid=(n_seq // seqs, n_hb, n_chunks),
        in_specs=[col(0), col(1), col(2), col(3),
                  pl.BlockSpec((lb_logits.shape[0], width), lambda s, h, c: (0, h)),
                  pl.BlockSpec((1, HGRN_DK), lambda s, h, c: (0, 0)), state]
        + [pl.BlockSpec(memory_space=pl.ANY)] * len(prev),
        out_specs=[pl.BlockSpec((seqs * chunk, width), lambda s, h, c: (s * n_chunks + c, h)),
                   state_out],
        out_shape=[jax.ShapeDtypeStruct((m, d), BF16),
                   jax.ShapeDtypeStruct(s0.shape, F32)],
        scratch_shapes=[pltpu.VMEM((seqs * heads, HGRN_DK, HGRN_DK), F32)],
        input_output_aliases={7: 1} if prev else {},
        compiler_params=pltpu.CompilerParams(
            dimension_semantics=("arbitrary", "arbitrary", "arbitrary"),
            vmem_limit_bytes=VMEM_LIMIT),
        name="hgrn_chunk",
    )(proj, proj, proj, proj, lb_logits, norm_w.reshape(1, HGRN_DK), s0, *prev)


def _pad_lanes(w, axis):
    pad = [(0, 0)] * w.ndim
    pad[axis] = (0, LORA_PAD - w.shape[axis])
    return jnp.pad(w, pad)


def _rwkv_layer_inputs(h, hprev, p, j, tiles_per_seq):
    has_vres = j > 0
    proj = _proj_rwkv(h, hprev, p["mu4"][j], p["w_in"], j, tiles_per_seq)
    lora = _lora(h, hprev, p["mu6"][j], p["w1"][j], p["a1"][j], p["v1"][max(j - 1, 0)],
                 has_vres, tiles_per_seq)
    return (proj,) + tuple(lora) + (has_vres,)


def _trunk_prompt(x, shift0, wkv0, hgrn0, p, *, n_seq, n_tok):
    d = x.shape[1]
    n_chunks = n_tok // PROMPT_CHUNK
    shifts = []
    wkv_out, hgrn_out, vf_src = None, None, None
    h = _rmsnorm(x, p["norm_pre"][0])
    for layer in range(DEPTH):
        j = layer // 2
        npre_next = p["norm_pre"][layer + 1] if layer + 1 < DEPTH else None
        if layer % 2 == 0:
            proj, wm, am, vm, tail, has_vres = _rwkv_layer_inputs(
                h, shift0[j][:, None, :], p, j, n_tok // ROW_TILE)
            shifts.append(tail[:, 7])
            vf_src = proj if vf_src is None else vf_src
            g, wkv_out = _rwkv_chunk(
                proj, vf_src, wm, am, vm, p["w2"][j], p["a2"][j], p["v2"][max(j - 1, 0)],
                p["rwkv_prm"][j], wkv0, wkv_out, layer_j=j, n_seq=n_seq, n_chunks=n_chunks,
                chunk=PROMPT_CHUNK, tvalid=PROMPT_CHUNK, has_vres=has_vres)
            w_o = p["rwkv_w_o"]
        else:
            proj = _proj_hgrn(h, p["hgrn_w_in"], j)
            g, hgrn_out = _hgrn_chunk(
                proj, p["hgrn_lb_logits"], p["hgrn_norm_w"][j], hgrn0, hgrn_out, layer_j=j,
                n_seq=n_seq, n_chunks=n_chunks, chunk=PROMPT_CHUNK, tvalid=PROMPT_CHUNK)
            w_o = p["hgrn_w_o"]
        x, h = _outproj(g, w_o, j, x, p["norm_post"][layer], npre_next,
                        BF16 if layer % 2 == 0 else F32)
    return x, jnp.stack(shifts), wkv_out, hgrn_out


def _trunk_sample(x, shift0, wkv0, hgrn0, p, *, n_seq, n_tok):
    d = x.shape[1]
    tpad = SAMPLE_TPAD

    def to_padded(a):
        a3 = a.reshape(n_tok, n_seq, a.shape[1]).transpose(1, 0, 2)
        return jnp.pad(a3, ((0, 0), (0, tpad - n_tok), (0, 0))).reshape(n_seq * tpad, a.shape[1])

    def from_padded(a):
        a3 = a.reshape(n_seq, tpad, a.shape[1])[:, :n_tok]
        return a3.transpose(1, 0, 2).reshape(n_tok * n_seq, a.shape[1])

    shifts = []
    wkv_out, hgrn_out, vf_src = None, None, None
    h = _rmsnorm(x, p["norm_pre"][0])
    for layer in range(DEPTH):
        j = layer // 2
        npre_next = p["norm_pre"][layer + 1] if layer + 1 < DEPTH else None
        if layer % 2 == 0:
            hprev = jnp.concatenate([shift0[j], h[:(n_tok - 1) * n_seq]], axis=0)
            shifts.append(h[(n_tok - 1) * n_seq:])
            proj, wm, am, vm, has_vres = _rwkv_layer_inputs(h, hprev, p, j, 0)
            vf_src = proj if vf_src is None else vf_src
            g, wkv_out = _rwkv_step(
                proj, vf_src, wm, am, vm, p["w2"][j], p["a2"][j], p["v2"][max(j - 1, 0)],
                p["rwkv_prm"][j], wkv0, wkv_out, layer_j=j, n_tok=n_tok, has_vres=has_vres)
            w_o = p["rwkv_w_o"]
        else:
            proj = to_padded(_proj_hgrn(h, p["hgrn_w_in"], j))
            g, hgrn_out = _hgrn_chunk(
                proj, p["hgrn_lb_logits"], p["hgrn_norm_w"][j], hgrn0, hgrn_out, layer_j=j,
                n_seq=n_seq, n_chunks=1, chunk=tpad, tvalid=n_tok, seqs=HGRN_SAMPLE_SEQS)
            g = from_padded(g)
            w_o = p["hgrn_w_o"]
        x, h = _outproj(g, w_o, j, x, p["norm_post"][layer], npre_next,
                        BF16 if layer % 2 == 0 else F32)
    return x, jnp.stack(shifts), wkv_out, hgrn_out


def kernel(x_prompt, x_sample, state_rwkv_shift, state_rwkv_wkv, state_hgrn, norm_pre, norm_post,
           rwkv_mu, rwkv_w_in, rwkv_w0, rwkv_w1, rwkv_w2, rwkv_a0, rwkv_a1, rwkv_a2, rwkv_v0,
           rwkv_v1, rwkv_v2, rwkv_k_k, rwkv_k_a, rwkv_r_k, rwkv_ln_w, rwkv_ln_b, rwkv_w_o,
           hgrn_w_in, hgrn_lb_logits, hgrn_norm_w, hgrn_w_o):
    n_rwkv = rwkv_mu.shape[0]
    d = x_prompt.shape[-1]
    v0_full = jnp.concatenate([jnp.zeros((1, d), F32), rwkv_v0], axis=0)
    rwkv_prm = jnp.stack([rwkv_w0, rwkv_a0, v0_full, rwkv_k_k, rwkv_k_a,
                          rwkv_r_k.reshape(n_rwkv, d), rwkv_ln_w, rwkv_ln_b], axis=1)
    p = {
        "norm_pre": norm_pre, "norm_post": norm_post,
        "mu4": rwkv_mu[:, :4, None, :], "mu6": rwkv_mu,
        "w_in": rwkv_w_in.astype(BF16),
        "w1": _pad_lanes(rwkv_w1, 2).astype(BF16), "a1": _pad_lanes(rwkv_a1, 2).astype(BF16),
        "v1": _pad_lanes(rwkv_v1, 2).astype(BF16),
        "w2": _pad_lanes(rwkv_w2, 1).astype(BF16), "a2": _pad_lanes(rwkv_a2, 1).astype(BF16),
        "v2": _pad_lanes(rwkv_v2, 1).astype(BF16),
        "rwkv_prm": rwkv_prm, "rwkv_w_o": rwkv_w_o.astype(BF16),
        "hgrn_w_in": hgrn_w_in.astype(BF16), "hgrn_lb_logits": hgrn_lb_logits,
        "hgrn_norm_w": hgrn_norm_w, "hgrn_w_o": hgrn_w_o.astype(BF16),
    }

    bp, tp, _ = x_prompt.shape
    zero_shift = jnp.zeros((n_rwkv, bp, d), F32)
    zero_wkv = jnp.zeros((n_rwkv, bp) + state_rwkv_wkv.shape[2:], F32)
    zero_hgrn = jnp.zeros((state_hgrn.shape[0], bp) + state_hgrn.shape[2:], F32)
    y_p, p_shift, p_wkv, p_hgrn = _trunk_prompt(
        x_prompt.reshape(bp * tp, d), zero_shift, zero_wkv, zero_hgrn, p, n_seq=bp, n_tok=tp)

    bs, ts, _ = x_sample.shape
    y_s, s_shift, s_wkv, s_hgrn = _trunk_sample(
        x_sample.transpose(1, 0, 2).reshape(ts * bs, d), state_rwkv_shift,
        state_rwkv_wkv.transpose(0, 2, 3, 4, 1), state_hgrn, p, n_seq=bs, n_tok=ts)

    return (y_p.reshape(bp, tp, d), y_s.reshape(ts, bs, d).transpose(1, 0, 2),
            p_shift, p_wkv, p_hgrn, s_shift, s_wkv.transpose(0, 4, 1, 2, 3), s_hgrn)
```

```python
import functools
import math

import jax
import jax.numpy as jnp
from jax import lax
from jax.experimental import pallas as pl
from jax.experimental.pallas import tpu as pltpu

F32 = jnp.float32
BF16 = jnp.bfloat16

D_MODEL = 2048
DEPTH = 4
RWKV_HEAD = 64
HGRN_DK = 128
RWKV_GN_EPS = 1e-5 * RWKV_HEAD
NORM_EPS = 1e-6
GATE_FLOOR = 1e-30
LOG2_E = 1.0 / math.log(2.0)

LANES = 128
GROUP = 256
HEADS_PER_GROUP = GROUP // RWKV_HEAD
LORA_PAD = 128
PROMPT_CHUNK = 64
SAMPLE_TPAD = 8
HGRN_SUB = 8
HGRN_HEADS_PER_STEP = 16
HGRN_SAMPLE_SEQS = 4
ROW_TILE = 512
VMEM_LIMIT = 48 * 1024 * 1024


def _mm(a, b):
    return jnp.dot(a, b, preferred_element_type=F32)


def _mm_nt(a, b):
    return lax.dot_general(a, b, (((1,), (1,)), ((), ())), preferred_element_type=F32)


def _mm_tn(a, b):
    return lax.dot_general(a, b, (((0,), (0,)), ((), ())), preferred_element_type=F32)


def _iota(shape, dim):
    return lax.broadcasted_iota(jnp.int32, shape, dim)


def _split2(x):
    hi = x.astype(BF16)
    lo = (x - hi.astype(F32)).astype(BF16)
    return hi, lo


def _segsum(x, ones_blk):
    hi, lo = _split2(x)
    return _mm(hi, ones_blk) + _mm(lo, ones_blk)


def _split3(x):
    x1 = x.astype(BF16)
    r1 = x - x1.astype(F32)
    x2 = r1.astype(BF16)
    return x1, x2, (r1 - x2.astype(F32)).astype(BF16)


def _segsum_many(xs, ones_blk, n_pieces=2):
    split = _split2 if n_pieces == 2 else _split3
    pieces = []
    for x in xs:
        pieces.extend(split(x))
    out = _mm(jnp.concatenate(pieces, axis=0), ones_blk)
    n = xs[0].shape[0]
    sums = []
    for i in range(len(xs)):
        parts = [out[(n_pieces * i + j) * n:(n_pieces * i + j + 1) * n] for j in range(n_pieces)]
        sums.append(functools.reduce(lambda a, b: a + b, parts))
    return sums


def _cumsum_rows(x):
    n = x.shape[0]
    tri = jnp.where(_iota((n, n), 0) >= _iota((n, n), 1), 1.0, 0.0).astype(BF16)
    x1, x2, x3 = _split3(x)
    return _mm(tri, x1) + _mm(tri, x2) + _mm(tri, x3)


def _block_diag(x, head_width, n_heads):
    shift = int(math.log2(head_width))
    lane_head = lax.shift_right_logical(_iota(x.shape, 1), shift)
    parts = [jnp.where(lane_head == h, x, 0.0).astype(BF16) for h in range(n_heads)]
    return jnp.concatenate(parts, axis=0)


def _sigmoid(x):
    return jax.nn.sigmoid(x)


def _log_decay(wl):
    return _sigmoid(wl) * (-math.exp(-0.5))


def _store_state(so_ref, fill, idx, value):
    if fill is None:
        so_ref[idx] = value
        return
    layer, n_layers = fill
    for other in range(n_layers):
        so_ref[(other,) + idx] = value if other == layer else jnp.zeros_like(value)


def _round_robin(gens):
    results = [None] * len(gens)
    active = list(range(len(gens)))
    while active:
        for idx in list(active):
            try:
                next(gens[idx])
            except StopIteration as stop:
                results[idx] = stop.value
                active.remove(idx)
    return results


def _rmsnorm_kernel(x_ref, w_ref, o_ref):
    x = x_ref[...]
    ms = jnp.mean(x * x, axis=-1, keepdims=True)
    o_ref[...] = x * lax.rsqrt(ms + NORM_EPS) * w_ref[...]


def _rmsnorm(x, w):
    m, d = x.shape
    return pl.pallas_call(
        _rmsnorm_kernel,
        grid=(m // ROW_TILE,),
        in_specs=[pl.BlockSpec((ROW_TILE, d), lambda i: (i, 0)),
                  pl.BlockSpec((1, d), lambda i: (0, 0))],
        out_specs=pl.BlockSpec((ROW_TILE, d), lambda i: (i, 0)),
        out_shape=jax.ShapeDtypeStruct((m, d), F32),
        name="rmsnorm",
    )(x, w.reshape(1, d))


def _prev_rows(h, first_ref, carry_ref, tile, tiles_per_seq):
    rolled = pltpu.roll(h, 1, axis=0)
    seq_start = lax.rem(tile, tiles_per_seq) == 0
    edge = jnp.where(seq_start, first_ref[...], carry_ref[7:8, :])
    carry_ref[...] = h[h.shape[0] - 8:, :]
    return jnp.where(_iota(h.shape, 0) == 0, edge, rolled)


def _proj_mix_kernel(tiles_per_seq, h_ref, hp_ref, mu_ref, w_ref, o_ref, *carry):
    h = h_ref[...]
    if tiles_per_seq:
        hprev = _prev_rows(h, hp_ref, carry[0], pl.program_id(1), tiles_per_seq)
    else:
        hprev = hp_ref[...]
    xs = h + (hprev - h) * mu_ref[...]
    o_ref[...] = _mm(xs.astype(BF16), w_ref[...])


def _proj_plain_kernel(h_ref, w_ref, o_ref):
    o_ref[...] = _mm(h_ref[...].astype(BF16), w_ref[...])


def _prev_spec(hprev, tiles_per_seq, d, grid_rank):
    if tiles_per_seq:
        if grid_rank == 2:
            return pl.BlockSpec((None, 1, d), lambda n, i: (i // tiles_per_seq, 0, 0))
        return pl.BlockSpec((None, 1, d), lambda i: (i // tiles_per_seq, 0, 0))
    if grid_rank == 2:
        return pl.BlockSpec((ROW_TILE, d), lambda n, i: (i, 0))
    return pl.BlockSpec((ROW_TILE, d), lambda i: (i, 0))


def _proj_rwkv(h, hprev, mu4, w4, layer_j, tiles_per_seq):
    m, d = h.shape
    n_proj = w4.shape[1]
    return pl.pallas_call(
        functools.partial(_proj_mix_kernel, tiles_per_seq),
        grid=(n_proj, m // ROW_TILE),
        in_specs=[pl.BlockSpec((ROW_TILE, d), lambda n, i: (i, 0)),
                  _prev_spec(hprev, tiles_per_seq, d, 2),
                  pl.BlockSpec((None, 1, d), lambda n, i: (n, 0, 0)),
                  pl.BlockSpec((None, None, d, d), lambda n, i: (layer_j, n, 0, 0))],
        out_specs=pl.BlockSpec((ROW_TILE, d), lambda n, i: (i, n)),
        out_shape=jax.ShapeDtypeStruct((m, n_proj * d), F32),
        scratch_shapes=[pltpu.VMEM((8, d), F32)] if tiles_per_seq else [],
        compiler_params=pltpu.CompilerParams(
            dimension_semantics=("arbitrary", "arbitrary"), vmem_limit_bytes=VMEM_LIMIT),
        name="proj_rwkv",
    )(h, hprev, mu4, w4)


def _proj_hgrn(h, w, layer_j):
    m, d = h.shape
    n_proj = w.shape[2] // d
    return pl.pallas_call(
        _proj_plain_kernel,
        grid=(n_proj, m // ROW_TILE),
        in_specs=[pl.BlockSpec((ROW_TILE, d), lambda n, i: (i, 0)),
                  pl.BlockSpec((None, d, d), lambda n, i: (layer_j, 0, n))],
        out_specs=pl.BlockSpec((ROW_TILE, d), lambda n, i: (i, n)),
        out_shape=jax.ShapeDtypeStruct((m, n_proj * d), F32),
        compiler_params=pltpu.CompilerParams(
            dimension_semantics=("arbitrary", "arbitrary"), vmem_limit_bytes=VMEM_LIMIT),
        name="proj_hgrn",
    )(h, w)


def _lora_kernel(has_v, tiles_per_seq, h_ref, hp_ref, mu_ref, w1_ref, a1_ref, v1_ref,
                 wm_ref, am_ref, vm_ref, *rest):
    h = h_ref[...]
    if tiles_per_seq:
        tail_ref, carry_ref = rest
        hprev = _prev_rows(h, hp_ref, carry_ref, pl.program_id(0), tiles_per_seq)
        tail_ref[...] = h[h.shape[0] - 8:, :]
    else:
        hprev = hp_ref[...]
    delta = hprev - h
    xw = h + delta * mu_ref[4:5, :]
    xa = h + delta * mu_ref[5:6, :]
    wm_ref[...] = jnp.tanh(_mm(xw.astype(BF16), w1_ref[...]))
    am_ref[...] = _mm(xa.astype(BF16), a1_ref[...])
    if has_v:
        xv = h + delta * mu_ref[2:3, :]
        vm_ref[...] = _mm(xv.astype(BF16), v1_ref[...])
    else:
        vm_ref[...] = jnp.zeros(vm_ref.shape, F32)


def _lora(h, hprev, mu6, w1p, a1p, v1p, has_v, tiles_per_seq):
    m, d = h.shape
    row = pl.BlockSpec((ROW_TILE, d), lambda i: (i, 0))
    wspec = pl.BlockSpec((d, LORA_PAD), lambda i: (0, 0))
    ospec = pl.BlockSpec((ROW_TILE, LORA_PAD), lambda i: (i, 0))
    oshape = jax.ShapeDtypeStruct((m, LORA_PAD), F32)
    return pl.pallas_call(
        functools.partial(_lora_kernel, has_v, tiles_per_seq),
        grid=(m // ROW_TILE,),
        in_specs=[row, _prev_spec(hprev, tiles_per_seq, d, 1),
                  pl.BlockSpec(mu6.shape, lambda i: (0, 0)), wspec, wspec, wspec],
        out_specs=[ospec, ospec, ospec] + (
            [pl.BlockSpec((None, 8, d), lambda i: (i // tiles_per_seq, 0, 0))] if tiles_per_seq else []),
        out_shape=[oshape, oshape, oshape] + (
            [jax.ShapeDtypeStruct((m // (tiles_per_seq * ROW_TILE), 8, d), F32)] if tiles_per_seq else []),
        scratch_shapes=[pltpu.VMEM((8, d), F32)] if tiles_per_seq else [],
        compiler_params=pltpu.CompilerParams(dimension_semantics=("arbitrary",)),
        name="lora",
    )(h, hprev, mu6, w1p, a1p, v1p)


def _outproj_kernel(g_ref, w_ref, x_ref, npost_ref, npre_ref, xo_ref, *ho_ref):
    part = g_ref.shape[0] // 4
    w = w_ref[...]
    parts = [slice(i * part, (i + 1) * part) for i in range(4)]
    outs = [_mm(g_ref[rows, :], w) for rows in parts]
    for rows, out in zip(parts, outs):
        ms = jnp.mean(out * out, axis=-1, keepdims=True)
        xn = x_ref[rows, :] + out * lax.rsqrt(ms + NORM_EPS) * npost_ref[...]
        xo_ref[rows, :] = xn
        if ho_ref:
            ms2 = jnp.mean(xn * xn, axis=-1, keepdims=True)
            ho_ref[0][rows, :] = (xn * lax.rsqrt(ms2 + NORM_EPS) * npre_ref[...]).astype(ho_ref[0].dtype)


def _outproj(g, w, layer_j, x, npost, npre_next, h_dtype):
    m, d = x.shape
    n_out = 1 if npre_next is None else 2
    if npre_next is None:
        npre_next = npost
    row = pl.BlockSpec((ROW_TILE, d), lambda i: (i, 0))
    vec = pl.BlockSpec((1, d), lambda i: (0, 0))
    out = pl.pallas_call(
        _outproj_kernel,
        grid=(m // ROW_TILE,),
        in_specs=[row, pl.BlockSpec((None, d, d), lambda i: (layer_j, 0, 0)), row, vec, vec],
        out_specs=[row, row][:n_out],
        out_shape=[jax.ShapeDtypeStruct((m, d), F32), jax.ShapeDtypeStruct((m, d), h_dtype)][:n_out],
        compiler_params=pltpu.CompilerParams(
            dimension_semantics=("arbitrary",), vmem_limit_bytes=VMEM_LIMIT),
        name="outproj",
    )(g, w, x, npost.reshape(1, d), npre_next.reshape(1, d))
    return (out[0], None) if n_out == 1 else out


def _rwkv_group(chunk, tvalid, has_vres, n_double, sl, grp, refs, wm, am, vm, ones_blk, diag):
    (r_ref, k_ref, v_ref, z_ref, vf_ref, w2_ref, a2_ref, v2_ref, prm_ref, g_ref, sbd_ref) = refs
    hw = RWKV_HEAD
    nh = HEADS_PER_GROUP
    prm = prm_ref[:, sl]
    w0, a0, v0 = prm[0:1], prm[1:2], prm[2:3]
    kk_w, ka_w, rk_w, ln_w, ln_b = prm[3:4], prm[4:5], prm[5:6], prm[6:7], prm[7:8]
    r = r_ref[:, sl]
    k = k_ref[:, sl]
    v = v_ref[:, sl]

    wl = w0 + _mm(wm, w2_ref[:, sl])
    al = a0 + _mm(am, a2_ref[:, sl])
    if has_vres:
        vl = v0 + _mm(vm, v2_ref[:, sl])
    yield
    logw = _log_decay(wl)
    if tvalid < chunk:
        logw = jnp.where(_iota(logw.shape, 0) < tvalid, logw, 0.0)
    alpha = _sigmoid(al)
    if has_vres:
        v = v + (vf_ref[:, sl] - v) * _sigmoid(vl)
    kk = k * kk_w
    k2 = k * (1.0 + (alpha - 1.0) * ka_w)
    kk_n2, bonus_s = _segsum_many([kk * kk, r * k2 * rk_w], ones_blk)
    cum = _cumsum_rows(logw)
    yield
    kk = kk / jnp.maximum(jnp.sqrt(kk_n2), 1e-12)
    b = kk * alpha
    a = -kk
    e_neg = jnp.exp(-cum)
    a_hat = a * jnp.exp(cum - logw)
    b_hat = b * e_neg
    k_hat = k2 * e_neg
    r_hat = r * jnp.exp(cum)
    cum_last = cum[chunk - 1:chunk, :]
    e_tail = jnp.exp(cum_last - cum)
    b_tail = b * e_tail
    k_tail = k2 * e_tail

    lhs = jnp.concatenate([a_hat, r_hat], axis=0).astype(BF16)
    gram_b = _mm_nt(lhs, _block_diag(b_hat, hw, nh))
    gram_k = _mm_nt(lhs, _block_diag(k_hat, hw, nh))
    s_bd = sbd_ref[grp]
    xy_state = _mm_nt(lhs, s_bd.astype(BF16))
    yield
    shape_cc = (chunk, nh * chunk)
    t_idx = _iota(shape_cc, 0)
    i_idx = jnp.bitwise_and(_iota(shape_cc, 1), chunk - 1)
    strict = i_idx < t_idx
    incl = i_idx <= t_idx
    n_ab = jnp.where(strict, gram_b[:chunk], 0.0)
    a_ak = jnp.where(strict, gram_k[:chunk], 0.0)
    a_rb = jnp.where(incl, gram_b[chunk:], 0.0)
    a_rk = jnp.where(incl, gram_k[chunk:], 0.0)
    v_bd = _block_diag(v, hw, nh)

    xy = xy_state + _mm(jnp.concatenate([a_ak, a_rk], axis=0).astype(BF16), v_bd)
    x = xy[:chunk]
    y_v = xy[chunk:]
    p = n_ab
    if n_double > 1:
        p_next = _mm(p.astype(BF16), _block_diag(p, chunk, nh))
    yield
    for j in range(n_double):
        x = x + _mm(p.astype(BF16), _block_diag(x, hw, nh))
        if j + 1 < n_double:
            p = p_next
            if j + 2 < n_double:
                p_next = _mm(p.astype(BF16), _block_diag(p, chunk, nh))
        yield
    u = x

    y = y_v + _mm(a_rb.astype(BF16), _block_diag(u, hw, nh))
    uv = jnp.concatenate([u, v], axis=0).astype(BF16)
    bk = jnp.concatenate([b_tail, k_tail], axis=0).astype(BF16)
    sbd_ref[grp] = s_bd * jnp.exp(cum_last) + jnp.where(diag, _mm_tn(uv, bk), 0.0)
    yield

    inv_n = 1.0 / RWKV_HEAD
    sum_y, sum_yy = _segsum_many([y, y * y], ones_blk, n_pieces=3)
    yield
    mean = sum_y * inv_n
    var = jnp.maximum(sum_yy * inv_n - mean * mean, 0.0)
    yn = (y - mean) * lax.rsqrt(var + RWKV_GN_EPS) * ln_w + ln_b
    z = z_ref[:, sl]
    g_ref[:, sl] = ((yn + bonus_s * v) * (z * _sigmoid(z))).astype(BF16)


def _rwkv_chunk_kernel(chunk, tvalid, has_vres, n_double, n_groups, fill,
                       r_ref, k_ref, v_ref, z_ref, wm_ref, am_ref, vm_ref, vf_ref,
                       w2_ref, a2_ref, v2_ref, prm_ref, s0_ref, ones_ref, *rest):
    g_ref, so_ref, sbd_ref = rest[-3:]
    c = pl.program_id(1)
    n_chunks = pl.num_programs(1)
    hw = RWKV_HEAD
    diag = (lax.shift_right_logical(_iota((GROUP, GROUP), 0), 6)
            == lax.shift_right_logical(_iota((GROUP, GROUP), 1), 6))

    @pl.when(c == 0)
    def _init():
        for g in range(n_groups):
            s4 = s0_ref[g * HEADS_PER_GROUP:(g + 1) * HEADS_PER_GROUP].reshape(GROUP, hw)
            tiled = jnp.concatenate([s4] * HEADS_PER_GROUP, axis=1)
            sbd_ref[g] = jnp.where(diag, tiled, 0.0)

    ones_blk = ones_ref[...]
    wm = wm_ref[...].astype(BF16)
    am = am_ref[...].astype(BF16)
    vm = vm_ref[...].astype(BF16)
    refs = (r_ref, k_ref, v_ref, z_ref, vf_ref, w2_ref, a2_ref, v2_ref, prm_ref, g_ref, sbd_ref)
    _round_robin([
        _rwkv_group(chunk, tvalid, has_vres, n_double, slice(g * GROUP, (g + 1) * GROUP), g, refs,
                    wm, am, vm, ones_blk, diag)
        for g in range(n_groups)])

    @pl.when(c == n_chunks - 1)
    def _fin():
        for g in range(n_groups):
            s_new = sbd_ref[g]
            dense = (s_new[:, 0:hw] + s_new[:, hw:2 * hw]
                     + s_new[:, 2 * hw:3 * hw] + s_new[:, 3 * hw:4 * hw])
            _store_state(so_ref, fill, (slice(g * HEADS_PER_GROUP, (g + 1) * HEADS_PER_GROUP),),
                         dense.reshape(HEADS_PER_GROUP, hw, hw))


def _rwkv_chunk(proj, vf_src, wm, am, vm, w2p, a2p, v2p, prm, s0, s_out_prev, *, layer_j, n_seq,
                n_chunks, chunk, tvalid, has_vres):
    m = n_seq * n_chunks * chunk
    d = proj.shape[1] // 4
    n_groups = d // GROUP
    n_double = max(1, math.ceil(math.log2(tvalid)))

    def col(off):
        return pl.BlockSpec((chunk, d), lambda s, c: (s * n_chunks + c, off))

    lora = pl.BlockSpec((chunk, LORA_PAD), lambda s, c: (s * n_chunks + c, 0))
    up = pl.BlockSpec((LORA_PAD, d), lambda s, c: (0, 0))
    state = pl.BlockSpec((None, None) + s0.shape[2:], lambda s, c: (layer_j, s, 0, 0, 0))
    prev = [] if s_out_prev is None else [s_out_prev]
    fill = None if prev else (layer_j, s0.shape[0])
    state_out = state if prev else pl.BlockSpec(
        (s0.shape[0], None) + s0.shape[2:], lambda s, c: (0, s, 0, 0, 0))
    ones_blk = jnp.where(
        lax.shift_right_logical(_iota((GROUP, GROUP), 0), 6)
        == lax.shift_right_logical(_iota((GROUP, GROUP), 1), 6), 1.0, 0.0).astype(BF16)
    kernel = functools.partial(_rwkv_chunk_kernel, chunk, tvalid, has_vres, n_double, n_groups, fill)
    return pl.pallas_call(
        kernel,
        grid=(n_seq, n_chunks),
        in_specs=[col(0), col(1), col(2), col(3), lora, lora, lora, col(2), up, up, up,
                  pl.BlockSpec((8, d), lambda s, c: (0, 0)), state,
                  pl.BlockSpec((GROUP, GROUP), lambda s, c: (0, 0))]
        + [pl.BlockSpec(memory_space=pl.ANY)] * len(prev),
        out_specs=[pl.BlockSpec((chunk, d), lambda s, c: (s * n_chunks + c, 0)), state_out],
        out_shape=[jax.ShapeDtypeStruct((m, d), BF16),
                   jax.ShapeDtypeStruct(s0.shape, F32)],
        scratch_shapes=[pltpu.VMEM((n_groups, GROUP, GROUP), F32)],
        input_output_aliases={14: 1} if prev else {},
        compiler_params=pltpu.CompilerParams(
            dimension_semantics=("arbitrary", "arbitrary"), vmem_limit_bytes=VMEM_LIMIT),
        name="rwkv_chunk",
    )(proj, proj, proj, proj, wm, am, vm, vf_src, w2p, a2p, v2p, prm, s0, ones_blk, *prev)


def _rwkv_step_kernel(n_tok, has_vres, fill,
                      r_ref, k_ref, v_ref, z_ref, wm_ref, am_ref, vm_ref, vf_ref,
                      w2_ref, a2_ref, v2_ref, prm_ref, s0_ref, ones_ref, *rest):
    g_ref, so_ref, tr_ref, y_ref = rest[-4:]
    hw = RWKV_HEAD
    n_seq = s0_ref.shape[-1]
    ones_blk = ones_ref[...]
    prm = prm_ref[...]
    w0, a0, v0 = prm[0:1], prm[1:2], prm[2:3]
    kk_w, ka_w, rk_w, ln_w, ln_b = prm[3:4], prm[4:5], prm[5:6], prm[6:7], prm[7:8]
    r = r_ref[...]
    k = k_ref[...]
    v = v_ref[...]

    wl = w0 + _mm(wm_ref[...].astype(BF16), w2_ref[...])
    decay = jnp.exp(_log_decay(wl))
    alpha = _sigmoid(a0 + _mm(am_ref[...].astype(BF16), a2_ref[...]))
    if has_vres:
        v = v + (vf_ref[...] - v) * _sigmoid(v0 + _mm(vm_ref[...].astype(BF16), v2_ref[...]))
    kk = k * kk_w
    k2 = k * (1.0 + (alpha - 1.0) * ka_w)
    kk_n2, bonus_s = _segsum_many([kk * kk, r * k2 * rk_w], ones_blk)
    kk = kk / jnp.maximum(jnp.sqrt(kk_n2), 1e-12)
    b = kk * alpha

    for idx, arr in enumerate((decay, k2, v, kk, b, r)):
        for t in range(n_tok):
            tr_ref[idx, t] = arr[t * n_seq:(t + 1) * n_seq, :].T

    def make_body(hh):
        lo = hh * hw

        def body(vi, carry):
            s = s0_ref[hh, vi]
            for t in range(n_tok):
                v_row = tr_ref[2, t, pl.ds(lo + vi, 1), :]
                sa = -jnp.sum(s * tr_ref[3, t, lo:lo + hw, :], axis=0, keepdims=True)
                s = (s * tr_ref[0, t, lo:lo + hw, :] + sa * tr_ref[4, t, lo:lo + hw, :]
                     + v_row * tr_ref[1, t, lo:lo + hw, :])
                y_ref[t, pl.ds(lo + vi, 1), :] = jnp.sum(
                    s * tr_ref[5, t, lo:lo + hw, :], axis=0, keepdims=True)
            _store_state(so_ref, fill, (hh, vi), s)
            return carry

        return body

    for hh in range(s0_ref.shape[0]):
        lax.fori_loop(0, hw, make_body(hh), 0, unroll=8)

    inv_n = 1.0 / hw
    for t in range(n_tok):
        rows = slice(t * n_seq, (t + 1) * n_seq)
        y = y_ref[t].T
        sum_y, sum_yy = _segsum_many([y, y * y], ones_blk, n_pieces=3)
        mean = sum_y * inv_n
        var = jnp.maximum(sum_yy * inv_n - mean * mean, 0.0)
        yn = (y - mean) * lax.rsqrt(var + RWKV_GN_EPS) * ln_w + ln_b
        z = z_ref[rows, :]
        g_ref[rows, :] = ((yn + bonus_s[rows] * v[rows]) * (z * _sigmoid(z))).astype(BF16)


def _rwkv_step(proj, vf_src, wm, am, vm, w2p, a2p, v2p, prm, s0, s_out_prev, *, layer_j, n_tok,
               has_vres):
    m = proj.shape[0]
    d = proj.shape[1] // 4
    pair = 2 * RWKV_HEAD
    n_pairs = d // pair
    n_seq = s0.shape[-1]

    def col(off):
        return pl.BlockSpec((m, pair), lambda hp: (0, off * n_pairs + hp))

    lora = pl.BlockSpec((m, LORA_PAD), lambda hp: (0, 0))
    up = pl.BlockSpec((LORA_PAD, pair), lambda hp: (0, hp))
    blk = (2, RWKV_HEAD, RWKV_HEAD, n_seq)
    state = pl.BlockSpec((None,) + blk, lambda hp: (layer_j, hp, 0, 0, 0))
    prev = [] if s_out_prev is None else [s_out_prev]
    fill = None if prev else (layer_j, s0.shape[0])
    state_out = state if prev else pl.BlockSpec((s0.shape[0],) + blk, lambda hp: (0, hp, 0, 0, 0))
    ones_blk = jnp.where(
        lax.shift_right_logical(_iota((pair, pair), 0), 6)
        == lax.shift_right_logical(_iota((pair, pair), 1), 6), 1.0, 0.0).astype(BF16)
    return pl.pallas_call(
        functools.partial(_rwkv_step_kernel, n_tok, has_vres, fill),
        grid=(n_pairs,),
        in_specs=[col(0), col(1), col(2), col(3), lora, lora, lora, col(2), up, up, up,
                  pl.BlockSpec((8, pair), lambda hp: (0, hp)), state,
                  pl.BlockSpec((pair, pair), lambda hp: (0, 0))]
        + [pl.BlockSpec(memory_space=pl.ANY)] * len(prev),
        out_specs=[pl.BlockSpec((m, pair), lambda hp: (0, hp)), state_out],
        out_shape=[jax.ShapeDtypeStruct((m, d), BF16), jax.ShapeDtypeStruct(s0.shape, F32)],
        scratch_shapes=[pltpu.VMEM((6, n_tok, pair, n_seq), F32),
                        pltpu.VMEM((n_tok, pair, n_seq), F32)],
        input_output_aliases={14: 1} if prev else {},
        compiler_params=pltpu.CompilerParams(
            dimension_semantics=("arbitrary",), vmem_limit_bytes=VMEM_LIMIT),
        name="rwkv_step",
    )(proj, proj, proj, proj, wm, am, vm, vf_src, w2p, a2p, v2p, prm, s0, ones_blk, *prev)


def _hgrn_head(chunk, tvalid, rows_sl, sl, head, refs, lb, nw, ones_blk):
    (q_ref, f_ref, i_ref, z_ref, st_ref) = refs
    sub = min(HGRN_SUB, chunk)
    shift = int(math.log2(sub))
    fl = f_ref[rows_sl, sl]
    qraw = q_ref[rows_sl, sl]
    val = i_ref[rows_sl, sl]
    sig = _sigmoid(fl)
    f = lb + (1.0 - lb) * sig
    logf = jnp.log(jnp.maximum(f, GATE_FLOOR)) * LOG2_E
    if tvalid < chunk:
        logf = jnp.where(_iota(logf.shape, 0) < tvalid, logf, 0.0)
    kg = (1.0 - lb) * (1.0 - sig)
    qs = qraw * _sigmoid(qraw) * (HGRN_DK ** -0.5)
    gcum = _cumsum_rows(logf)
    yield

    st = st_ref[head]
    o_inter = _mm_nt((qs * jnp.exp2(gcum)).astype(BF16), st.astype(BF16))
    g_last = gcum[chunk - 1:chunk, :]
    k_tail = kg * jnp.exp2(g_last - gcum)
    st_ref[head] = st * jnp.exp2(g_last) + _mm_tn(val.astype(BF16), k_tail.astype(BF16))

    def bcast_rows(x):
        return jnp.concatenate(
            [jnp.broadcast_to(x[t:t + 1, :], (sub, HGRN_DK)) for t in range(tvalid)], axis=0)

    def tile_blocks(x):
        return jnp.concatenate(
            [x[(t >> shift) * sub:((t >> shift) + 1) * sub] for t in range(tvalid)], axis=0)

    rows = tvalid * sub
    prod = bcast_rows(qs) * tile_blocks(kg) * jnp.exp2(
        jnp.minimum(bcast_rows(gcum) - tile_blocks(gcum), 0.0))
    att_diag = _mm(prod.astype(BF16), ones_blk)
    n_blk = -(-tvalid // sub)
    att_off = []
    for bi in range(1, n_blk):
        lo = bi * sub
        g_edge = gcum[lo - 1:lo]
        q_n = (qs[lo:lo + sub] * jnp.exp2(gcum[lo:lo + sub] - g_edge)).astype(BF16)
        k_n = (kg[0:lo] * jnp.exp2(g_edge - gcum[0:lo])).astype(BF16)
        att_off.append(_mm_nt(q_n, k_n))
    yield

    ridx = _iota((rows, HGRN_DK), 0)
    causal = (jnp.bitwise_and(ridx, sub - 1)
              <= jnp.bitwise_and(lax.shift_right_logical(ridx, shift), sub - 1))
    weighted = jnp.where(causal, att_diag * tile_blocks(val), 0.0)
    sel = jnp.where(lax.shift_right_logical(_iota((chunk, rows), 1), shift) == _iota((chunk, rows), 0),
                    1.0, 0.0).astype(BF16)
    o = o_inter + _mm(sel, weighted.astype(BF16))
    if n_blk > 1:
        parts = [jnp.zeros((sub, HGRN_DK), F32)]
        for bi in range(1, n_blk):
            parts.append(_mm(att_off[bi - 1].astype(BF16), val[0:bi * sub].astype(BF16)))
        if n_blk * sub < chunk:
            parts.append(jnp.zeros((chunk - n_blk * sub, HGRN_DK), F32))
        o = o + jnp.concatenate(parts, axis=0)
    yield

    o = o * lax.rsqrt(jnp.mean(o * o, axis=-1, keepdims=True) + NORM_EPS) * nw
    z = z_ref[rows_sl, sl]
    return o * (z * _sigmoid(z))


def _hgrn_chunk_kernel(chunk, tvalid, layer_j, seqs, heads, fill,
                       q_ref, f_ref, i_ref, z_ref, lbl_ref, nw_ref, s0_ref, *rest):
    g_ref, so_ref, st_ref = rest[-3:]
    c = pl.program_id(2)
    n_chunks = pl.num_programs(2)

    @pl.when(c == 0)
    def _init():
        for q in range(seqs):
            for hh in range(heads):
                st_ref[q * heads + hh] = s0_ref[q, hh].T

    logits = lbl_ref[...]
    ex = jnp.exp(logits - jnp.max(logits, axis=0, keepdims=True))
    soft = ex / jnp.sum(ex, axis=0, keepdims=True)
    lb_all = jnp.sum(soft[0:layer_j + 1], axis=0, keepdims=True) - soft[0:1]

    ones_blk = jnp.ones((HGRN_DK, HGRN_DK), BF16)
    nw = nw_ref[...]
    refs = (q_ref, f_ref, i_ref, z_ref, st_ref)
    outs = _round_robin([
        _hgrn_head(chunk, tvalid, slice(q * chunk, (q + 1) * chunk),
                   slice(hh * HGRN_DK, (hh + 1) * HGRN_DK), q * heads + hh, refs,
                   lb_all[:, hh * HGRN_DK:(hh + 1) * HGRN_DK], nw, ones_blk)
        for q in range(seqs) for hh in range(heads)])
    for hh in range(heads):
        col = [outs[q * heads + hh] for q in range(seqs)]
        g_ref[:, hh * HGRN_DK:(hh + 1) * HGRN_DK] = (
            col[0] if seqs == 1 else jnp.concatenate(col, axis=0)).astype(BF16)

    @pl.when(c == n_chunks - 1)
    def _fin():
        for q in range(seqs):
            for hh in range(heads):
                _store_state(so_ref, fill, (q, hh), st_ref[q * heads + hh].T)


def _hgrn_chunk(proj, lb_logits, norm_w, s0, s_out_prev, *, layer_j, n_seq, n_chunks, chunk, tvalid,
                seqs=1):
    assert seqs == 1 or n_chunks == 1
    m = n_seq * n_chunks * chunk
    d = proj.shape[1] // 4
    n_heads = d // HGRN_DK
    heads = HGRN_HEADS_PER_STEP
    n_hb = n_heads // heads
    width = heads * HGRN_DK

    def col(off):
        return pl.BlockSpec((seqs * chunk, width), lambda s, h, c: (s * n_chunks + c, off * n_hb + h))

    state = pl.BlockSpec((None, seqs, heads, HGRN_DK, HGRN_DK), lambda s, h, c: (layer_j, s, h, 0, 0))
    prev = [] if s_out_prev is None else [s_out_prev]
    fill = None if prev else (layer_j, s0.shape[0])
    state_out = state if prev else pl.BlockSpec(
        (s0.shape[0], seqs, heads, HGRN_DK, HGRN_DK), lambda s, h, c: (0, s, h, 0, 0))
    kernel = functools.partial(_hgrn_chunk_kernel, chunk, tvalid, layer_j, seqs, heads, fill)
    return pl.pallas_call(
        kernel,
        grid=(n_seq // seqs, n_hb, n_chunks),
        in_specs=[col(0), col(1), col(2), col(3),
                  pl.BlockSpec((lb_logits.shape[0], width), lambda s, h, c: (0, h)),
                  pl.BlockSpec((1, HGRN_DK), lambda s, h, c: (0, 0)), state]
        + [pl.BlockSpec(memory_space=pl.ANY)] * len(prev),
        out_specs=[pl.BlockSpec((seqs * chunk, width), lambda s, h, c: (s * n_chunks + c, h)),
                   state_out],
        out_shape=[jax.ShapeDtypeStruct((m, d), BF16),
                   jax.ShapeDtypeStruct(s0.shape, F32)],
        scratch_shapes=[pltpu.VMEM((seqs * heads, HGRN_DK, HGRN_DK), F32)],
        input_output_aliases={7: 1} if prev else {},
        compiler_params=pltpu.CompilerParams(
            dimension_semantics=("arbitrary", "arbitrary", "arbitrary"),
            vmem_limit_bytes=VMEM_LIMIT),
        name="hgrn_chunk",
    )(proj, proj, proj, proj, lb_logits, norm_w.reshape(1, HGRN_DK), s0, *prev)


def _pad_lanes(w, axis):
    pad = [(0, 0)] * w.ndim
    pad[axis] = (0, LORA_PAD - w.shape[axis])
    return jnp.pad(w, pad)


def _rwkv_layer_inputs(h, hprev, p, j, tiles_per_seq):
    has_vres = j > 0
    proj = _proj_rwkv(h, hprev, p["mu4"][j], p["w_in"], j, tiles_per_seq)
    lora = _lora(h, hprev, p["mu6"][j], p["w1"][j], p["a1"][j], p["v1"][max(j - 1, 0)],
                 has_vres, tiles_per_seq)
    return (proj,) + tuple(lora) + (has_vres,)


def _trunk_prompt(x, shift0, wkv0, hgrn0, p, *, n_seq, n_tok):
    d = x.shape[1]
    n_chunks = n_tok // PROMPT_CHUNK
    shifts = []
    wkv_out, hgrn_out, vf_src = None, None, None
    h = _rmsnorm(x, p["norm_pre"][0])
    for layer in range(DEPTH):
        j = layer // 2
        npre_next = p["norm_pre"][layer + 1] if layer + 1 < DEPTH else None
        if layer % 2 == 0:
            proj, wm, am, vm, tail, has_vres = _rwkv_layer_inputs(
                h, shift0[j][:, None, :], p, j, n_tok // ROW_TILE)
            shifts.append(tail[:, 7])
            vf_src = proj if vf_src is None else vf_src
            g, wkv_out = _rwkv_chunk(
                proj, vf_src, wm, am, vm, p["w2"][j], p["a2"][j], p["v2"][max(j - 1, 0)],
                p["rwkv_prm"][j], wkv0, wkv_out, layer_j=j, n_seq=n_seq, n_chunks=n_chunks,
                chunk=PROMPT_CHUNK, tvalid=PROMPT_CHUNK, has_vres=has_vres)
            w_o = p["rwkv_w_o"]
        else:
            proj = _proj_hgrn(h, p["hgrn_w_in"], j)
            g, hgrn_out = _hgrn_chunk(
                proj, p["hgrn_lb_logits"], p["hgrn_norm_w"][j], hgrn0, hgrn_out, layer_j=j,
                n_seq=n_seq, n_chunks=n_chunks, chunk=PROMPT_CHUNK, tvalid=PROMPT_CHUNK)
            w_o = p["hgrn_w_o"]
        x, h = _outproj(g, w_o, j, x, p["norm_post"][layer], npre_next,
                        BF16 if layer % 2 == 0 else F32)
    return x, jnp.stack(shifts), wkv_out, hgrn_out


def _trunk_sample(x, shift0, wkv0, hgrn0, p, *, n_seq, n_tok):
    d = x.shape[1]
    tpad = SAMPLE_TPAD

    def to_padded(a):
        a3 = a.reshape(n_tok, n_seq, a.shape[1]).transpose(1, 0, 2)
        return jnp.pad(a3, ((0, 0), (0, tpad - n_tok), (0, 0))).reshape(n_seq * tpad, a.shape[1])

    def from_padded(a):
        a3 = a.reshape(n_seq, tpad, a.shape[1])[:, :n_tok]
        return a3.transpose(1, 0, 2).reshape(n_tok * n_seq, a.shape[1])

    shifts = []
    wkv_out, hgrn_out, vf_src = None, None, None
    h = _rmsnorm(x, p["norm_pre"][0])
    for layer in range(DEPTH):
        j = layer // 2
        npre_next = p["norm_pre"][layer + 1] if layer + 1 < DEPTH else None
        if layer % 2 == 0:
            hprev = jnp.concatenate([shift0[j], h[:(n_tok - 1) * n_seq]], axis=0)
            shifts.append(h[(n_tok - 1) * n_seq:])
            proj, wm, am, vm, has_vres = _rwkv_layer_inputs(h, hprev, p, j, 0)
            vf_src = proj if vf_src is None else vf_src
            g, wkv_out = _rwkv_step(
                proj, vf_src, wm, am, vm, p["w2"][j], p["a2"][j], p["v2"][max(j - 1, 0)],
                p["rwkv_prm"][j], wkv0, wkv_out, layer_j=j, n_tok=n_tok, has_vres=has_vres)
            w_o = p["rwkv_w_o"]
        else:
            proj = to_padded(_proj_hgrn(h, p["hgrn_w_in"], j))
            g, hgrn_out = _hgrn_chunk(
                proj, p["hgrn_lb_logits"], p["hgrn_norm_w"][j], hgrn0, hgrn_out, layer_j=j,
                n_seq=n_seq, n_chunks=1, chunk=tpad, tvalid=n_tok, seqs=HGRN_SAMPLE_SEQS)
            g = from_padded(g)
            w_o = p["hgrn_w_o"]
        x, h = _outproj(g, w_o, j, x, p["norm_post"][layer], npre_next,
                        BF16 if layer % 2 == 0 else F32)
    return x, jnp.stack(shifts), wkv_out, hgrn_out


def kernel(x_prompt, x_sample, state_rwkv_shift, state_rwkv_wkv, state_hgrn, norm_pre, norm_post,
           rwkv_mu, rwkv_w_in, rwkv_w0, rwkv_w1, rwkv_w2, rwkv_a0, rwkv_a1, rwkv_a2, rwkv_v0,
           rwkv_v1, rwkv_v2, rwkv_k_k, rwkv_k_a, rwkv_r_k, rwkv_ln_w, rwkv_ln_b, rwkv_w_o,
           hgrn_w_in, hgrn_lb_logits, hgrn_norm_w, hgrn_w_o):
    n_rwkv = rwkv_mu.shape[0]
    d = x_prompt.shape[-1]
    v0_full = jnp.concatenate([jnp.zeros((1, d), F32), rwkv_v0], axis=0)
    rwkv_prm = jnp.stack([rwkv_w0, rwkv_a0, v0_full, rwkv_k_k, rwkv_k_a,
                          rwkv_r_k.reshape(n_rwkv, d), rwkv_ln_w, rwkv_ln_b], axis=1)
    p = {
        "norm_pre": norm_pre, "norm_post": norm_post,
        "mu4": rwkv_mu[:, :4, None, :], "mu6": rwkv_mu,
        "w_in": rwkv_w_in.astype(BF16),
        "w1": _pad_lanes(rwkv_w1, 2).astype(BF16), "a1": _pad_lanes(rwkv_a1, 2).astype(BF16),
        "v1": _pad_lanes(rwkv_v1, 2).astype(BF16),
        "w2": _pad_lanes(rwkv_w2, 1).astype(BF16), "a2": _pad_lanes(rwkv_a2, 1).astype(BF16),
        "v2": _pad_lanes(rwkv_v2, 1).astype(BF16),
        "rwkv_prm": rwkv_prm, "rwkv_w_o": rwkv_w_o.astype(BF16),
        "hgrn_w_in": hgrn_w_in.astype(BF16), "hgrn_lb_logits": hgrn_lb_logits,
        "hgrn_norm_w": hgrn_norm_w, "hgrn_w_o": hgrn_w_o.astype(BF16),
    }

    bp, tp, _ = x_prompt.shape
    zero_shift = jnp.zeros((n_rwkv, bp, d), F32)
    zero_wkv = jnp.zeros((n_rwkv, bp) + state_rwkv_wkv.shape[2:], F32)
    zero_hgrn = jnp.zeros((state_hgrn.shape[0], bp) + state_hgrn.shape[2:], F32)
    y_p, p_shift, p_wkv, p_hgrn = _trunk_prompt(
        x_prompt.reshape(bp * tp, d), zero_shift, zero_wkv, zero_hgrn, p, n_seq=bp, n_tok=tp)

    bs, ts, _ = x_sample.shape
    y_s, s_shift, s_wkv, s_hgrn = _trunk_sample(
        x_sample.transpose(1, 0, 2).reshape(ts * bs, d), state_rwkv_shift,
        state_rwkv_wkv.transpose(0, 2, 3, 4, 1), state_hgrn, p, n_seq=bs, n_tok=ts)

    return (y_p.reshape(bp, tp, d), y_s.reshape(ts, bs, d).transpose(1, 0, 2),
            p_shift, p_wkv, p_hgrn, s_shift, s_wkv.transpose(0, 4, 1, 2, 3), s_hgrn)
```

```python
import functools
import math

import jax
import jax.numpy as jnp
from jax import lax
from jax.experimental import pallas as pl
from jax.experimental.pallas import tpu as pltpu

F32 = jnp.float32
BF16 = jnp.bfloat16

D_MODEL = 2048
DEPTH = 4
RWKV_HEAD = 64
HGRN_DK = 128
RWKV_GN_EPS = 1e-5 * RWKV_HEAD
NORM_EPS = 1e-6
GATE_FLOOR = 1e-30
LOG2_E = 1.0 / math.log(2.0)

LANES = 128
GROUP = 256
HEADS_PER_GROUP = GROUP // RWKV_HEAD
LORA_PAD = 128
PROMPT_CHUNK = 64
SAMPLE_TPAD = 8
HGRN_SUB = 8
HGRN_HEADS_PER_STEP = 16
HGRN_SAMPLE_SEQS = 4
ROW_TILE = 512
PROJ_TILE = 1024
VMEM_LIMIT = 48 * 1024 * 1024


def _mm(a, b):
    return jnp.dot(a, b, preferred_element_type=F32)


def _mm_nt(a, b):
    return lax.dot_general(a, b, (((1,), (1,)), ((), ())), preferred_element_type=F32)


def _mm_tn(a, b):
    return lax.dot_general(a, b, (((0,), (0,)), ((), ())), preferred_element_type=F32)


def _iota(shape, dim):
    return lax.broadcasted_iota(jnp.int32, shape, dim)


def _split2(x):
    hi = x.astype(BF16)
    lo = (x - hi.astype(F32)).astype(BF16)
    return hi, lo


def _segsum(x, ones_blk):
    hi, lo = _split2(x)
    return _mm(hi, ones_blk) + _mm(lo, ones_blk)


def _split3(x):
    x1 = x.astype(BF16)
    r1 = x - x1.astype(F32)
    x2 = r1.astype(BF16)
    return x1, x2, (r1 - x2.astype(F32)).astype(BF16)


def _segsum_many(xs, ones_blk, n_pieces=2):
    split = _split2 if n_pieces == 2 else _split3
    pieces = []
    for x in xs:
        pieces.extend(split(x))
    out = _mm(jnp.concatenate(pieces, axis=0), ones_blk)
    n = xs[0].shape[0]
    sums = []
    for i in range(len(xs)):
        parts = [out[(n_pieces * i + j) * n:(n_pieces * i + j + 1) * n] for j in range(n_pieces)]
        sums.append(functools.reduce(lambda a, b: a + b, parts))
    return sums


def _cumsum_rows(x):
    n = x.shape[0]
    tri = jnp.where(_iota((n, n), 0) >= _iota((n, n), 1), 1.0, 0.0).astype(BF16)
    x1, x2, x3 = _split3(x)
    return _mm(tri, x1) + _mm(tri, x2) + _mm(tri, x3)


def _block_diag(x, head_width, n_heads):
    shift = int(math.log2(head_width))
    lane_head = lax.shift_right_logical(_iota(x.shape, 1), shift)
    parts = [jnp.where(lane_head == h, x, 0.0).astype(BF16) for h in range(n_heads)]
    return jnp.concatenate(parts, axis=0)


def _sigmoid(x):
    return jax.nn.sigmoid(x)


def _log_decay(wl):
    return _sigmoid(wl) * (-math.exp(-0.5))


def _store_state(so_ref, fill, idx, value):
    if fill is None:
        so_ref[idx] = value
        return
    layer, n_layers = fill
    for other in range(n_layers):
        so_ref[(other,) + idx] = value if other == layer else jnp.zeros_like(value)


def _round_robin(gens):
    results = [None] * len(gens)
    active = list(range(len(gens)))
    while active:
        for idx in list(active):
            try:
                next(gens[idx])
            except StopIteration as stop:
                results[idx] = stop.value
                active.remove(idx)
    return results


def _rmsnorm_kernel(x_ref, w_ref, o_ref):
    x = x_ref[...]
    ms = jnp.mean(x * x, axis=-1, keepdims=True)
    o_ref[...] = x * lax.rsqrt(ms + NORM_EPS) * w_ref[...]


def _rmsnorm(x, w):
    m, d = x.shape
    return pl.pallas_call(
        _rmsnorm_kernel,
        grid=(m // ROW_TILE,),
        in_specs=[pl.BlockSpec((ROW_TILE, d), lambda i: (i, 0)),
                  pl.BlockSpec((1, d), lambda i: (0, 0))],
        out_specs=pl.BlockSpec((ROW_TILE, d), lambda i: (i, 0)),
        out_shape=jax.ShapeDtypeStruct((m, d), F32),
        name="rmsnorm",
    )(x, w.reshape(1, d))


def _prev_rows(h, first_ref, carry_ref, tile, tiles_per_seq):
    rolled = pltpu.roll(h, 1, axis=0)
    seq_start = lax.rem(tile, tiles_per_seq) == 0
    edge = jnp.where(seq_start, first_ref[...], carry_ref[7:8, :])
    carry_ref[...] = h[h.shape[0] - 8:, :]
    return jnp.where(_iota(h.shape, 0) == 0, edge, rolled)


def _proj_mix_kernel(tiles_per_seq, h_ref, hp_ref, mu_ref, w_ref, o_ref, *carry):
    h = h_ref[...]
    if tiles_per_seq:
        hprev = _prev_rows(h, hp_ref, carry[0], pl.program_id(1), tiles_per_seq)
    else:
        hprev = hp_ref[...]
    xs = h + (hprev - h) * mu_ref[...]
    o_ref[...] = _mm(xs.astype(BF16), w_ref[...])


def _proj_plain_kernel(h_ref, w_ref, o_ref):
    o_ref[...] = _mm(h_ref[...].astype(BF16), w_ref[...])


def _prev_spec(hprev, tiles_per_seq, d, grid_rank, tile=ROW_TILE):
    if tiles_per_seq:
        if grid_rank == 2:
            return pl.BlockSpec((None, 1, d), lambda n, i: (i // tiles_per_seq, 0, 0))
        return pl.BlockSpec((None, 1, d), lambda i: (i // tiles_per_seq, 0, 0))
    if grid_rank == 2:
        return pl.BlockSpec((tile, d), lambda n, i: (i, 0))
    return pl.BlockSpec((tile, d), lambda i: (i, 0))


def _proj_rwkv(h, hprev, mu4, w4, layer_j, tiles_per_seq):
    m, d = h.shape
    n_proj = w4.shape[1]
    tile = min(PROJ_TILE, m)
    return pl.pallas_call(
        functools.partial(_proj_mix_kernel, tiles_per_seq),
        grid=(n_proj, m // tile),
        in_specs=[pl.BlockSpec((tile, d), lambda n, i: (i, 0)),
                  _prev_spec(hprev, tiles_per_seq, d, 2, tile),
                  pl.BlockSpec((None, 1, d), lambda n, i: (n, 0, 0)),
                  pl.BlockSpec((None, None, d, d), lambda n, i: (layer_j, n, 0, 0),
                               pipeline_mode=pl.Buffered(1))],
        out_specs=pl.BlockSpec((tile, d), lambda n, i: (i, n)),
        out_shape=jax.ShapeDtypeStruct((m, n_proj * d), F32),
        scratch_shapes=[pltpu.VMEM((8, d), F32)] if tiles_per_seq else [],
        compiler_params=pltpu.CompilerParams(
            dimension_semantics=("arbitrary", "arbitrary"), vmem_limit_bytes=VMEM_LIMIT),
        name="proj_rwkv",
    )(h, hprev, mu4, w4)


def _proj_hgrn(h, w, layer_j):
    m, d = h.shape
    n_proj = w.shape[2] // d
    tile = min(PROJ_TILE, m)
    return pl.pallas_call(
        _proj_plain_kernel,
        grid=(n_proj, m // tile),
        in_specs=[pl.BlockSpec((tile, d), lambda n, i: (i, 0)),
                  pl.BlockSpec((None, d, d), lambda n, i: (layer_j, 0, n),
                               pipeline_mode=pl.Buffered(1))],
        out_specs=pl.BlockSpec((tile, d), lambda n, i: (i, n)),
        out_shape=jax.ShapeDtypeStruct((m, n_proj * d), F32),
        compiler_params=pltpu.CompilerParams(
            dimension_semantics=("arbitrary", "arbitrary"), vmem_limit_bytes=VMEM_LIMIT),
        name="proj_hgrn",
    )(h, w)


def _lora_kernel(has_v, tiles_per_seq, h_ref, hp_ref, mu_ref, w1_ref, a1_ref, v1_ref,
                 wm_ref, am_ref, vm_ref, *rest):
    h = h_ref[...]
    if tiles_per_seq:
        tail_ref, carry_ref = rest
        hprev = _prev_rows(h, hp_ref, carry_ref, pl.program_id(0), tiles_per_seq)
        tail_ref[...] = h[h.shape[0] - 8:, :]
    else:
        hprev = hp_ref[...]
    delta = hprev - h
    xw = h + delta * mu_ref[4:5, :]
    xa = h + delta * mu_ref[5:6, :]
    wm_ref[...] = jnp.tanh(_mm(xw.astype(BF16), w1_ref[...]))
    am_ref[...] = _mm(xa.astype(BF16), a1_ref[...])
    if has_v:
        xv = h + delta * mu_ref[2:3, :]
        vm_ref[...] = _mm(xv.astype(BF16), v1_ref[...])
    else:
        vm_ref[...] = jnp.zeros(vm_ref.shape, F32)


def _lora(h, hprev, mu6, w1p, a1p, v1p, has_v, tiles_per_seq):
    m, d = h.shape
    row = pl.BlockSpec((ROW_TILE, d), lambda i: (i, 0))
    wspec = pl.BlockSpec((d, LORA_PAD), lambda i: (0, 0))
    ospec = pl.BlockSpec((ROW_TILE, LORA_PAD), lambda i: (i, 0))
    oshape = jax.ShapeDtypeStruct((m, LORA_PAD), F32)
    return pl.pallas_call(
        functools.partial(_lora_kernel, has_v, tiles_per_seq),
        grid=(m // ROW_TILE,),
        in_specs=[row, _prev_spec(hprev, tiles_per_seq, d, 1),
                  pl.BlockSpec(mu6.shape, lambda i: (0, 0)), wspec, wspec, wspec],
        out_specs=[ospec, ospec, ospec] + (
            [pl.BlockSpec((None, 8, d), lambda i: (i // tiles_per_seq, 0, 0))] if tiles_per_seq else []),
        out_shape=[oshape, oshape, oshape] + (
            [jax.ShapeDtypeStruct((m // (tiles_per_seq * ROW_TILE), 8, d), F32)] if tiles_per_seq else []),
        scratch_shapes=[pltpu.VMEM((8, d), F32)] if tiles_per_seq else [],
        compiler_params=pltpu.CompilerParams(dimension_semantics=("arbitrary",)),
        name="lora",
    )(h, hprev, mu6, w1p, a1p, v1p)


def _outproj_kernel(g_ref, w_ref, x_ref, npost_ref, npre_ref, xo_ref, *ho_ref):
    part = g_ref.shape[0] // 4
    w = w_ref[...]
    parts = [slice(i * part, (i + 1) * part) for i in range(4)]
    outs = [_mm(g_ref[rows, :], w) for rows in parts]
    for rows, out in zip(parts, outs):
        ms = jnp.mean(out * out, axis=-1, keepdims=True)
        xn = x_ref[rows, :] + out * lax.rsqrt(ms + NORM_EPS) * npost_ref[...]
        xo_ref[rows, :] = xn
        if ho_ref:
            ms2 = jnp.mean(xn * xn, axis=-1, keepdims=True)
            ho_ref[0][rows, :] = (xn * lax.rsqrt(ms2 + NORM_EPS) * npre_ref[...]).astype(ho_ref[0].dtype)


def _outproj(g, w, layer_j, x, npost, npre_next, h_dtype):
    m, d = x.shape
    n_out = 1 if npre_next is None else 2
    if npre_next is None:
        npre_next = npost
    row = pl.BlockSpec((ROW_TILE, d), lambda i: (i, 0))
    vec = pl.BlockSpec((1, d), lambda i: (0, 0))
    out = pl.pallas_call(
        _outproj_kernel,
        grid=(m // ROW_TILE,),
        in_specs=[row, pl.BlockSpec((None, d, d), lambda i: (layer_j, 0, 0)), row, vec, vec],
        out_specs=[row, row][:n_out],
        out_shape=[jax.ShapeDtypeStruct((m, d), F32), jax.ShapeDtypeStruct((m, d), h_dtype)][:n_out],
        compiler_params=pltpu.CompilerParams(
            dimension_semantics=("arbitrary",), vmem_limit_bytes=VMEM_LIMIT),
        name="outproj",
    )(g, w, x, npost.reshape(1, d), npre_next.reshape(1, d))
    return (out[0], None) if n_out == 1 else out


def _rwkv_group(chunk, tvalid, has_vres, n_double, sl, grp, refs, wm, am, vm, ones_blk, diag):
    (r_ref, k_ref, v_ref, z_ref, vf_ref, w2_ref, a2_ref, v2_ref, prm_ref, g_ref, sbd_ref) = refs
    hw = RWKV_HEAD
    nh = HEADS_PER_GROUP
    prm = prm_ref[:, sl]
    w0, a0, v0 = prm[0:1], prm[1:2], prm[2:3]
    kk_w, ka_w, rk_w, ln_w, ln_b = prm[3:4], prm[4:5], prm[5:6], prm[6:7], prm[7:8]
    r = r_ref[:, sl]
    k = k_ref[:, sl]
    v = v_ref[:, sl]

    wl = w0 + _mm(wm, w2_ref[:, sl])
    al = a0 + _mm(am, a2_ref[:, sl])
    if has_vres:
        vl = v0 + _mm(vm, v2_ref[:, sl])
    yield
    logw = _log_decay(wl)
    if tvalid < chunk:
        logw = jnp.where(_iota(logw.shape, 0) < tvalid, logw, 0.0)
    alpha = _sigmoid(al)
    if has_vres:
        v = v + (vf_ref[:, sl] - v) * _sigmoid(vl)
    kk = k * kk_w
    k2 = k * (1.0 + (alpha - 1.0) * ka_w)
    kk_n2, bonus_s = _segsum_many([kk * kk, r * k2 * rk_w], ones_blk)
    cum = _cumsum_rows(logw)
    yield
    kk = kk / jnp.maximum(jnp.sqrt(kk_n2), 1e-12)
    b = kk * alpha
    a = -kk
    e_neg = jnp.exp(-cum)
    a_hat = a * jnp.exp(cum - logw)
    b_hat = b * e_neg
    k_hat = k2 * e_neg
    r_hat = r * jnp.exp(cum)
    cum_last = cum[chunk - 1:chunk, :]
    e_tail = jnp.exp(cum_last - cum)
    b_tail = b * e_tail
    k_tail = k2 * e_tail

    lhs = jnp.concatenate([a_hat, r_hat], axis=0).astype(BF16)
    gram_b = _mm_nt(lhs, _block_diag(b_hat, hw, nh))
    gram_k = _mm_nt(lhs, _block_diag(k_hat, hw, nh))
    s_bd = sbd_ref[grp]
    xy_state = _mm_nt(lhs, s_bd.astype(BF16))
    yield
    shape_cc = (chunk, nh * chunk)
    t_idx = _iota(shape_cc, 0)
    i_idx = jnp.bitwise_and(_iota(shape_cc, 1), chunk - 1)
    strict = i_idx < t_idx
    incl = i_idx <= t_idx
    n_ab = jnp.where(strict, gram_b[:chunk], 0.0)
    a_ak = jnp.where(strict, gram_k[:chunk], 0.0)
    a_rb = jnp.where(incl, gram_b[chunk:], 0.0)
    a_rk = jnp.where(incl, gram_k[chunk:], 0.0)
    v_bd = _block_diag(v, hw, nh)

    xy = xy_state + _mm(jnp.concatenate([a_ak, a_rk], axis=0).astype(BF16), v_bd)
    x = xy[:chunk]
    y_v = xy[chunk:]
    p = n_ab
    if n_double > 1:
        p_next = _mm(p.astype(BF16), _block_diag(p, chunk, nh))
    yield
    for j in range(n_double):
        x = x + _mm(p.astype(BF16), _block_diag(x, hw, nh))
        if j + 1 < n_double:
            p = p_next
            if j + 2 < n_double:
                p_next = _mm(p.astype(BF16), _block_diag(p, chunk, nh))
        yield
    u = x

    y = y_v + _mm(a_rb.astype(BF16), _block_diag(u, hw, nh))
    uv = jnp.concatenate([u, v], axis=0).astype(BF16)
    bk = jnp.concatenate([b_tail, k_tail], axis=0).astype(BF16)
    sbd_ref[grp] = s_bd * jnp.exp(cum_last) + jnp.where(diag, _mm_tn(uv, bk), 0.0)
    yield

    inv_n = 1.0 / RWKV_HEAD
    sum_y, sum_yy = _segsum_many([y, y * y], ones_blk, n_pieces=3)
    yield
    mean = sum_y * inv_n
    var = jnp.maximum(sum_yy * inv_n - mean * mean, 0.0)
    yn = (y - mean) * lax.rsqrt(var + RWKV_GN_EPS) * ln_w + ln_b
    z = z_ref[:, sl]
    g_ref[:, sl] = ((yn + bonus_s * v) * (z * _sigmoid(z))).astype(BF16)


def _rwkv_chunk_kernel(chunk, tvalid, has_vres, n_double, n_groups, fill,
                       r_ref, k_ref, v_ref, z_ref, wm_ref, am_ref, vm_ref, vf_ref,
                       w2_ref, a2_ref, v2_ref, prm_ref, s0_ref, ones_ref, *rest):
    g_ref, so_ref, sbd_ref = rest[-3:]
    c = pl.program_id(1)
    n_chunks = pl.num_programs(1)
    hw = RWKV_HEAD
    diag = (lax.shift_right_logical(_iota((GROUP, GROUP), 0), 6)
            == lax.shift_right_logical(_iota((GROUP, GROUP), 1), 6))

    @pl.when(c == 0)
    def _init():
        for g in range(n_groups):
            s4 = s0_ref[g * HEADS_PER_GROUP:(g + 1) * HEADS_PER_GROUP].reshape(GROUP, hw)
            tiled = jnp.concatenate([s4] * HEADS_PER_GROUP, axis=1)
            sbd_ref[g] = jnp.where(diag, tiled, 0.0)

    ones_blk = ones_ref[...]
    wm = wm_ref[...].astype(BF16)
    am = am_ref[...].astype(BF16)
    vm = vm_ref[...].astype(BF16)
    refs = (r_ref, k_ref, v_ref, z_ref, vf_ref, w2_ref, a2_ref, v2_ref, prm_ref, g_ref, sbd_ref)
    _round_robin([
        _rwkv_group(chunk, tvalid, has_vres, n_double, slice(g * GROUP, (g + 1) * GROUP), g, refs,
                    wm, am, vm, ones_blk, diag)
        for g in range(n_groups)])

    @pl.when(c == n_chunks - 1)
    def _fin():
        for g in range(n_groups):
            s_new = sbd_ref[g]
            dense = (s_new[:, 0:hw] + s_new[:, hw:2 * hw]
                     + s_new[:, 2 * hw:3 * hw] + s_new[:, 3 * hw:4 * hw])
            _store_state(so_ref, fill, (slice(g * HEADS_PER_GROUP, (g + 1) * HEADS_PER_GROUP),),
                         dense.reshape(HEADS_PER_GROUP, hw, hw))


def _rwkv_chunk(proj, vf_src, wm, am, vm, w2p, a2p, v2p, prm, s0, s_out_prev, *, layer_j, n_seq,
                n_chunks, chunk, tvalid, has_vres):
    m = n_seq * n_chunks * chunk
    d = proj.shape[1] // 4
    n_groups = d // GROUP
    n_double = max(1, math.ceil(math.log2(tvalid)))

    def col(off):
        return pl.BlockSpec((chunk, d), lambda s, c: (s * n_chunks + c, off))

    lora = pl.BlockSpec((chunk, LORA_PAD), lambda s, c: (s * n_chunks + c, 0))
    up = pl.BlockSpec((LORA_PAD, d), lambda s, c: (0, 0))
    state = pl.BlockSpec((None, None) + s0.shape[2:], lambda s, c: (layer_j, s, 0, 0, 0))
    prev = [] if s_out_prev is None else [s_out_prev]
    fill = None if prev else (layer_j, s0.shape[0])
    state_out = state if prev else pl.BlockSpec(
        (s0.shape[0], None) + s0.shape[2:], lambda s, c: (0, s, 0, 0, 0))
    ones_blk = jnp.where(
        lax.shift_right_logical(_iota((GROUP, GROUP), 0), 6)
        == lax.shift_right_logical(_iota((GROUP, GROUP), 1), 6), 1.0, 0.0).astype(BF16)
    kernel = functools.partial(_rwkv_chunk_kernel, chunk, tvalid, has_vres, n_double, n_groups, fill)
    return pl.pallas_call(
        kernel,
        grid=(n_seq, n_chunks),
        in_specs=[col(0), col(1), col(2), col(3), lora, lora, lora, col(2), up, up, up,
                  pl.BlockSpec((8, d), lambda s, c: (0, 0)), state,
                  pl.BlockSpec((GROUP, GROUP), lambda s, c: (0, 0))]
        + [pl.BlockSpec(memory_space=pl.ANY)] * len(prev),
        out_specs=[pl.BlockSpec((chunk, d), lambda s, c: (s * n_chunks + c, 0)), state_out],
        out_shape=[jax.ShapeDtypeStruct((m, d), BF16),
                   jax.ShapeDtypeStruct(s0.shape, F32)],
        scratch_shapes=[pltpu.VMEM((n_groups, GROUP, GROUP), F32)],
        input_output_aliases={14: 1} if prev else {},
        compiler_params=pltpu.CompilerParams(
            dimension_semantics=("arbitrary", "arbitrary"), vmem_limit_bytes=VMEM_LIMIT),
        name="rwkv_chunk",
    )(proj, proj, proj, proj, wm, am, vm, vf_src, w2p, a2p, v2p, prm, s0, ones_blk, *prev)


def _rwkv_step_kernel(n_tok, has_vres, fill,
                      r_ref, k_ref, v_ref, z_ref, wm_ref, am_ref, vm_ref, vf_ref,
                      w2_ref, a2_ref, v2_ref, prm_ref, s0_ref, ones_ref, *rest):
    g_ref, so_ref, tr_ref, y_ref = rest[-4:]
    hw = RWKV_HEAD
    n_seq = s0_ref.shape[-1]
    ones_blk = ones_ref[...]
    prm = prm_ref[...]
    w0, a0, v0 = prm[0:1], prm[1:2], prm[2:3]
    kk_w, ka_w, rk_w, ln_w, ln_b = prm[3:4], prm[4:5], prm[5:6], prm[6:7], prm[7:8]
    r = r_ref[...]
    k = k_ref[...]
    v = v_ref[...]

    wl = w0 + _mm(wm_ref[...].astype(BF16), w2_ref[...])
    decay = jnp.exp(_log_decay(wl))
    alpha = _sigmoid(a0 + _mm(am_ref[...].astype(BF16), a2_ref[...]))
    if has_vres:
        v = v + (vf_ref[...] - v) * _sigmoid(v0 + _mm(vm_ref[...].astype(BF16), v2_ref[...]))
    kk = k * kk_w
    k2 = k * (1.0 + (alpha - 1.0) * ka_w)
    kk_n2, bonus_s = _segsum_many([kk * kk, r * k2 * rk_w], ones_blk)
    kk = kk / jnp.maximum(jnp.sqrt(kk_n2), 1e-12)
    b = kk * alpha

    for idx, arr in enumerate((decay, k2, v, kk, b, r)):
        for t in range(n_tok):
            tr_ref[idx, t] = arr[t * n_seq:(t + 1) * n_seq, :].T

    def make_body(hh):
        lo = hh * hw

        def body(vi, carry):
            s = s0_ref[hh, vi]
            for t in range(n_tok):
                v_row = tr_ref[2, t, pl.ds(lo + vi, 1), :]
                sa = -jnp.sum(s * tr_ref[3, t, lo:lo + hw, :], axis=0, keepdims=True)
                s = (s * tr_ref[0, t, lo:lo + hw, :] + sa * tr_ref[4, t, lo:lo + hw, :]
                     + v_row * tr_ref[1, t, lo:lo + hw, :])
                y_ref[t, pl.ds(lo + vi, 1), :] = jnp.sum(
                    s * tr_ref[5, t, lo:lo + hw, :], axis=0, keepdims=True)
            _store_state(so_ref, fill, (hh, vi), s)
            return carry

        return body

    for hh in range(s0_ref.shape[0]):
        lax.fori_loop(0, hw, make_body(hh), 0, unroll=8)

    inv_n = 1.0 / hw
    for t in range(n_tok):
        rows = slice(t * n_seq, (t + 1) * n_seq)
        y = y_ref[t].T
        sum_y, sum_yy = _segsum_many([y, y * y], ones_blk, n_pieces=3)
        mean = sum_y * inv_n
        var = jnp.maximum(sum_yy * inv_n - mean * mean, 0.0)
        yn = (y - mean) * lax.rsqrt(var + RWKV_GN_EPS) * ln_w + ln_b
        z = z_ref[rows, :]
        g_ref[rows, :] = ((yn + bonus_s[rows] * v[rows]) * (z * _sigmoid(z))).astype(BF16)


def _rwkv_step(proj, vf_src, wm, am, vm, w2p, a2p, v2p, prm, s0, s_out_prev, *, layer_j, n_tok,
               has_vres):
    m = proj.shape[0]
    d = proj.shape[1] // 4
    pair = 2 * RWKV_HEAD
    n_pairs = d // pair
    n_seq = s0.shape[-1]

    def col(off):
        return pl.BlockSpec((m, pair), lambda hp: (0, off * n_pairs + hp))

    lora = pl.BlockSpec((m, LORA_PAD), lambda hp: (0, 0))
    up = pl.BlockSpec((LORA_PAD, pair), lambda hp: (0, hp))
    blk = (2, RWKV_HEAD, RWKV_HEAD, n_seq)
    state = pl.BlockSpec((None,) + blk, lambda hp: (layer_j, hp, 0, 0, 0))
    prev = [] if s_out_prev is None else [s_out_prev]
    fill = None if prev else (layer_j, s0.shape[0])
    state_out = state if prev else pl.BlockSpec((s0.shape[0],) + blk, lambda hp: (0, hp, 0, 0, 0))
    ones_blk = jnp.where(
        lax.shift_right_logical(_iota((pair, pair), 0), 6)
        == lax.shift_right_logical(_iota((pair, pair), 1), 6), 1.0, 0.0).astype(BF16)
    return pl.pallas_call(
        functools.partial(_rwkv_step_kernel, n_tok, has_vres, fill),
        grid=(n_pairs,),
        in_specs=[col(0), col(1), col(2), col(3), lora, lora, lora, col(2), up, up, up,
                  pl.BlockSpec((8, pair), lambda hp: (0, hp)), state,
                  pl.BlockSpec((pair, pair), lambda hp: (0, 0))]
        + [pl.BlockSpec(memory_space=pl.ANY)] * len(prev),
        out_specs=[pl.BlockSpec((m, pair), lambda hp: (0, hp)), state_out],
        out_shape=[jax.ShapeDtypeStruct((m, d), BF16), jax.ShapeDtypeStruct(s0.shape, F32)],
        scratch_shapes=[pltpu.VMEM((6, n_tok, pair, n_seq), F32),
                        pltpu.VMEM((n_tok, pair, n_seq), F32)],
        input_output_aliases={14: 1} if prev else {},
        compiler_params=pltpu.CompilerParams(
            dimension_semantics=("arbitrary",), vmem_limit_bytes=VMEM_LIMIT),
        name="rwkv_step",
    )(proj, proj, proj, proj, wm, am, vm, vf_src, w2p, a2p, v2p, prm, s0, ones_blk, *prev)


def _hgrn_head(chunk, tvalid, rows_sl, sl, head, refs, lb, nw, ones_blk):
    (q_ref, f_ref, i_ref, z_ref, st_ref) = refs
    sub = min(HGRN_SUB, chunk)
    shift = int(math.log2(sub))
    fl = f_ref[rows_sl, sl]
    qraw = q_ref[rows_sl, sl]
    val = i_ref[rows_sl, sl]
    sig = _sigmoid(fl)
    f = lb + (1.0 - lb) * sig
    logf = jnp.log(jnp.maximum(f, GATE_FLOOR)) * LOG2_E
    if tvalid < chunk:
        logf = jnp.where(_iota(logf.shape, 0) < tvalid, logf, 0.0)
    kg = (1.0 - lb) * (1.0 - sig)
    qs = qraw * _sigmoid(qraw) * (HGRN_DK ** -0.5)
    gcum = _cumsum_rows(logf)
    yield

    st = st_ref[head]
    o_inter = _mm_nt((qs * jnp.exp2(gcum)).astype(BF16), st.astype(BF16))
    g_last = gcum[chunk - 1:chunk, :]
    k_tail = kg * jnp.exp2(g_last - gcum)
    st_ref[head] = st * jnp.exp2(g_last) + _mm_tn(val.astype(BF16), k_tail.astype(BF16))

    def bcast_rows(x):
        return jnp.concatenate(
            [jnp.broadcast_to(x[t:t + 1, :], (sub, HGRN_DK)) for t in range(tvalid)], axis=0)

    def tile_blocks(x):
        return jnp.concatenate(
            [x[(t >> shift) * sub:((t >> shift) + 1) * sub] for t in range(tvalid)], axis=0)

    rows = tvalid * sub
    prod = bcast_rows(qs) * tile_blocks(kg) * jnp.exp2(
        jnp.minimum(bcast_rows(gcum) - tile_blocks(gcum), 0.0))
    att_diag = _mm(prod.astype(BF16), ones_blk)
    n_blk = -(-tvalid // sub)
    att_off = []
    for bi in range(1, n_blk):
        lo = bi * sub
        g_edge = gcum[lo - 1:lo]
        q_n = (qs[lo:lo + sub] * jnp.exp2(gcum[lo:lo + sub] - g_edge)).astype(BF16)
        k_n = (kg[0:lo] * jnp.exp2(g_edge - gcum[0:lo])).astype(BF16)
        att_off.append(_mm_nt(q_n, k_n))
    yield

    ridx = _iota((rows, HGRN_DK), 0)
    causal = (jnp.bitwise_and(ridx, sub - 1)
              <= jnp.bitwise_and(lax.shift_right_logical(ridx, shift), sub - 1))
    weighted = jnp.where(causal, att_diag * tile_blocks(val), 0.0)
    sel = jnp.where(lax.shift_right_logical(_iota((chunk, rows), 1), shift) == _iota((chunk, rows), 0),
                    1.0, 0.0).astype(BF16)
    o = o_inter + _mm(sel, weighted.astype(BF16))
    if n_blk > 1:
        parts = [jnp.zeros((sub, HGRN_DK), F32)]
        for bi in range(1, n_blk):
            parts.append(_mm(att_off[bi - 1].astype(BF16), val[0:bi * sub].astype(BF16)))
        if n_blk * sub < chunk:
            parts.append(jnp.zeros((chunk - n_blk * sub, HGRN_DK), F32))
        o = o + jnp.concatenate(parts, axis=0)
    yield

    o = o * lax.rsqrt(jnp.mean(o * o, axis=-1, keepdims=True) + NORM_EPS) * nw
    z = z_ref[rows_sl, sl]
    return o * (z * _sigmoid(z))


def _hgrn_chunk_kernel(chunk, tvalid, layer_j, seqs, heads, fill,
                       q_ref, f_ref, i_ref, z_ref, lbl_ref, nw_ref, s0_ref, *rest):
    g_ref, so_ref, st_ref = rest[-3:]
    c = pl.program_id(2)
    n_chunks = pl.num_programs(2)

    @pl.when(c == 0)
    def _init():
        for q in range(seqs):
            for hh in range(heads):
                st_ref[q * heads + hh] = s0_ref[q, hh].T

    logits = lbl_ref[...]
    ex = jnp.exp(logits - jnp.max(logits, axis=0, keepdims=True))
    soft = ex / jnp.sum(ex, axis=0, keepdims=True)
    lb_all = jnp.sum(soft[0:layer_j + 1], axis=0, keepdims=True) - soft[0:1]

    ones_blk = jnp.ones((HGRN_DK, HGRN_DK), BF16)
    nw = nw_ref[...]
    refs = (q_ref, f_ref, i_ref, z_ref, st_ref)
    outs = _round_robin([
        _hgrn_head(chunk, tvalid, slice(q * chunk, (q + 1) * chunk),
                   slice(hh * HGRN_DK, (hh + 1) * HGRN_DK), q * heads + hh, refs,
                   lb_all[:, hh * HGRN_DK:(hh + 1) * HGRN_DK], nw, ones_blk)
        for q in range(seqs) for hh in range(heads)])
    for hh in range(heads):
        col = [outs[q * heads + hh] for q in range(seqs)]
        g_ref[:, hh * HGRN_DK:(hh + 1) * HGRN_DK] = (
            col[0] if seqs == 1 else jnp.concatenate(col, axis=0)).astype(BF16)

    @pl.when(c == n_chunks - 1)
    def _fin():
        for q in range(seqs):
            for hh in range(heads):
                _store_state(so_ref, fill, (q, hh), st_ref[q * heads + hh].T)


def _hgrn_chunk(proj, lb_logits, norm_w, s0, s_out_prev, *, layer_j, n_seq, n_chunks, chunk, tvalid,
                seqs=1):
    assert seqs == 1 or n_chunks == 1
    m = n_seq * n_chunks * chunk
    d = proj.shape[1] // 4
    n_heads = d // HGRN_DK
    heads = HGRN_HEADS_PER_STEP
    n_hb = n_heads // heads
    width = heads * HGRN_DK

    def col(off):
        return pl.BlockSpec((seqs * chunk, width), lambda s, h, c: (s * n_chunks + c, off * n_hb + h))

    state = pl.BlockSpec((None, seqs, heads, HGRN_DK, HGRN_DK), lambda s, h, c: (layer_j, s, h, 0, 0))
    prev = [] if s_out_prev is None else [s_out_prev]
    fill = None if prev else (layer_j, s0.shape[0])
    state_out = state if prev else pl.BlockSpec(
        (s0.shape[0], seqs, heads, HGRN_DK, HGRN_DK), lambda s, h, c: (0, s, h, 0, 0))
    kernel = functools.partial(_hgrn_chunk_kernel, chunk, tvalid, layer_j, seqs, heads, fill)
    return pl.pallas_call(
        kernel,
        grid=(n_seq // seqs, n_hb, n_chunks),
        in_specs=[col(0), col(1), col(2), col(3),
                  pl.BlockSpec((lb_logits.shape[0], width), lambda s, h, c: (0, h)),
                  pl.BlockSpec((1, HGRN_DK), lambda s, h, c: (0, 0)), state]
        + [pl.BlockSpec(memory_space=pl.ANY)] * len(prev),
        out_specs=[pl.BlockSpec((seqs * chunk, width), lambda s, h, c: (s * n_chunks + c, h)),
                   state_out],
        out_shape=[jax.ShapeDtypeStruct((m, d), BF16),
                   jax.ShapeDtypeStruct(s0.shape, F32)],
        scratch_shapes=[pltpu.VMEM((seqs * heads, HGRN_DK, HGRN_DK), F32)],
        input_output_aliases={7: 1} if prev else {},
        compiler_params=pltpu.CompilerParams(
            dimension_semantics=("arbitrary", "arbitrary", "arbitrary"),
            vmem_limit_bytes=VMEM_LIMIT),
        name="hgrn_chunk",
    )(proj, proj, proj, proj, lb_logits, norm_w.reshape(1, HGRN_DK), s0, *prev)


def _pad_lanes(w, axis):
    pad = [(0, 0)] * w.ndim
    pad[axis] = (0, LORA_PAD - w.shape[axis])
    return jnp.pad(w, pad)


def _rwkv_layer_inputs(h, hprev, p, j, tiles_per_seq):
    has_vres = j > 0
    proj = _proj_rwkv(h, hprev, p["mu4"][j], p["w_in"], j,
                      tiles_per_seq * ROW_TILE // min(PROJ_TILE, h.shape[0]))
    lora = _lora(h, hprev, p["mu6"][j], p["w1"][j], p["a1"][j], p["v1"][max(j - 1, 0)],
                 has_vres, tiles_per_seq)
    return (proj,) + tuple(lora) + (has_vres,)


def _trunk_prompt(x, shift0, wkv0, hgrn0, p, *, n_seq, n_tok):
    d = x.shape[1]
    n_chunks = n_tok // PROMPT_CHUNK
    shifts = []
    wkv_out, hgrn_out, vf_src = None, None, None
    h = _rmsnorm(x, p["norm_pre"][0])
    for layer in range(DEPTH):
        j = layer // 2
        npre_next = p["norm_pre"][layer + 1] if layer + 1 < DEPTH else None
        if layer % 2 == 0:
            proj, wm, am, vm, tail, has_vres = _rwkv_layer_inputs(
                h, shift0[j][:, None, :], p, j, n_tok // ROW_TILE)
            shifts.append(tail[:, 7])
            vf_src = proj if vf_src is None else vf_src
            g, wkv_out = _rwkv_chunk(
                proj, vf_src, wm, am, vm, p["w2"][j], p["a2"][j], p["v2"][max(j - 1, 0)],
                p["rwkv_prm"][j], wkv0, wkv_out, layer_j=j, n_seq=n_seq, n_chunks=n_chunks,
                chunk=PROMPT_CHUNK, tvalid=PROMPT_CHUNK, has_vres=has_vres)
            w_o = p["rwkv_w_o"]
        else:
            proj = _proj_hgrn(h, p["hgrn_w_in"], j)
            g, hgrn_out = _hgrn_chunk(
                proj, p["hgrn_lb_logits"], p["hgrn_norm_w"][j], hgrn0, hgrn_out, layer_j=j,
                n_seq=n_seq, n_chunks=n_chunks, chunk=PROMPT_CHUNK, tvalid=PROMPT_CHUNK)
            w_o = p["hgrn_w_o"]
        x, h = _outproj(g, w_o, j, x, p["norm_post"][layer], npre_next,
                        BF16 if layer % 2 == 0 else F32)
    return x, jnp.stack(shifts), wkv_out, hgrn_out


def _trunk_sample(x, shift0, wkv0, hgrn0, p, *, n_seq, n_tok):
    d = x.shape[1]
    tpad = SAMPLE_TPAD

    def to_padded(a):
        a3 = a.reshape(n_tok, n_seq, a.shape[1]).transpose(1, 0, 2)
        return jnp.pad(a3, ((0, 0), (0, tpad - n_tok), (0, 0))).reshape(n_seq * tpad, a.shape[1])

    def from_padded(a):
        a3 = a.reshape(n_seq, tpad, a.shape[1])[:, :n_tok]
        return a3.transpose(1, 0, 2).reshape(n_tok * n_seq, a.shape[1])

    shifts = []
    wkv_out, hgrn_out, vf_src = None, None, None
    h = _rmsnorm(x, p["norm_pre"][0])
    for layer in range(DEPTH):
        j = layer // 2
        npre_next = p["norm_pre"][layer + 1] if layer + 1 < DEPTH else None
        if layer % 2 == 0:
            hprev = jnp.concatenate([shift0[j], h[:(n_tok - 1) * n_seq]], axis=0)
            shifts.append(h[(n_tok - 1) * n_seq:])
            proj, wm, am, vm, has_vres = _rwkv_layer_inputs(h, hprev, p, j, 0)
            vf_src = proj if vf_src is None else vf_src
            g, wkv_out = _rwkv_step(
                proj, vf_src, wm, am, vm, p["w2"][j], p["a2"][j], p["v2"][max(j - 1, 0)],
                p["rwkv_prm"][j], wkv0, wkv_out, layer_j=j, n_tok=n_tok, has_vres=has_vres)
            w_o = p["rwkv_w_o"]
        else:
            proj = to_padded(_proj_hgrn(h, p["hgrn_w_in"], j))
            g, hgrn_out = _hgrn_chunk(
                proj, p["hgrn_lb_logits"], p["hgrn_norm_w"][j], hgrn0, hgrn_out, layer_j=j,
                n_seq=n_seq, n_chunks=1, chunk=tpad, tvalid=n_tok, seqs=HGRN_SAMPLE_SEQS)
            g = from_padded(g)
            w_o = p["hgrn_w_o"]
        x, h = _outproj(g, w_o, j, x, p["norm_post"][layer], npre_next,
                        BF16 if layer % 2 == 0 else F32)
    return x, jnp.stack(shifts), wkv_out, hgrn_out


def kernel(x_prompt, x_sample, state_rwkv_shift, state_rwkv_wkv, state_hgrn, norm_pre, norm_post,
           rwkv_mu, rwkv_w_in, rwkv_w0, rwkv_w1, rwkv_w2, rwkv_a0, rwkv_a1, rwkv_a2, rwkv_v0,
           rwkv_v1, rwkv_v2, rwkv_k_k, rwkv_k_a, rwkv_r_k, rwkv_ln_w, rwkv_ln_b, rwkv_w_o,
           hgrn_w_in, hgrn_lb_logits, hgrn_norm_w, hgrn_w_o):
    n_rwkv = rwkv_mu.shape[0]
    d = x_prompt.shape[-1]
    v0_full = jnp.concatenate([jnp.zeros((1, d), F32), rwkv_v0], axis=0)
    rwkv_prm = jnp.stack([rwkv_w0, rwkv_a0, v0_full, rwkv_k_k, rwkv_k_a,
                          rwkv_r_k.reshape(n_rwkv, d), rwkv_ln_w, rwkv_ln_b], axis=1)
    p = {
        "norm_pre": norm_pre, "norm_post": norm_post,
        "mu4": rwkv_mu[:, :4, None, :], "mu6": rwkv_mu,
        "w_in": rwkv_w_in.astype(BF16),
        "w1": _pad_lanes(rwkv_w1, 2).astype(BF16), "a1": _pad_lanes(rwkv_a1, 2).astype(BF16),
        "v1": _pad_lanes(rwkv_v1, 2).astype(BF16),
        "w2": _pad_lanes(rwkv_w2, 1).astype(BF16), "a2": _pad_lanes(rwkv_a2, 1).astype(BF16),
        "v2": _pad_lanes(rwkv_v2, 1).astype(BF16),
        "rwkv_prm": rwkv_prm, "rwkv_w_o": rwkv_w_o.astype(BF16),
        "hgrn_w_in": hgrn_w_in.astype(BF16), "hgrn_lb_logits": hgrn_lb_logits,
        "hgrn_norm_w": hgrn_norm_w, "hgrn_w_o": hgrn_w_o.astype(BF16),
    }

    bp, tp, _ = x_prompt.shape
    zero_shift = jnp.zeros((n_rwkv, bp, d), F32)
    zero_wkv = jnp.zeros((n_rwkv, bp) + state_rwkv_wkv.shape[2:], F32)
    zero_hgrn = jnp.zeros((state_hgrn.shape[0], bp) + state_hgrn.shape[2:], F32)
    y_p, p_shift, p_wkv, p_hgrn = _trunk_prompt(
        x_prompt.reshape(bp * tp, d), zero_shift, zero_wkv, zero_hgrn, p, n_seq=bp, n_tok=tp)

    bs, ts, _ = x_sample.shape
    y_s, s_shift, s_wkv, s_hgrn = _trunk_sample(
        x_sample.transpose(1, 0, 2).reshape(ts * bs, d), state_rwkv_shift,
        state_rwkv_wkv.transpose(0, 2, 3, 4, 1), state_hgrn, p, n_seq=bs, n_tok=ts)

    return (y_p.reshape(bp, tp, d), y_s.reshape(ts, bs, d).transpose(1, 0, 2),
            p_shift, p_wkv, p_hgrn, s_shift, s_wkv.transpose(0, 4, 1, 2, 3), s_hgrn)
```
